```python
import math
import jax
import jax.numpy as jnp
from jax import lax
import numpy as np

D_MODEL = 4096
BATCH = 4
SEQ = 2048
DEPTH = 2
DEC_BATCH = 8
DEC_SEQ = 4
PAST_LEN = 16384
PAGE_SIZE = 128

N_META = 16
HEAD_DIM = 128
HALF = HEAD_DIM // 2
H_A = 12
H_B = 12
W_A = H_A * HEAD_DIM
W_B = H_B * HEAD_DIM
W_C = D_MODEL - W_A - W_B
MIX = W_A + W_B + W_C
N_IN = 3 * W_A + 4 * W_B + 2 * W_C
CONV_W = 31
D_FF = ((8 * D_MODEL + 767) // 768) * 256
N_BUCKETS = 32
REL_EXACT = 16
REL_MAX_DIST = 128
CHUNK = 128
Q_BLOCK = 128
ROPE_BASE = 10000.0
EPS = 1e-6
NEG_INF = -1e30

kernel_name = 'hybrid_diffattn_retention_conformer_step'


def rms_norm(x, g):
    xf = x.astype(jnp.float32)
    y = xf * lax.rsqrt(jnp.mean(xf * xf, axis=-1, keepdims=True) + EPS)
    return (y * g.astype(jnp.float32)).astype(x.dtype)


def layer_norm(x, g, b):
    xf = x.astype(jnp.float32)
    xc = xf - jnp.mean(xf, axis=-1, keepdims=True)
    var = jnp.mean(xc * xc, axis=-1, keepdims=True)
    return (xc * lax.rsqrt(var + EPS) * g.astype(jnp.float32) + b.astype(jnp.float32)).astype(x.dtype)


def t5_bucket(dist):
    n = jnp.maximum(dist, 0)
    nf = jnp.maximum(n, 1).astype(jnp.float32)
    large = REL_EXACT + (jnp.log(nf / REL_EXACT) / math.log(REL_MAX_DIST / REL_EXACT)
                         * (N_BUCKETS - REL_EXACT)).astype(jnp.int32)
    return jnp.where(n < REL_EXACT, n, jnp.minimum(large, N_BUCKETS - 1))


def rotary(x, pos):
    inv = ROPE_BASE ** (-jnp.arange(HALF, dtype=jnp.float32) / HALF)
    ang = pos.astype(jnp.float32)[:, None] * inv[None, :]
    cos = jnp.cos(ang)[None, :, None, :]
    sin = jnp.sin(ang)[None, :, None, :]
    xf = x.astype(jnp.float32)
    x1, x2 = xf[..., :HALF], xf[..., HALF:]
    return jnp.concatenate([x1 * cos - x2 * sin, x1 * sin + x2 * cos], axis=-1)


def diff_attend(q1, q2, q_pos, k1, k2, v, k_pos, rel_bias, lam):
    dist = q_pos[:, None] - k_pos[None, :]
    bias = jnp.transpose(rel_bias.astype(jnp.float32)[t5_bucket(dist)], (2, 0, 1))
    bias = jnp.where(dist[None] >= 0, bias, NEG_INF)
    scale = HALF ** -0.5

    def attn_map(q, k):
        s = jnp.einsum('bqhd,bkhd->bhqk', q, k, preferred_element_type=jnp.float32) * scale + bias
        return jax.nn.softmax(s, axis=-1)

    a = attn_map(q1, k1) - lam * attn_map(q2, k2)
    return jnp.einsum('bhqk,bkhd->bqhd', a.astype(v.dtype), v)


def diff_attend_blocked(q1, q2, k1, k2, v, pos, rel_bias, lam):
    B, T = q1.shape[:2]
    n_blk = -(-T // Q_BLOCK)
    pad = n_blk * Q_BLOCK - T

    def blocks(a):
        a = jnp.pad(a, ((0, 0), (0, pad), (0, 0), (0, 0)))
        return jnp.swapaxes(a.reshape(B, n_blk, Q_BLOCK, H_A, HALF), 0, 1)

    q_pos = jnp.arange(n_blk * Q_BLOCK, dtype=jnp.int32).reshape(n_blk, Q_BLOCK)
    out = lax.map(lambda blk: diff_attend(blk[0], blk[1], blk[2], k1, k2, v, pos, rel_bias, lam),
                  (blocks(q1), blocks(q2), q_pos))
    return jnp.swapaxes(out, 0, 1).reshape(B, n_blk * Q_BLOCK, H_A, HEAD_DIM)[:, :T]


def retention_chunk(q, k, v, state, log_gamma):
    L = q.shape[1]
    idx = jnp.arange(L, dtype=jnp.float32)
    rel = idx[:, None] - idx[None, :]
    decay = jnp.where(rel[None] >= 0,
                      jnp.exp(log_gamma[:, None, None] * jnp.maximum(rel, 0.0)[None]), 0.0)
    scores = jnp.einsum('bihd,bjhd->bhij', q, k) * decay[None]
    o = jnp.einsum('bhij,bjhe->bihe', scores, v)
    o = o + jnp.einsum('bihd,bhde->bihe', q, state) * jnp.exp(
        (idx + 1.0)[:, None] * log_gamma[None, :])[None, :, :, None]
    k_dec = k * jnp.exp((L - 1.0 - idx)[:, None] * log_gamma[None, :])[None, :, :, None]
    new_state = jnp.exp(L * log_gamma)[None, :, None, None] * state + jnp.einsum('bjhd,bjhe->bhde', k_dec, v)
    return o, new_state


def retention_prompt(q, k, v, log_gamma):
    B = q.shape[0]
    s0 = jnp.zeros((B, H_B, HEAD_DIM, HEAD_DIM), jnp.float32)
    o_meta, s = retention_chunk(q[:, :N_META], k[:, :N_META], v[:, :N_META], s0, log_gamma)

    def chunks(a):
        a = a[:, N_META:]
        return jnp.swapaxes(a.reshape(B, a.shape[1] // CHUNK, CHUNK, H_B, HEAD_DIM), 0, 1)

    def step(state, qkv):
        o, state = retention_chunk(qkv[0], qkv[1], qkv[2], state, log_gamma)
        return state, o

    s, o_real = lax.scan(step, s, (chunks(q), chunks(k), chunks(v)))
    o_real = jnp.swapaxes(o_real, 0, 1).reshape(B, -1, H_B, HEAD_DIM)
    return jnp.concatenate([o_meta, o_real], axis=1), s


def depthwise_causal_conv(u_ctx, w, b):
    y = lax.conv_general_dilated(u_ctx, w[:, None, :].astype(u_ctx.dtype), window_strides=(1,),
                                 padding='VALID', dimension_numbers=('NWC', 'WIO', 'NWC'),
                                 feature_group_count=u_ctx.shape[-1])
    return y + b


def split_projection(n, w):
    sizes = [W_A, W_A, W_A, W_B, W_B, W_B, W_B, W_C, W_C]
    cuts = [int(c) for c in np.cumsum(sizes)[:-1]]
    return jnp.split(n @ w, cuts, axis=-1)


def mixer(n, pos, lam, lam_init, w_in, subln_a, gn_b, conv_w, conv_b, conv_ln_g, conv_ln_b, w_out,
          rel_bias, log_gamma, kv_past, ret_state, conv_ctx):
    B, L = n.shape[:2]
    qa, ka, va, qb, kb, vb, gb, ca, cb = split_projection(n, w_in)
    qa = qa.reshape(B, L, H_A, HEAD_DIM)
    ka = ka.reshape(B, L, H_A, HEAD_DIM)
    va = va.reshape(B, L, H_A, HEAD_DIM)
    if kv_past is None:
        o_a = diff_attend_blocked(qa[..., :HALF], qa[..., HALF:], ka[..., :HALF], ka[..., HALF:], va,
                                  pos, rel_bias, lam)
    else:
        k_all = jnp.concatenate([kv_past[0].astype(ka.dtype), ka], axis=1)
        v_all = jnp.concatenate([kv_past[1].astype(va.dtype), va], axis=1)
        k_pos = jnp.arange(k_all.shape[1], dtype=jnp.int32)
        o_a = diff_attend(qa[..., :HALF], qa[..., HALF:], pos, k_all[..., :HALF], k_all[..., HALF:],
                          v_all, k_pos, rel_bias, lam)
    y_a = (rms_norm(o_a, subln_a) * (1.0 - lam_init)).reshape(B, L, W_A).astype(n.dtype)
    q_r = rotary(qb.reshape(B, L, H_B, HEAD_DIM), pos)
    k_r = rotary(kb.reshape(B, L, H_B, HEAD_DIM), pos) * (HEAD_DIM ** -0.5)
    v_r = vb.reshape(B, L, H_B, HEAD_DIM).astype(jnp.float32)
    if ret_state is None:
        o_r, new_ret = retention_prompt(q_r, k_r, v_r, log_gamma)
    else:
        o_r, new_ret = retention_chunk(q_r, k_r, v_r, ret_state.astype(jnp.float32), log_gamma)
    y_b = rms_norm(o_r, gn_b).reshape(B, L, W_B).astype(n.dtype) * jax.nn.silu(gb)
    u = ca * jax.nn.sigmoid(cb)
    u_ctx = jnp.concatenate([conv_ctx.astype(u.dtype), u], axis=1)
    c = depthwise_causal_conv(u_ctx, conv_w, conv_b)
    y_c = jax.nn.silu(layer_norm(c, conv_ln_g, conv_ln_b)).astype(n.dtype)
    y = jnp.concatenate([y_a, y_b, y_c], axis=-1) @ w_out
    return y, (ka, va, new_ret.astype(n.dtype), u_ctx[:, -(CONV_W - 1):])


def swiglu(n, w_gate, w_up, w_down):
    return (jax.nn.silu(n @ w_gate) * (n @ w_up)) @ w_down


def setup_inputs(seed: int = 0) -> dict:
    key = jax.random.key(seed)
    ks = jax.random.split(key, 32)
    f32 = jnp.float32
    n_pages = PAST_LEN // PAGE_SIZE
    n_used = DEC_BATCH * n_pages
    n_pool = n_used + max(1, n_used // 4)

    def nrm(k, shape, scale):
        return jax.random.normal(k, shape, f32) * scale

    def gain(k, shape):
        return 1.0 + 0.01 * jax.random.normal(k, shape, f32)

    perm = jax.random.permutation(ks[6], n_pool)
    page_table = perm[:n_used].reshape(DEC_BATCH, n_pages).astype(jnp.int32)
    return {
        'x_prompt': nrm(ks[0], (BATCH, SEQ, D_MODEL), 1.0),
        'x_sample': nrm(ks[1], (DEC_BATCH, DEC_SEQ, D_MODEL), 1.0),
        'cache_k': nrm(ks[2], (DEPTH, n_pool, PAGE_SIZE, H_A, HEAD_DIM), 1.0),
        'cache_v': nrm(ks[3], (DEPTH, n_pool, PAGE_SIZE, H_A, HEAD_DIM), 1.0),
        'state_ret': nrm(ks[4], (DEPTH, DEC_BATCH, H_B, HEAD_DIM, HEAD_DIM), 1.0),
        'state_conv': nrm(ks[5], (DEPTH, DEC_BATCH, CONV_W - 1, W_C), 0.5),
        'page_table': page_table,
        'meta': nrm(ks[7], (N_META, D_MODEL), 1.0),
        'rel_bias': nrm(ks[8], (N_BUCKETS, H_A), 0.5),
        'norm_mix': gain(ks[9], (DEPTH, D_MODEL)),
        'w_in': nrm(ks[10], (DEPTH, D_MODEL, N_IN), D_MODEL ** -0.5),
        'lam_q1': nrm(ks[11], (DEPTH, HALF), 0.1),
        'lam_k1': nrm(ks[12], (DEPTH, HALF), 0.1),
        'lam_q2': nrm(ks[13], (DEPTH, HALF), 0.1),
        'lam_k2': nrm(ks[14], (DEPTH, HALF), 0.1),
        'subln_a': gain(ks[15], (DEPTH, HEAD_DIM)),
        'gn_b': gain(ks[16], (DEPTH, HEAD_DIM)),
        'conv_w': nrm(ks[17], (DEPTH, CONV_W, W_C), CONV_W ** -0.5),
        'conv_b': nrm(ks[18], (DEPTH, W_C), 0.01),
        'conv_ln_g': gain(ks[19], (DEPTH, W_C)),
        'conv_ln_b': nrm(ks[20], (DEPTH, W_C), 0.01),
        'w_out': nrm(ks[21], (DEPTH, MIX, D_MODEL), MIX ** -0.5),
        'norm_ffn': gain(ks[22], (DEPTH, D_MODEL)),
        'w_gate': nrm(ks[23], (DEPTH, D_MODEL, D_FF), D_MODEL ** -0.5),
        'w_up': nrm(ks[24], (DEPTH, D_MODEL, D_FF), D_MODEL ** -0.5),
        'w_down': nrm(ks[25], (DEPTH, D_FF, D_MODEL), D_FF ** -0.5),
        'norm_final': gain(ks[26], (D_MODEL,)),
    }


def reference(x_prompt, x_sample, cache_k, cache_v, state_ret, state_conv, page_table, meta, rel_bias,
              norm_mix, w_in, lam_q1, lam_k1, lam_q2, lam_k2, subln_a, gn_b, conv_w, conv_b, conv_ln_g,
              conv_ln_b, w_out, norm_ffn, w_gate, w_up, w_down, norm_final):
    f32 = jnp.float32
    B, S = x_prompt.shape[:2]
    DB, L = x_sample.shape[:2]
    past = page_table.shape[1] * PAGE_SIZE
    pos_p = jnp.arange(S + N_META, dtype=jnp.int32)
    pos_s = past + jnp.arange(L, dtype=jnp.int32)
    log_gamma = jnp.log1p(-jnp.exp2(-5.0 - jnp.arange(H_B, dtype=f32)))
    h_p = jnp.concatenate([jnp.broadcast_to(meta[None].astype(x_prompt.dtype), (B, N_META, D_MODEL)),
                           x_prompt], axis=1)
    h_s = x_sample
    new_p, new_s = [], []
    for l in range(DEPTH):
        lam_init = 0.8 - 0.6 * math.exp(-0.3 * l)
        lam = (jnp.exp(jnp.sum(lam_q1[l].astype(f32) * lam_k1[l].astype(f32)))
               - jnp.exp(jnp.sum(lam_q2[l].astype(f32) * lam_k2[l].astype(f32))) + lam_init)
        shared = (lam, lam_init, w_in[l], subln_a[l], gn_b[l], conv_w[l], conv_b[l], conv_ln_g[l],
                  conv_ln_b[l], w_out[l], rel_bias, log_gamma)
        y, st = mixer(rms_norm(h_p, norm_mix[l]), pos_p, *shared, None, None,
                      jnp.zeros((B, CONV_W - 1, W_C), h_p.dtype))
        h_p = h_p + y
        h_p = h_p + swiglu(rms_norm(h_p, norm_ffn[l]), w_gate[l], w_up[l], w_down[l])
        new_p.append(st)
        k_past = cache_k[l][page_table].reshape(DB, past, H_A, HEAD_DIM)
        v_past = cache_v[l][page_table].reshape(DB, past, H_A, HEAD_DIM)
        y, st = mixer(rms_norm(h_s, norm_mix[l]), pos_s, *shared, (k_past, v_past), state_ret[l],
                      state_conv[l])
        h_s = h_s + y
        h_s = h_s + swiglu(rms_norm(h_s, norm_ffn[l]), w_gate[l], w_up[l], w_down[l])
        new_s.append(st)
    y_prompt = rms_norm(h_p, norm_final)[:, N_META:]
    y_sample = rms_norm(h_s, norm_final)
    k_prompt = jnp.stack([s[0] for s in new_p])
    v_prompt = jnp.stack([s[1] for s in new_p])
    ret_prompt = jnp.stack([s[2] for s in new_p])
    conv_prompt = jnp.stack([s[3] for s in new_p])
    k_sample = jnp.stack([s[0] for s in new_s])
    v_sample = jnp.stack([s[1] for s in new_s])
    ret_sample = jnp.stack([s[2] for s in new_s])
    conv_sample = jnp.stack([s[3] for s in new_s])
    return (y_prompt, y_sample, k_prompt, v_prompt, ret_prompt, conv_prompt,
            k_sample, v_sample, ret_sample, conv_sample)
```

```python
import functools
import math

import jax
import jax.numpy as jnp
from jax import lax
from jax.experimental import pallas as pl
from jax.experimental.pallas import tpu as pltpu

F32 = jnp.float32
BF16 = jnp.bfloat16

D_MODEL = 4096
N_META = 16
HEAD_DIM = 128
HALF = HEAD_DIM // 2
H_A = 12
H_B = 12
W_A = H_A * HEAD_DIM
W_B = H_B * HEAD_DIM
W_C = D_MODEL - W_A - W_B
N_IN = 3 * W_A + 4 * W_B + 2 * W_C
CONV_W = 31
N_BUCKETS = 32
REL_EXACT = 16
REL_MAX_DIST = 128
ROPE_BASE = 10000.0
EPS = 1e-6
NEG_INF = -1e30
PAGE_SIZE = 128

BLK = 128
SROWS = 16
CTX_ROWS = 32
M_INIT = -3.0e38
VMEM_LIMIT = 56 * 1024 * 1024

COL_QA, COL_KA, COL_VA = 0, W_A // 128, 2 * W_A // 128
COL_QB = 3 * W_A // 128
COL_KB = COL_QB + W_B // 128
COL_VB = COL_KB + W_B // 128
COL_GB = COL_VB + W_B // 128
COL_CA = COL_GB + W_B // 128
COL_CB = COL_CA + W_C // 128


def _cparams(sem, vmem=None):
    return pltpu.CompilerParams(dimension_semantics=sem, vmem_limit_bytes=vmem)


def _sigmoid(x):
    return 1.0 / (1.0 + jnp.exp(-x))


def _rmsnorm_body(x_ref, g_ref, o_ref):
    x = x_ref[...]
    ms = jnp.mean(x * x, axis=-1, keepdims=True)
    o_ref[...] = (x * lax.rsqrt(ms + EPS) * g_ref[...]).astype(o_ref.dtype)


def _rmsnorm(x, g, out_dtype, tr, name):
    m, d = x.shape
    return pl.pallas_call(
        _rmsnorm_body,
        grid=(m // tr,),
        in_specs=[pl.BlockSpec((tr, d), lambda i: (i, 0)),
                  pl.BlockSpec((1, d), lambda i: (0, 0))],
        out_specs=pl.BlockSpec((tr, d), lambda i: (i, 0)),
        out_shape=jax.ShapeDtypeStruct((m, d), out_dtype),
        compiler_params=_cparams(("parallel",)),
        name=name,
    )(x, g.reshape(1, d))


def _mm_body(a_ref, w_ref, o_ref):
    o_ref[...] = jnp.dot(a_ref[...], w_ref[...].astype(BF16), preferred_element_type=F32)


def _mm_res_body(a_ref, w_ref, r_ref, o_ref):
    o_ref[...] = r_ref[...] + jnp.dot(a_ref[...], w_ref[...].astype(BF16),
                                      preferred_element_type=F32)


def _matmul(a, w, layer, tm, tn, name, res=None):
    m, k = a.shape
    n = w.shape[2]
    in_specs = [pl.BlockSpec((tm, k), lambda i, j: (i, 0)),
                pl.BlockSpec((None, k, tn), lambda i, j: (layer, 0, j))]
    args = [a, w]
    body = _mm_body
    if res is not None:
        in_specs.append(pl.BlockSpec((tm, tn), lambda i, j: (i, j)))
        args.append(res)
        body = _mm_res_body
    return pl.pallas_call(
        body,
        grid=(m // tm, n // tn),
        in_specs=in_specs,
        out_specs=pl.BlockSpec((tm, tn), lambda i, j: (i, j)),
        out_shape=jax.ShapeDtypeStruct((m, n), F32),
        compiler_params=_cparams(("parallel", "parallel"), VMEM_LIMIT),
        name=name,
    )(*args)


def _ffn_up_body(a_ref, wg_ref, wu_ref, o_ref):
    a = a_ref[...]
    g = jnp.dot(a, wg_ref[...].astype(BF16), preferred_element_type=F32)
    u = jnp.dot(a, wu_ref[...].astype(BF16), preferred_element_type=F32)
    o_ref[...] = (g * _sigmoid(g) * u).astype(o_ref.dtype)


def _ffn_up(a, wg, wu, layer, tm, tn, name):
    m, k = a.shape
    n = wg.shape[2]
    wspec = pl.BlockSpec((None, k, tn), lambda i, j: (layer, 0, j))
    return pl.pallas_call(
        _ffn_up_body,
        grid=(m // tm, n // tn),
        in_specs=[pl.BlockSpec((tm, k), lambda i, j: (i, 0)), wspec, wspec],
        out_specs=pl.BlockSpec((tm, tn), lambda i, j: (i, j)),
        out_shape=jax.ShapeDtypeStruct((m, n), BF16),
        compiler_params=_cparams(("parallel", "parallel"), VMEM_LIMIT),
        name=name,
    )(a, wg, wu)


def _ffn_down_body(a_ref, w_ref, r_ref, o_ref, acc_ref):
    kk = pl.program_id(2)

    @pl.when(kk == 0)
    def _():
        acc_ref[...] = r_ref[...]

    acc_ref[...] += jnp.dot(a_ref[...], w_ref[...].astype(BF16), preferred_element_type=F32)

    @pl.when(kk == pl.num_programs(2) - 1)
    def _():
        o_ref[...] = acc_ref[...]


def _ffn_down(a, w, res, layer, tm, tn, tk, name):
    m, k = a.shape
    n = w.shape[2]
    return pl.pallas_call(
        _ffn_down_body,
        grid=(m // tm, n // tn, k // tk),
        in_specs=[pl.BlockSpec((tm, tk), lambda i, j, kk: (i, kk)),
                  pl.BlockSpec((None, tk, tn), lambda i, j, kk: (layer, kk, j)),
                  pl.BlockSpec((tm, tn), lambda i, j, kk: (i, j))],
        out_specs=pl.BlockSpec((tm, tn), lambda i, j, kk: (i, j)),
        out_shape=jax.ShapeDtypeStruct((m, n), F32),
        scratch_shapes=[pltpu.VMEM((tm, tn), F32)],
        compiler_params=_cparams(("parallel", "parallel", "arbitrary"), VMEM_LIMIT),
        name=name,
    )(a, w, res)


def _lambda(lamp_ref, lam_init):
    lp = lamp_ref[...]
    s1 = jnp.sum(lp[0:1] * lp[1:2], axis=-1, keepdims=True)
    s2 = jnp.sum(lp[2:3] * lp[3:4], axis=-1, keepdims=True)
    return jnp.exp(s1) - jnp.exp(s2) + lam_init


def _split_maps(q):
    lane = lax.broadcasted_iota(jnp.int32, q.shape, 1)
    lo = jnp.where(lane < HALF, q, 0.0)
    hi = jnp.where(lane >= HALF, q, 0.0)
    return jnp.concatenate([lo, hi], axis=0)


def _subln(a, sub_ref, lam_init):
    y = a * lax.rsqrt(jnp.mean(a * a, axis=-1, keepdims=True) + EPS) * sub_ref[...]
    return y * (1.0 - lam_init)


def _attn_prompt_body(q_ref, k_ref, v_ref, band_ref, lamp_ref, sub_ref, o_ref, kb_ref, vb_ref,
                      *, lam_init, nblk):
    qi = pl.program_id(2)
    tp = nblk * BLK

    @pl.when(qi == 0)
    def _():
        kb_ref[...] = k_ref[...].astype(BF16)
        vb_ref[...] = v_ref[...].astype(BF16)

    qs = _split_maps(q_ref[...] * (HALF ** -0.5)).astype(BF16)
    s = lax.dot_general(qs, kb_ref[...], (((1,), (1,)), ((), ())),
                        preferred_element_type=F32)
    off = pl.multiple_of((nblk - 1 - qi) * BLK, BLK)
    band = band_ref[:, pl.ds(off, tp)]
    s = s + jnp.concatenate([band, band], axis=0)
    m = jnp.max(s, axis=1, keepdims=True)
    p = jnp.exp(s - m)
    l = jnp.sum(p, axis=1, keepdims=True)
    o = jnp.dot(p.astype(BF16), vb_ref[...], preferred_element_type=F32) / l
    a = o[:BLK] - _lambda(lamp_ref, lam_init) * o[BLK:]
    o_ref[...] = _subln(a, sub_ref, lam_init).astype(o_ref.dtype)


def _attn_prompt(proj, band, lamp, sub, nb, nblk, lam_init, name):
    tp = nblk * BLK
    body = functools.partial(_attn_prompt_body, lam_init=lam_init, nblk=nblk)
    return pl.pallas_call(
        body,
        grid=(nb, H_A, nblk),
        in_specs=[
            pl.BlockSpec((BLK, HEAD_DIM), lambda b, h, qi: (b * nblk + qi, COL_QA + h)),
            pl.BlockSpec((tp, HEAD_DIM), lambda b, h, qi: (b, COL_KA + h)),
            pl.BlockSpec((tp, HEAD_DIM), lambda b, h, qi: (b, COL_VA + h)),
            pl.BlockSpec((None, BLK, band.shape[2]), lambda b, h, qi: (h, 0, 0)),
            pl.BlockSpec((4, HALF), lambda b, h, qi: (0, 0)),
            pl.BlockSpec((1, HEAD_DIM), lambda b, h, qi: (0, 0)),
        ],
        out_specs=pl.BlockSpec((BLK, HEAD_DIM), lambda b, h, qi: (b * nblk + qi, h)),
        out_shape=jax.ShapeDtypeStruct((nb * tp, W_A), BF16),
        scratch_shapes=[pltpu.VMEM((tp, HEAD_DIM), BF16), pltpu.VMEM((tp, HEAD_DIM), BF16)],
        compiler_params=_cparams(("parallel", "parallel", "arbitrary"), VMEM_LIMIT),
        name=name,
    )(proj, proj, proj, band, lamp, sub)


def _attn_sample_body(pt_ref, q_ref, kn_ref, vn_ref, kp_ref, vp_ref, bias_ref, lamp_ref, sub_ref,
                      o_ref, qs_ref, m_ref, l_ref, acc_ref, *, lam_init, n_pages, n_new):
    p = pl.program_id(1)

    @pl.when(p == 0)
    def _():
        for h in range(H_A):
            q = q_ref[0:n_new, h * HEAD_DIM:(h + 1) * HEAD_DIM] * (HALF ** -0.5)
            qs = _split_maps(q)
            pad = jnp.zeros((SROWS - 2 * n_new, HEAD_DIM), F32)
            qs_ref[h] = jnp.concatenate([qs, pad], axis=0).astype(BF16)
        m_ref[...] = jnp.full(m_ref.shape, M_INIT, F32)
        l_ref[...] = jnp.zeros(l_ref.shape, F32)
        acc_ref[...] = jnp.zeros(acc_ref.shape, F32)

    def step(get_k, get_v):
        for h in range(H_A):
            s = lax.dot_general(qs_ref[h], get_k(h).astype(BF16), (((1,), (1,)), ((), ())),
                                preferred_element_type=F32) + bias_ref[h]
            m_prev = m_ref[h]
            m_new = jnp.maximum(m_prev, jnp.max(s, axis=1, keepdims=True))
            alpha = jnp.exp(m_prev - m_new)
            pm = jnp.exp(s - m_new)
            l_ref[h] = alpha * l_ref[h] + jnp.sum(pm, axis=1, keepdims=True)
            acc_ref[h] = alpha * acc_ref[h] + jnp.dot(pm.astype(BF16), get_v(h).astype(BF16),
                                                      preferred_element_type=F32)
            m_ref[h] = m_new

    @pl.when(p < n_pages)
    def _():
        step(lambda h: kp_ref[:, h, :], lambda h: vp_ref[:, h, :])

    @pl.when(p == n_pages)
    def _():
        zpad = jnp.zeros((PAGE_SIZE - SROWS, HEAD_DIM), F32)

        def new_rows(ref, h):
            return jnp.concatenate([ref[:, h * HEAD_DIM:(h + 1) * HEAD_DIM], zpad], axis=0)

        step(lambda h: new_rows(kn_ref, h), lambda h: new_rows(vn_ref, h))
        lam = _lambda(lamp_ref, lam_init)
        for h in range(H_A):
            o = acc_ref[h] / l_ref[h]
            a = o[0:n_new] - lam * o[n_new:2 * n_new]
            y = _subln(a, sub_ref, lam_init)
            pad = jnp.zeros((SROWS - n_new, HEAD_DIM), F32)
            o_ref[:, h * HEAD_DIM:(h + 1) * HEAD_DIM] = (
                jnp.concatenate([y, pad], axis=0).astype(o_ref.dtype))


def _attn_sample(page_table, proj, cache_k, cache_v, bias, lamp, sub, layer, row_blk0, n_new,
                 lam_init, name):
    db, n_pages = page_table.shape
    body = functools.partial(_attn_sample_body, lam_init=lam_init, n_pages=n_pages, n_new=n_new)
    page_spec = pl.BlockSpec(
        (None, None, PAGE_SIZE, H_A, HEAD_DIM),
        lambda b, p, pt: (layer, pt[b, jnp.minimum(p, n_pages - 1)], 0, 0, 0))
    grid_spec = pltpu.PrefetchScalarGridSpec(
        num_scalar_prefetch=1,
        grid=(db, n_pages + 1),
        in_specs=[
            pl.BlockSpec((SROWS, W_A), lambda b, p, pt: (row_blk0 + b, 0)),
            pl.BlockSpec((SROWS, W_A), lambda b, p, pt: (row_blk0 + b, 1)),
            pl.BlockSpec((SROWS, W_A), lambda b, p, pt: (row_blk0 + b, 2)),
            page_spec, page_spec,
            pl.BlockSpec((None, H_A, SROWS, PAGE_SIZE), lambda b, p, pt: (p, 0, 0, 0)),
            pl.BlockSpec((4, HALF), lambda b, p, pt: (0, 0)),
            pl.BlockSpec((1, HEAD_DIM), lambda b, p, pt: (0, 0)),
        ],
        out_specs=pl.BlockSpec((SROWS, W_A), lambda b, p, pt: (b, 0)),
        scratch_shapes=[pltpu.VMEM((H_A, SROWS, HEAD_DIM), BF16),
                        pltpu.VMEM((H_A, SROWS, HEAD_DIM), F32),
                        pltpu.VMEM((H_A, SROWS, HEAD_DIM), F32),
                        pltpu.VMEM((H_A, SROWS, HEAD_DIM), F32)],
    )
    return pl.pallas_call(
        body,
        grid_spec=grid_spec,
        out_shape=jax.ShapeDtypeStruct((db * SROWS, W_A), BF16),
        compiler_params=_cparams(("parallel", "arbitrary"), VMEM_LIMIT),
        name=name,
    )(page_table, proj, proj, proj, cache_k, cache_v, bias, lamp, sub)


RET_HG = 4


def _retention_body(q_ref, k_ref, v_ref, g_ref, c2_ref, s2_ref, lg_ref, gn_ref, s0_ref,
                    y_ref, sout_ref, st_ref, *, rows, n_chunks, last_valid):
    c = pl.program_id(2)

    @pl.when(c == 0)
    def _():
        st_ref[...] = s0_ref[...]

    lc = jnp.where(c == n_chunks - 1, float(last_valid), float(BLK))
    ii = lax.broadcasted_iota(jnp.int32, (BLK, BLK), 0).astype(F32)
    jj = lax.broadcasted_iota(jnp.int32, (BLK, BLK), 1).astype(F32)
    rel = ii - jj

    def rows128(x):
        if rows == BLK:
            return x
        return jnp.concatenate([x, jnp.zeros((BLK - rows, x.shape[1]), x.dtype)], axis=0)

    c2 = rows128(c2_ref[...])
    s2 = rows128(s2_ref[...])

    def rot(x):
        return x * c2 + pltpu.roll(x, HALF, 1) * s2

    for j in range(RET_HG):
        sl = slice(j * HEAD_DIM, (j + 1) * HEAD_DIM)
        lgv = lg_ref[:, sl]
        q = rows128(q_ref[:, sl])
        k = rows128(k_ref[:, sl])
        v = rows128(v_ref[:, sl]).astype(BF16)
        g = g_ref[:, sl]
        qr = rot(q).astype(BF16)
        kr = rot(k) * (HEAD_DIM ** -0.5)
        decay = jnp.where(rel >= 0, jnp.exp(lgv * jnp.maximum(rel, 0.0)), 0.0)
        sc = lax.dot_general(qr, kr.astype(BF16), (((1,), (1,)), ((), ())),
                             preferred_element_type=F32) * decay
        st = st_ref[j]
        o = jnp.dot(sc.astype(BF16), v, preferred_element_type=F32)
        o = o + jnp.dot(qr, st.astype(BF16), preferred_element_type=F32) * jnp.exp((ii + 1.0) * lgv)
        kdec = kr * jnp.where(ii < lc, jnp.exp((lc - 1.0 - ii) * lgv), 0.0)
        st_ref[j] = jnp.exp(lc * lgv) * st + jnp.dot(kdec.T.astype(BF16), v,
                                                     preferred_element_type=F32)
        y = o[:rows]
        y = y * lax.rsqrt(jnp.mean(y * y, axis=-1, keepdims=True) + EPS) * gn_ref[...]
        y_ref[:, sl] = (y * (g * _sigmoid(g))).astype(y_ref.dtype)

    @pl.when(c == n_chunks - 1)
    def _():
        sout_ref[...] = st_ref[...]


def _retention(proj, c2, s2, lgrow, gn, s0, nb, n_chunks, rows, row_blk0, last_valid, name):
    body = functools.partial(_retention_body, rows=rows, n_chunks=n_chunks, last_valid=last_valid)
    wb = RET_HG * HEAD_DIM
    cb = wb // 128

    def in_spec(col0):
        return pl.BlockSpec((rows, wb),
                            lambda b, hg, c: (row_blk0 + b * n_chunks + c, col0 // cb + hg))

    st_spec = pl.BlockSpec((None, RET_HG, HEAD_DIM, HEAD_DIM), lambda b, hg, c: (b, hg, 0, 0))
    return pl.pallas_call(
        body,
        grid=(nb, H_B // RET_HG, n_chunks),
        in_specs=[in_spec(COL_QB), in_spec(COL_KB), in_spec(COL_VB), in_spec(COL_GB),
                  pl.BlockSpec((rows, HEAD_DIM), lambda b, hg, c: (c, 0)),
                  pl.BlockSpec((rows, HEAD_DIM), lambda b, hg, c: (c, 0)),
                  pl.BlockSpec((1, wb), lambda b, hg, c: (0, hg)),
                  pl.BlockSpec((1, HEAD_DIM), lambda b, hg, c: (0, 0)),
                  st_spec],
        out_specs=[pl.BlockSpec((rows, wb), lambda b, hg, c: (b * n_chunks + c, hg)), st_spec],
        out_shape=[jax.ShapeDtypeStruct((nb * n_chunks * rows, W_B), BF16),
                   jax.ShapeDtypeStruct((nb, H_B, HEAD_DIM, HEAD_DIM), F32)],
        scratch_shapes=[pltpu.VMEM((RET_HG, HEAD_DIM, HEAD_DIM), F32)],
        compiler_params=_cparams(("parallel", "parallel", "arbitrary"), VMEM_LIMIT),
        name=name,
    )(proj, proj, proj, proj, c2, s2, lgrow, gn, s0)


def _conv_body(ca0_ref, ca1_ref, cb0_ref, cb1_ref, ctx_ref, w_ref, b_ref, lg_ref, lb_ref,
               y_ref, tail_ref, uc_ref, *, rows, n_chunks, last_valid):
    c = pl.program_id(1)

    @pl.when(c == 0)
    def _():
        uc_ref[0:CTX_ROWS, :] = ctx_ref[...]

    ca = jnp.concatenate([ca0_ref[...], ca1_ref[...]], axis=1)
    cb = jnp.concatenate([cb0_ref[...], cb1_ref[...]], axis=1)
    uc_ref[CTX_ROWS:CTX_ROWS + rows, :] = ca * _sigmoid(cb)
    acc = jnp.broadcast_to(b_ref[...], (rows, W_C))
    for w in range(CONV_W):
        acc = acc + uc_ref[pl.ds(CTX_ROWS - (CONV_W - 1) + w, rows), :] * w_ref[w:w + 1, :]
    mu = jnp.mean(acc, axis=-1, keepdims=True)
    xc = acc - mu
    var = jnp.mean(xc * xc, axis=-1, keepdims=True)
    y = xc * lax.rsqrt(var + EPS) * lg_ref[...] + lb_ref[...]
    y_ref[...] = (y * _sigmoid(y)).astype(y_ref.dtype)

    @pl.when(c == n_chunks - 1)
    def _():
        tail_ref[...] = uc_ref[pl.ds(CTX_ROWS + last_valid - (CONV_W - 1), CONV_W - 1), :]

    nxt = uc_ref[rows:rows + CTX_ROWS, :]
    uc_ref[0:CTX_ROWS, :] = nxt


def _conv(proj, ctx0, conv_w, conv_b, ln_g, ln_b, layer, nb, n_chunks, rows, row_blk0,
          last_valid, name):
    body = functools.partial(_conv_body, rows=rows, n_chunks=n_chunks, last_valid=last_valid)
    half = W_C // 2
    hb = half // 128

    def in_spec(col):
        return pl.BlockSpec((rows, half), lambda b, c: (row_blk0 + b * n_chunks + c, col))

    def par_spec():
        return pl.BlockSpec((None, 1, W_C), lambda b, c: (layer, 0, 0))

    return pl.pallas_call(
        body,
        grid=(nb, n_chunks),
        in_specs=[in_spec(COL_CA // hb), in_spec(COL_CA // hb + 1),
                  in_spec(COL_CB // hb), in_spec(COL_CB // hb + 1),
                  pl.BlockSpec((None, CTX_ROWS, W_C), lambda b, c: (b, 0, 0)),
                  pl.BlockSpec((None, CONV_W, W_C), lambda b, c: (layer, 0, 0)),
                  par_spec(), par_spec(), par_spec()],
        out_specs=[pl.BlockSpec((rows, W_C), lambda b, c: (b * n_chunks + c, 0)),
                   pl.BlockSpec((None, CONV_W - 1, W_C), lambda b, c: (b, 0, 0))],
        out_shape=[jax.ShapeDtypeStruct((nb * n_chunks * rows, W_C), BF16),
                   jax.ShapeDtypeStruct((nb, CONV_W - 1, W_C), F32)],
        scratch_shapes=[pltpu.VMEM((CTX_ROWS + rows, W_C), F32)],
        compiler_params=_cparams(("parallel", "arbitrary"), VMEM_LIMIT),
        name=name,
    )(proj, proj, proj, proj, ctx0, conv_w, conv_b, ln_g, ln_b)


def _t5_bias(rel_bias, dist):
    n = jnp.maximum(dist, 0)
    nf = jnp.maximum(n, 1).astype(F32)
    large = REL_EXACT + (jnp.log(nf / REL_EXACT) / math.log(REL_MAX_DIST / REL_EXACT)
                         * (N_BUCKETS - REL_EXACT)).astype(jnp.int32)
    bucket = jnp.where(n < REL_EXACT, n, jnp.minimum(large, N_BUCKETS - 1))
    bias = jnp.moveaxis(rel_bias.astype(F32)[bucket], -1, 0)
    return jnp.where(dist[None] >= 0, bias, NEG_INF)


def _rotary_tables(pos):
    inv = ROPE_BASE ** (-jnp.arange(HALF, dtype=F32) / HALF)
    ang = pos.astype(F32)[:, None] * inv[None, :]
    cos, sin = jnp.cos(ang), jnp.sin(ang)
    return jnp.concatenate([cos, cos], axis=1), jnp.concatenate([-sin, sin], axis=1)


def kernel(x_prompt, x_sample, cache_k, cache_v, state_ret, state_conv, page_table, meta, rel_bias,
           norm_mix, w_in, lam_q1, lam_k1, lam_q2, lam_k2, subln_a, gn_b, conv_w, conv_b, conv_ln_g,
           conv_ln_b, w_out, norm_ffn, w_gate, w_up, w_down, norm_final):
    nb, seq = x_prompt.shape[:2]
    db, n_new = x_sample.shape[:2]
    depth = w_in.shape[0]
    n_pages = page_table.shape[1]
    past = n_pages * PAGE_SIZE
    t_valid = seq + N_META
    nblk = -(-t_valid // BLK)
    tp = nblk * BLK
    last_valid = t_valid - (nblk - 1) * BLK
    mp = nb * tp
    m_all = mp + db * SROWS
    s_blk0 = mp // SROWS
    d_ff = w_gate.shape[2]

    tm = m_all // 6
    tr = m_all // 24

    hp = jnp.concatenate([jnp.broadcast_to(meta[None].astype(F32), (nb, N_META, D_MODEL)),
                          x_prompt, jnp.zeros((nb, tp - t_valid, D_MODEL), F32)], axis=1)
    hs = jnp.pad(x_sample, ((0, 0), (0, SROWS - n_new), (0, 0)))
    h = jnp.concatenate([hp.reshape(mp, D_MODEL), hs.reshape(db * SROWS, D_MODEL)], axis=0)

    ii = jnp.arange(BLK, dtype=jnp.int32)[:, None]
    cc = jnp.arange((2 * nblk - 1) * BLK, dtype=jnp.int32)[None, :]
    band = _t5_bias(rel_bias, (nblk - 1) * BLK + ii - cc)
    r_new = jnp.arange(SROWS, dtype=jnp.int32) % n_new
    kpos = jnp.arange((n_pages + 1) * PAGE_SIZE, dtype=jnp.int32)
    dist_s = past + r_new[:, None] - kpos[None, :]
    dist_s = jnp.where(kpos[None, :] >= past + n_new, -1, dist_s)
    bias_s = _t5_bias(rel_bias, dist_s).reshape(H_A, SROWS, n_pages + 1, PAGE_SIZE)
    bias_s = jnp.transpose(bias_s, (2, 0, 1, 3))
    c2_p, s2_p = _rotary_tables(jnp.arange(tp, dtype=jnp.int32))
    c2_s, s2_s = _rotary_tables(past + jnp.arange(SROWS, dtype=jnp.int32))
    log_gamma = jnp.log1p(-jnp.exp2(-5.0 - jnp.arange(H_B, dtype=F32)))
    lgrow = jnp.repeat(log_gamma, HEAD_DIM)[None, :]

    zero_state = jnp.zeros((nb, H_B, HEAD_DIM, HEAD_DIM), F32)
    zero_ctx = jnp.zeros((nb, CTX_ROWS, W_C), F32)
    conv_b3 = conv_b[:, None, :]
    ln_g3 = conv_ln_g[:, None, :]
    ln_b3 = conv_ln_b[:, None, :]

    k_p, v_p, ret_p, conv_p, k_s, v_s, ret_s, conv_s = [], [], [], [], [], [], [], []
    for l in range(depth):
        lam_init = 0.8 - 0.6 * math.exp(-0.3 * l)
        lamp = jnp.stack([lam_q1[l], lam_k1[l], lam_q2[l], lam_k2[l]]).astype(F32)
        sub = subln_a[l][None, :]
        gn = gn_b[l][None, :]

        n1 = _rmsnorm(h, norm_mix[l], BF16, tr, f"norm_mix{l}")
        proj = _matmul(n1, w_in, l, tm, 256, f"in_proj{l}")

        ya_p = _attn_prompt(proj, band, lamp, sub, nb, nblk, lam_init, f"attn_prompt{l}")
        ya_s = _attn_sample(page_table, proj, cache_k, cache_v, bias_s, lamp, sub, l, s_blk0,
                            n_new, lam_init, f"attn_sample{l}")
        yb_p, st_p = _retention(proj, c2_p, s2_p, lgrow, gn, zero_state, nb, nblk, BLK, 0,
                                last_valid, f"ret_prompt{l}")
        yb_s, st_s = _retention(proj, c2_s, s2_s, lgrow, gn, state_ret[l], db, 1, SROWS,
                                s_blk0, n_new, f"ret_sample{l}")
        yc_p, tail_p = _conv(proj, zero_ctx, conv_w, conv_b3, ln_g3, ln_b3, l, nb, nblk, BLK, 0,
                             last_valid, f"conv_prompt{l}")
        ctx_s = jnp.pad(state_conv[l], ((0, 0), (CTX_ROWS - (CONV_W - 1), 0), (0, 0)))
        yc_s, tail_s = _conv(proj, ctx_s, conv_w, conv_b3, ln_g3, ln_b3, l, db, 1, SROWS, s_blk0,
                             n_new, f"conv_sample{l}")

        mix = jnp.concatenate([jnp.concatenate([ya_p, ya_s], axis=0),
                               jnp.concatenate([yb_p, yb_s], axis=0),
                               jnp.concatenate([yc_p, yc_s], axis=0)], axis=1)
        h = _matmul(mix, w_out, l, tm, 256, f"out_proj{l}", res=h)
        n2 = _rmsnorm(h, norm_ffn[l], BF16, tr, f"norm_ffn{l}")
        hid = _ffn_up(n2, w_gate, w_up, l, tm, 256, f"ffn_up{l}")
        h = _ffn_down(hid, w_down, h, l, m_all // 8, 256, d_ff // 2, f"ffn_down{l}")

        pp = proj[:mp].reshape(nb, tp, N_IN)[:, :t_valid]
        ps = proj[mp:].reshape(db, SROWS, N_IN)[:, :n_new]
        k_p.append(pp[..., W_A:2 * W_A].reshape(nb, t_valid, H_A, HEAD_DIM))
        v_p.append(pp[..., 2 * W_A:3 * W_A].reshape(nb, t_valid, H_A, HEAD_DIM))
        k_s.append(ps[..., W_A:2 * W_A].reshape(db, n_new, H_A, HEAD_DIM))
        v_s.append(ps[..., 2 * W_A:3 * W_A].reshape(db, n_new, H_A, HEAD_DIM))
        ret_p.append(st_p)
        ret_s.append(st_s)
        conv_p.append(tail_p)
        conv_s.append(tail_s)

    hf = _rmsnorm(h, norm_final, F32, tr, "norm_final")
    y_prompt = hf[:mp].reshape(nb, tp, D_MODEL)[:, N_META:t_valid]
    y_sample = hf[mp:].reshape(db, SROWS, D_MODEL)[:, :n_new]
    return (y_prompt, y_sample, jnp.stack(k_p), jnp.stack(v_p), jnp.stack(ret_p), jnp.stack(conv_p),
            jnp.stack(k_s), jnp.stack(v_s), jnp.stack(ret_s), jnp.stack(conv_s))
```

```python
import functools
import math

import jax
import jax.numpy as jnp
from jax import lax
from jax.experimental import pallas as pl
from jax.experimental.pallas import tpu as pltpu

F32 = jnp.float32
BF16 = jnp.bfloat16

D_MODEL = 4096
N_META = 16
HEAD_DIM = 128
HALF = HEAD_DIM // 2
H_A = 12
H_B = 12
W_A = H_A * HEAD_DIM
W_B = H_B * HEAD_DIM
W_C = D_MODEL - W_A - W_B
N_IN = 3 * W_A + 4 * W_B + 2 * W_C
CONV_W = 31
N_BUCKETS = 32
REL_EXACT = 16
REL_MAX_DIST = 128
ROPE_BASE = 10000.0
EPS = 1e-6
NEG_INF = -1e30
PAGE_SIZE = 128

BLK = 128
SROWS = 16
CTX_ROWS = 32
M_INIT = -3.0e38
VMEM_LIMIT = 56 * 1024 * 1024

COL_QA, COL_KA, COL_VA = 0, W_A // 128, 2 * W_A // 128
COL_QB = 3 * W_A // 128
COL_KB = COL_QB + W_B // 128
COL_VB = COL_KB + W_B // 128
COL_GB = COL_VB + W_B // 128
COL_CA = COL_GB + W_B // 128
COL_CB = COL_CA + W_C // 128


def _cparams(sem, vmem=None):
    return pltpu.CompilerParams(dimension_semantics=sem, vmem_limit_bytes=vmem)


def _sigmoid(x):
    return 1.0 / (1.0 + jnp.exp(-x))


def _rmsnorm_body(x_ref, g_ref, o_ref):
    x = x_ref[...]
    ms = jnp.mean(x * x, axis=-1, keepdims=True)
    o_ref[...] = (x * lax.rsqrt(ms + EPS) * g_ref[...]).astype(o_ref.dtype)


def _rmsnorm(x, g, out_dtype, tr, name):
    m, d = x.shape
    return pl.pallas_call(
        _rmsnorm_body,
        grid=(m // tr,),
        in_specs=[pl.BlockSpec((tr, d), lambda i: (i, 0)),
                  pl.BlockSpec((1, d), lambda i: (0, 0))],
        out_specs=pl.BlockSpec((tr, d), lambda i: (i, 0)),
        out_shape=jax.ShapeDtypeStruct((m, d), out_dtype),
        compiler_params=_cparams(("parallel",)),
        name=name,
    )(x, g.reshape(1, d))


def _mm_body(a_ref, w_ref, o_ref):
    o_ref[...] = jnp.dot(a_ref[...], w_ref[...].astype(BF16), preferred_element_type=F32)


def _mm_res_body(a_ref, w_ref, r_ref, o_ref):
    o_ref[...] = r_ref[...] + jnp.dot(a_ref[...], w_ref[...].astype(BF16),
                                      preferred_element_type=F32)


def _matmul(a, w, layer, tm, tn, name, res=None, tk=None, kblk=0):
    m = a.shape[0]
    k = a.shape[1] if tk is None else tk
    n = w.shape[2]
    in_specs = [pl.BlockSpec((tm, k), lambda i, j: (i, kblk)),
                pl.BlockSpec((None, k, tn), lambda i, j: (layer, kblk, j))]
    args = [a, w]
    body = _mm_body
    if res is not None:
        in_specs.append(pl.BlockSpec((tm, tn), lambda i, j: (i, j)))
        args.append(res)
        body = _mm_res_body
    return pl.pallas_call(
        body,
        grid=(m // tm, n // tn),
        in_specs=in_specs,
        out_specs=pl.BlockSpec((tm, tn), lambda i, j: (i, j)),
        out_shape=jax.ShapeDtypeStruct((m, n), F32),
        compiler_params=_cparams(("parallel", "parallel"), VMEM_LIMIT),
        name=name,
    )(*args)


def _ffn_up_body(a_ref, wg_ref, wu_ref, o_ref):
    a = a_ref[...]
    g = jnp.dot(a, wg_ref[...].astype(BF16), preferred_element_type=F32)
    u = jnp.dot(a, wu_ref[...].astype(BF16), preferred_element_type=F32)
    o_ref[...] = (g * _sigmoid(g) * u).astype(o_ref.dtype)


def _ffn_up(a, wg, wu, layer, tm, tn, name):
    m, k = a.shape
    n = wg.shape[2]
    wspec = pl.BlockSpec((None, k, tn), lambda i, j: (layer, 0, j))
    return pl.pallas_call(
        _ffn_up_body,
        grid=(m // tm, n // tn),
        in_specs=[pl.BlockSpec((tm, k), lambda i, j: (i, 0)), wspec, wspec],
        out_specs=pl.BlockSpec((tm, tn), lambda i, j: (i, j)),
        out_shape=jax.ShapeDtypeStruct((m, n), BF16),
        compiler_params=_cparams(("parallel", "parallel"), VMEM_LIMIT),
        name=name,
    )(a, wg, wu)


def _lambda(lamp_ref, lam_init):
    lp = lamp_ref[...]
    s1 = jnp.sum(lp[0:1] * lp[1:2], axis=-1, keepdims=True)
    s2 = jnp.sum(lp[2:3] * lp[3:4], axis=-1, keepdims=True)
    return jnp.exp(s1) - jnp.exp(s2) + lam_init


def _split_maps(q):
    lane = lax.broadcasted_iota(jnp.int32, q.shape, 1)
    lo = jnp.where(lane < HALF, q, 0.0)
    hi = jnp.where(lane >= HALF, q, 0.0)
    return jnp.concatenate([lo, hi], axis=0)


def _subln(a, sub_ref, lam_init):
    y = a * lax.rsqrt(jnp.mean(a * a, axis=-1, keepdims=True) + EPS) * sub_ref[...]
    return y * (1.0 - lam_init)


def _attn_prompt_body(q_ref, k_ref, v_ref, band_ref, lamp_ref, sub_ref, o_ref, kb_ref, vb_ref,
                      *, lam_init, nblk):
    qi = pl.program_id(2)

    @pl.when(qi == 0)
    def _():
        kb_ref[...] = k_ref[...].astype(BF16)
        vb_ref[...] = v_ref[...].astype(BF16)

    def attend(width):
        qs = _split_maps(q_ref[...] * (HALF ** -0.5)).astype(BF16)
        s = lax.dot_general(qs, kb_ref[0:width, :], (((1,), (1,)), ((), ())),
                            preferred_element_type=F32)
        off = pl.multiple_of((nblk - 1 - qi) * BLK, BLK)
        band = band_ref[:, pl.ds(off, width)]
        s = s + jnp.concatenate([band, band], axis=0)
        m = jnp.max(s, axis=1, keepdims=True)
        p = jnp.exp(s - m)
        l = jnp.sum(p, axis=1, keepdims=True)
        o = jnp.dot(p.astype(BF16), vb_ref[0:width, :], preferred_element_type=F32) / l
        a = o[:BLK] - _lambda(lamp_ref, lam_init) * o[BLK:]
        o_ref[...] = _subln(a, sub_ref, lam_init).astype(o_ref.dtype)

    n_cls = 4
    bounds = sorted({-(-nblk * (c + 1) // n_cls) for c in range(n_cls)})
    lo = 0
    for hi in bounds:
        pl.when(jnp.logical_and(qi >= lo, qi < hi))(functools.partial(attend, hi * BLK))
        lo = hi


def _attn_prompt(proj, band, lamp, sub, nb, nblk, lam_init, name):
    tp = nblk * BLK
    body = functools.partial(_attn_prompt_body, lam_init=lam_init, nblk=nblk)
    return pl.pallas_call(
        body,
        grid=(nb, H_A, nblk),
        in_specs=[
            pl.BlockSpec((BLK, HEAD_DIM), lambda b, h, qi: (b * nblk + qi, COL_QA + h)),
            pl.BlockSpec((tp, HEAD_DIM), lambda b, h, qi: (b, COL_KA + h)),
            pl.BlockSpec((tp, HEAD_DIM), lambda b, h, qi: (b, COL_VA + h)),
            pl.BlockSpec((None, BLK, band.shape[2]), lambda b, h, qi: (h, 0, 0)),
            pl.BlockSpec((4, HALF), lambda b, h, qi: (0, 0)),
            pl.BlockSpec((1, HEAD_DIM), lambda b, h, qi: (0, 0)),
        ],
        out_specs=pl.BlockSpec((BLK, HEAD_DIM), lambda b, h, qi: (b * nblk + qi, h)),
        out_shape=jax.ShapeDtypeStruct((nb * tp, W_A), BF16),
        scratch_shapes=[pltpu.VMEM((tp, HEAD_DIM), BF16), pltpu.VMEM((tp, HEAD_DIM), BF16)],
        compiler_params=_cparams(("parallel", "parallel", "arbitrary"), VMEM_LIMIT),
        name=name,
    )(proj, proj, proj, band, lamp, sub)


PPS = 4


def _attn_sample_body(pt_ref, q_ref, kn_ref, vn_ref, *rest, lam_init, n_pages, n_new):
    kp_refs, vp_refs = rest[:PPS], rest[PPS:2 * PPS]
    bias_ref, lamp_ref, sub_ref, o_ref, qs_ref, m_ref, l_ref, acc_ref = rest[2 * PPS:]
    p = pl.program_id(1)
    n_steps = n_pages // PPS

    @pl.when(p == 0)
    def _():
        for h in range(H_A):
            q = q_ref[0:n_new, h * HEAD_DIM:(h + 1) * HEAD_DIM] * (HALF ** -0.5)
            qs = _split_maps(q)
            pad = jnp.zeros((SROWS - 2 * n_new, HEAD_DIM), F32)
            qs_ref[h] = jnp.concatenate([qs, pad], axis=0).astype(BF16)
        m_ref[...] = jnp.full(m_ref.shape, M_INIT, F32)
        l_ref[...] = jnp.zeros(l_ref.shape, F32)
        acc_ref[...] = jnp.zeros(acc_ref.shape, F32)

    def update(get_k, get_v, bias, n_rep):
        s = jnp.stack([lax.dot_general(qs_ref[h], get_k(h).astype(BF16), (((1,), (1,)), ((), ())),
                                       preferred_element_type=F32) for h in range(H_A)])
        s = s + bias
        m_prev = m_ref[...]
        m_new = jnp.maximum(m_prev, jnp.max(s, axis=-1, keepdims=True))
        alpha = jnp.exp(m_prev - m_new)
        pm = jnp.exp(s - jnp.concatenate([m_new] * n_rep, axis=-1))
        l_ref[...] = alpha * l_ref[...] + jnp.sum(pm, axis=-1, keepdims=True)
        m_ref[...] = m_new
        pb = pm.astype(BF16)
        for h in range(H_A):
            acc_ref[h] = alpha[h] * acc_ref[h] + jnp.dot(pb[h], get_v(h).astype(BF16),
                                                         preferred_element_type=F32)

    @pl.when(p < n_steps)
    def _():
        def pages(refs, h):
            return jnp.concatenate([r[h] for r in refs], axis=0)

        last_idx = jnp.where(p == n_steps - 1, 1, 0)
        bias = jnp.concatenate([bias_ref[0]] * (PPS - 1) + [bias_ref[last_idx]], axis=-1)
        update(functools.partial(pages, kp_refs), functools.partial(pages, vp_refs), bias, PPS)

    @pl.when(p == n_steps)
    def _():
        zpad = jnp.zeros((PAGE_SIZE - SROWS, HEAD_DIM), F32)

        def new_rows(ref, h):
            return jnp.concatenate([ref[:, h * HEAD_DIM:(h + 1) * HEAD_DIM], zpad], axis=0)

        update(functools.partial(new_rows, kn_ref), functools.partial(new_rows, vn_ref),
               bias_ref[2], 1)
        lam = _lambda(lamp_ref, lam_init)
        for h in range(H_A):
            o = acc_ref[h] / l_ref[h]
            a = o[0:n_new] - lam * o[n_new:2 * n_new]
            y = _subln(a, sub_ref, lam_init)
            pad = jnp.zeros((SROWS - n_new, HEAD_DIM), F32)
            o_ref[:, h * HEAD_DIM:(h + 1) * HEAD_DIM] = (
                jnp.concatenate([y, pad], axis=0).astype(o_ref.dtype))


def _attn_sample(page_table, proj, cache_k, cache_v, bias, lamp, sub, layer, row_blk0, n_new,
                 lam_init, name):
    db, n_pages = page_table.shape
    assert n_pages % PPS == 0
    body = functools.partial(_attn_sample_body, lam_init=lam_init, n_pages=n_pages, n_new=n_new)

    def page_spec(t):
        return pl.BlockSpec(
            (None, None, H_A, PAGE_SIZE, HEAD_DIM),
            lambda b, p, pt: (layer, pt[b, jnp.minimum(p * PPS + t, n_pages - 1)], 0, 0, 0))

    page_specs = [page_spec(t) for t in range(PPS)]
    grid_spec = pltpu.PrefetchScalarGridSpec(
        num_scalar_prefetch=1,
        grid=(db, n_pages // PPS + 1),
        in_specs=[
            pl.BlockSpec((SROWS, W_A), lambda b, p, pt: (row_blk0 + b, 0)),
            pl.BlockSpec((SROWS, W_A), lambda b, p, pt: (row_blk0 + b, 1)),
            pl.BlockSpec((SROWS, W_A), lambda b, p, pt: (row_blk0 + b, 2)),
            *page_specs, *page_specs,
            pl.BlockSpec((3, H_A, SROWS, PAGE_SIZE), lambda b, p, pt: (0, 0, 0, 0)),
            pl.BlockSpec((4, HALF), lambda b, p, pt: (0, 0)),
            pl.BlockSpec((1, HEAD_DIM), lambda b, p, pt: (0, 0)),
        ],
        out_specs=pl.BlockSpec((SROWS, W_A), lambda b, p, pt: (b, 0)),
        scratch_shapes=[pltpu.VMEM((H_A, SROWS, HEAD_DIM), BF16),
                        pltpu.VMEM((H_A, SROWS, HEAD_DIM), F32),
                        pltpu.VMEM((H_A, SROWS, HEAD_DIM), F32),
                        pltpu.VMEM((H_A, SROWS, HEAD_DIM), F32)],
    )
    return pl.pallas_call(
        body,
        grid_spec=grid_spec,
        out_shape=jax.ShapeDtypeStruct((db * SROWS, W_A), BF16),
        compiler_params=_cparams(("parallel", "arbitrary"), VMEM_LIMIT),
        name=name,
    )(page_table, proj, proj, proj, *([cache_k] * PPS), *([cache_v] * PPS), bias, lamp, sub)


RET_HG = 4


def _retention_body(q_ref, k_ref, v_ref, g_ref, c2_ref, s2_ref, lg_ref, gn_ref, s0_ref,
                    y_ref, sout_ref, st_ref, *, rows, n_chunks, last_valid):
    c = pl.program_id(2)

    @pl.when(c == 0)
    def _():
        st_ref[...] = s0_ref[...]

    lc = jnp.where(c == n_chunks - 1, float(last_valid), float(BLK))
    ii = lax.broadcasted_iota(jnp.int32, (BLK, BLK), 0).astype(F32)
    jj = lax.broadcasted_iota(jnp.int32, (BLK, BLK), 1).astype(F32)
    rel = ii - jj

    def rows128(x):
        if rows == BLK:
            return x
        return jnp.concatenate([x, jnp.zeros((BLK - rows, x.shape[1]), x.dtype)], axis=0)

    c2 = rows128(c2_ref[...])
    s2 = rows128(s2_ref[...])

    def rot(x):
        return x * c2 + pltpu.roll(x, HALF, 1) * s2

    for j in range(RET_HG):
        sl = slice(j * HEAD_DIM, (j + 1) * HEAD_DIM)
        lgv = lg_ref[:, sl]
        q = rows128(q_ref[:, sl])
        k = rows128(k_ref[:, sl])
        v = rows128(v_ref[:, sl]).astype(BF16)
        g = g_ref[:, sl]
        qr = rot(q).astype(BF16)
        kr = rot(k) * (HEAD_DIM ** -0.5)
        decay = jnp.where(rel >= 0, jnp.exp(lgv * jnp.maximum(rel, 0.0)), 0.0)
        sc = lax.dot_general(qr, kr.astype(BF16), (((1,), (1,)), ((), ())),
                             preferred_element_type=F32) * decay
        st = st_ref[j]
        o = jnp.dot(sc.astype(BF16), v, preferred_element_type=F32)
        o = o + jnp.dot(qr, st.astype(BF16), preferred_element_type=F32) * jnp.exp((ii + 1.0) * lgv)
        kdec = kr * jnp.where(ii < lc, jnp.exp((lc - 1.0 - ii) * lgv), 0.0)
        st_ref[j] = jnp.exp(lc * lgv) * st + jnp.dot(kdec.T.astype(BF16), v,
                                                     preferred_element_type=F32)
        y = o[:rows]
        y = y * lax.rsqrt(jnp.mean(y * y, axis=-1, keepdims=True) + EPS) * gn_ref[...]
        y_ref[:, sl] = (y * (g * _sigmoid(g))).astype(y_ref.dtype)

    @pl.when(c == n_chunks - 1)
    def _():
        sout_ref[...] = st_ref[...]


def _retention(proj, c2, s2, lgrow, gn, s0, nb, n_chunks, rows, row_blk0, last_valid, name):
    body = functools.partial(_retention_body, rows=rows, n_chunks=n_chunks, last_valid=last_valid)
    wb = RET_HG * HEAD_DIM
    cb = wb // 128

    def in_spec(col0):
        return pl.BlockSpec((rows, wb),
                            lambda b, hg, c: (row_blk0 + b * n_chunks + c, col0 // cb + hg))

    st_spec = pl.BlockSpec((None, RET_HG, HEAD_DIM, HEAD_DIM), lambda b, hg, c: (b, hg, 0, 0))
    return pl.pallas_call(
        body,
        grid=(nb, H_B // RET_HG, n_chunks),
        in_specs=[in_spec(COL_QB), in_spec(COL_KB), in_spec(COL_VB), in_spec(COL_GB),
                  pl.BlockSpec((rows, HEAD_DIM), lambda b, hg, c: (c, 0)),
                  pl.BlockSpec((rows, HEAD_DIM), lambda b, hg, c: (c, 0)),
                  pl.BlockSpec((1, wb), lambda b, hg, c: (0, hg)),
                  pl.BlockSpec((1, HEAD_DIM), lambda b, hg, c: (0, 0)),
                  st_spec],
        out_specs=[pl.BlockSpec((rows, wb), lambda b, hg, c: (b * n_chunks + c, hg)), st_spec],
        out_shape=[jax.ShapeDtypeStruct((nb * n_chunks * rows, W_B), BF16),
                   jax.ShapeDtypeStruct((nb, H_B, HEAD_DIM, HEAD_DIM), F32)],
        scratch_shapes=[pltpu.VMEM((RET_HG, HEAD_DIM, HEAD_DIM), F32)],
        compiler_params=_cparams(("parallel", "parallel", "arbitrary"), VMEM_LIMIT),
        name=name,
    )(proj, proj, proj, proj, c2, s2, lgrow, gn, s0)


def _conv_body(ca0_ref, ca1_ref, cb0_ref, cb1_ref, ctx_ref, w_ref, b_ref, lg_ref, lb_ref,
               y_ref, tail_ref, uc_ref, *, rows, n_chunks, last_valid):
    c = pl.program_id(1)

    @pl.when(c == 0)
    def _():
        uc_ref[0:CTX_ROWS, :] = ctx_ref[...]

    ca = jnp.concatenate([ca0_ref[...], ca1_ref[...]], axis=1)
    cb = jnp.concatenate([cb0_ref[...], cb1_ref[...]], axis=1)
    uc_ref[CTX_ROWS:CTX_ROWS + rows, :] = ca * _sigmoid(cb)
    acc = jnp.broadcast_to(b_ref[...], (rows, W_C))
    for w in range(CONV_W):
        acc = acc + uc_ref[pl.ds(CTX_ROWS - (CONV_W - 1) + w, rows), :] * w_ref[w:w + 1, :]
    mu = jnp.mean(acc, axis=-1, keepdims=True)
    xc = acc - mu
    var = jnp.mean(xc * xc, axis=-1, keepdims=True)
    y = xc * lax.rsqrt(var + EPS) * lg_ref[...] + lb_ref[...]
    y_ref[...] = (y * _sigmoid(y)).astype(y_ref.dtype)

    @pl.when(c == n_chunks - 1)
    def _():
        tail_ref[...] = uc_ref[pl.ds(CTX_ROWS + last_valid - (CONV_W - 1), CONV_W - 1), :]

    nxt = uc_ref[rows:rows + CTX_ROWS, :]
    uc_ref[0:CTX_ROWS, :] = nxt


def _conv(proj, ctx0, conv_w, conv_b, ln_g, ln_b, layer, nb, n_chunks, rows, row_blk0,
          last_valid, name):
    body = functools.partial(_conv_body, rows=rows, n_chunks=n_chunks, last_valid=last_valid)
    half = W_C // 2
    hb = half // 128

    def in_spec(col):
        return pl.BlockSpec((rows, half), lambda b, c: (row_blk0 + b * n_chunks + c, col))

    def par_spec():
        return pl.BlockSpec((None, 1, W_C), lambda b, c: (layer, 0, 0))

    return pl.pallas_call(
        body,
        grid=(nb, n_chunks),
        in_specs=[in_spec(COL_CA // hb), in_spec(COL_CA // hb + 1),
                  in_spec(COL_CB // hb), in_spec(COL_CB // hb + 1),
                  pl.BlockSpec((None, CTX_ROWS, W_C), lambda b, c: (b, 0, 0)),
                  pl.BlockSpec((None, CONV_W, W_C), lambda b, c: (layer, 0, 0)),
                  par_spec(), par_spec(), par_spec()],
        out_specs=[pl.BlockSpec((rows, W_C), lambda b, c: (b * n_chunks + c, 0)),
                   pl.BlockSpec((None, CONV_W - 1, W_C), lambda b, c: (b, 0, 0))],
        out_shape=[jax.ShapeDtypeStruct((nb * n_chunks * rows, W_C), BF16),
                   jax.ShapeDtypeStruct((nb, CONV_W - 1, W_C), F32)],
        scratch_shapes=[pltpu.VMEM((CTX_ROWS + rows, W_C), F32)],
        compiler_params=_cparams(("parallel", "arbitrary"), VMEM_LIMIT),
        name=name,
    )(proj, proj, proj, proj, ctx0, conv_w, conv_b, ln_g, ln_b)


def _t5_bias(rel_bias, dist):
    n = jnp.maximum(dist, 0)
    nf = jnp.maximum(n, 1).astype(F32)
    large = REL_EXACT + (jnp.log(nf / REL_EXACT) / math.log(REL_MAX_DIST / REL_EXACT)
                         * (N_BUCKETS - REL_EXACT)).astype(jnp.int32)
    bucket = jnp.where(n < REL_EXACT, n, jnp.minimum(large, N_BUCKETS - 1))
    onehot = (bucket[..., None] == jnp.arange(N_BUCKETS, dtype=jnp.int32)).astype(F32)
    bias = jnp.einsum("...k,kh->h...", onehot, rel_bias.astype(F32),
                      precision=lax.Precision.HIGHEST)
    return jnp.where(dist[None] >= 0, bias, NEG_INF)


def _rotary_tables(pos):
    inv = ROPE_BASE ** (-jnp.arange(HALF, dtype=F32) / HALF)
    ang = pos.astype(F32)[:, None] * inv[None, :]
    cos, sin = jnp.cos(ang), jnp.sin(ang)
    return jnp.concatenate([cos, cos], axis=1), jnp.concatenate([-sin, sin], axis=1)


def kernel(x_prompt, x_sample, cache_k, cache_v, state_ret, state_conv, page_table, meta, rel_bias,
           norm_mix, w_in, lam_q1, lam_k1, lam_q2, lam_k2, subln_a, gn_b, conv_w, conv_b, conv_ln_g,
           conv_ln_b, w_out, norm_ffn, w_gate, w_up, w_down, norm_final):
    nb, seq = x_prompt.shape[:2]
    db, n_new = x_sample.shape[:2]
    depth = w_in.shape[0]
    n_pages = page_table.shape[1]
    past = n_pages * PAGE_SIZE
    t_valid = seq + N_META
    nblk = -(-t_valid // BLK)
    tp = nblk * BLK
    last_valid = t_valid - (nblk - 1) * BLK
    mp = nb * tp
    m_all = mp + db * SROWS
    s_blk0 = mp // SROWS
    d_ff = w_gate.shape[2]

    tm = m_all // 6
    tr = m_all // 24

    hp = jnp.concatenate([jnp.broadcast_to(meta[None].astype(F32), (nb, N_META, D_MODEL)),
                          x_prompt, jnp.zeros((nb, tp - t_valid, D_MODEL), F32)], axis=1)
    hs = jnp.pad(x_sample, ((0, 0), (0, SROWS - n_new), (0, 0)))
    h = jnp.concatenate([hp.reshape(mp, D_MODEL), hs.reshape(db * SROWS, D_MODEL)], axis=0)

    assert BLK >= REL_MAX_DIST and PAGE_SIZE >= REL_MAX_DIST and nblk >= 2
    far = _t5_bias(rel_bias, jnp.full((1, 1), REL_MAX_DIST, jnp.int32))
    d0 = (nblk - 1) * BLK
    ii = jnp.arange(BLK, dtype=jnp.int32)[:, None]
    cn = jnp.arange(d0 - BLK, d0 + BLK, dtype=jnp.int32)[None, :]
    near = _t5_bias(rel_bias, d0 + ii - cn)
    band = jnp.concatenate([jnp.broadcast_to(far, (H_A, BLK, d0 - BLK)), near,
                            jnp.full((H_A, BLK, d0), NEG_INF, F32)], axis=2)
    r_new = jnp.arange(SROWS, dtype=jnp.int32) % n_new
    jk = jnp.arange(PAGE_SIZE, dtype=jnp.int32)[None, :]
    dist_s = jnp.stack([jnp.full((SROWS, PAGE_SIZE), REL_MAX_DIST, jnp.int32),
                        PAGE_SIZE + r_new[:, None] - jk,
                        jnp.where(jk < n_new, r_new[:, None] - jk, -1)])
    bias_s = jnp.transpose(_t5_bias(rel_bias, dist_s), (1, 0, 2, 3))
    cache_kh = jnp.transpose(cache_k, (0, 1, 3, 2, 4))
    cache_vh = jnp.transpose(cache_v, (0, 1, 3, 2, 4))
    c2_p, s2_p = _rotary_tables(jnp.arange(tp, dtype=jnp.int32))
    c2_s, s2_s = _rotary_tables(past + jnp.arange(SROWS, dtype=jnp.int32))
    log_gamma = jnp.log1p(-jnp.exp2(-5.0 - jnp.arange(H_B, dtype=F32)))
    lgrow = jnp.repeat(log_gamma, HEAD_DIM)[None, :]

    zero_state = jnp.zeros((nb, H_B, HEAD_DIM, HEAD_DIM), F32)
    zero_ctx = jnp.zeros((nb, CTX_ROWS, W_C), F32)
    conv_b3 = conv_b[:, None, :]
    ln_g3 = conv_ln_g[:, None, :]
    ln_b3 = conv_ln_b[:, None, :]

    k_p, v_p, ret_p, conv_p, k_s, v_s, ret_s, conv_s = [], [], [], [], [], [], [], []
    for l in range(depth):
        lam_init = 0.8 - 0.6 * math.exp(-0.3 * l)
        lamp = jnp.stack([lam_q1[l], lam_k1[l], lam_q2[l], lam_k2[l]]).astype(F32)
        sub = subln_a[l][None, :]
        gn = gn_b[l][None, :]

        n1 = _rmsnorm(h, norm_mix[l], BF16, tr, f"norm_mix{l}")
        proj = _matmul(n1, w_in, l, tm, 256, f"in_proj{l}")

        ya_p = _attn_prompt(proj, band, lamp, sub, nb, nblk, lam_init, f"attn_prompt{l}")
        ya_s = _attn_sample(page_table, proj, cache_kh, cache_vh, bias_s, lamp, sub, l, s_blk0,
                            n_new, lam_init, f"attn_sample{l}")
        yb_p, st_p = _retention(proj, c2_p, s2_p, lgrow, gn, zero_state, nb, nblk, BLK, 0,
                                last_valid, f"ret_prompt{l}")
        yb_s, st_s = _retention(proj, c2_s, s2_s, lgrow, gn, state_ret[l], db, 1, SROWS,
                                s_blk0, n_new, f"ret_sample{l}")
        yc_p, tail_p = _conv(proj, zero_ctx, conv_w, conv_b3, ln_g3, ln_b3, l, nb, nblk, BLK, 0,
                             last_valid, f"conv_prompt{l}")
        ctx_s = jnp.pad(state_conv[l], ((0, 0), (CTX_ROWS - (CONV_W - 1), 0), (0, 0)))
        yc_s, tail_s = _conv(proj, ctx_s, conv_w, conv_b3, ln_g3, ln_b3, l, db, 1, SROWS, s_blk0,
                             n_new, f"conv_sample{l}")

        mix = jnp.concatenate([jnp.concatenate([ya_p, ya_s], axis=0),
                               jnp.concatenate([yb_p, yb_s], axis=0),
                               jnp.concatenate([yc_p, yc_s], axis=0)], axis=1)
        h = _matmul(mix, w_out, l, tm, 256, f"out_proj{l}", res=h)
        n2 = _rmsnorm(h, norm_ffn[l], BF16, tr, f"norm_ffn{l}")
        hid = _ffn_up(n2, w_gate, w_up, l, tm, 256, f"ffn_up{l}")
        for kb in range(2):
            h = _matmul(hid, w_down, l, m_all // 8, 256, f"ffn_down{l}_{kb}", res=h,
                        tk=d_ff // 2, kblk=kb)

        kv = proj[:, W_A:3 * W_A]
        kvp = kv[:mp].reshape(nb, tp, 2, H_A, HEAD_DIM)[:, :t_valid]
        kvs = kv[mp:].reshape(db, SROWS, 2, H_A, HEAD_DIM)[:, :n_new]
        k_p.append(kvp[:, :, 0])
        v_p.append(kvp[:, :, 1])
        k_s.append(kvs[:, :, 0])
        v_s.append(kvs[:, :, 1])
        ret_p.append(st_p)
        ret_s.append(st_s)
        conv_p.append(tail_p)
        conv_s.append(tail_s)

    hf = _rmsnorm(h, norm_final, F32, tr, "norm_final")
    y_prompt = hf[:mp].reshape(nb, tp, D_MODEL)[:, N_META:t_valid]
    y_sample = hf[mp:].reshape(db, SROWS, D_MODEL)[:, :n_new]
    return (y_prompt, y_sample, jnp.stack(k_p), jnp.stack(v_p), jnp.stack(ret_p), jnp.stack(conv_p),
            jnp.stack(k_s), jnp.stack(v_s), jnp.stack(ret_s), jnp.stack(conv_s))
```

```python
import functools
import math

import jax
import jax.numpy as jnp
from jax import lax
from jax.experimental import pallas as pl
from jax.experimental.pallas import tpu as pltpu

F32 = jnp.float32
BF16 = jnp.bfloat16

D_MODEL = 4096
N_META = 16
HEAD_DIM = 128
HALF = HEAD_DIM // 2
H_A = 12
H_B = 12
W_A = H_A * HEAD_DIM
W_B = H_B * HEAD_DIM
W_C = D_MODEL - W_A - W_B
N_IN = 3 * W_A + 4 * W_B + 2 * W_C
CONV_W = 31
N_BUCKETS = 32
REL_EXACT = 16
REL_MAX_DIST = 128
ROPE_BASE = 10000.0
EPS = 1e-6
NEG_INF = -1e30
PAGE_SIZE = 128

BLK = 128
SROWS = 16
CTX_ROWS = 32
M_INIT = -3.0e38
VMEM_LIMIT = 56 * 1024 * 1024

COL_QA, COL_KA, COL_VA = 0, W_A // 128, 2 * W_A // 128
COL_QB = 3 * W_A // 128
COL_KB = COL_QB + W_B // 128
COL_VB = COL_KB + W_B // 128
COL_GB = COL_VB + W_B // 128
COL_CA = COL_GB + W_B // 128
COL_CB = COL_CA + W_C // 128


def _cparams(sem, vmem=None):
    return pltpu.CompilerParams(dimension_semantics=sem, vmem_limit_bytes=vmem)


def _sigmoid(x):
    return 1.0 / (1.0 + jnp.exp(-x))


def _rmsnorm_body(x_ref, g_ref, o_ref):
    x = x_ref[...]
    ms = jnp.mean(x * x, axis=-1, keepdims=True)
    o_ref[...] = (x * lax.rsqrt(ms + EPS) * g_ref[...]).astype(o_ref.dtype)


def _rmsnorm(x, g, out_dtype, tr, name):
    m, d = x.shape
    return pl.pallas_call(
        _rmsnorm_body,
        grid=(m // tr,),
        in_specs=[pl.BlockSpec((tr, d), lambda i: (i, 0)),
                  pl.BlockSpec((1, d), lambda i: (0, 0))],
        out_specs=pl.BlockSpec((tr, d), lambda i: (i, 0)),
        out_shape=jax.ShapeDtypeStruct((m, d), out_dtype),
        compiler_params=_cparams(("parallel",)),
        name=name,
    )(x, g.reshape(1, d))


def _mm_body(a_ref, w_ref, o_ref):
    o_ref[...] = jnp.dot(a_ref[...], w_ref[...].astype(BF16), preferred_element_type=F32)


def _mm_res_body(a_ref, w_ref, r_ref, o_ref):
    o_ref[...] = r_ref[...] + jnp.dot(a_ref[...], w_ref[...].astype(BF16),
                                      preferred_element_type=F32)


def _matmul(a, w, layer, tm, tn, name, res=None, tk=None, kblk=0):
    m = a.shape[0]
    k = a.shape[1] if tk is None else tk
    n = w.shape[2]
    in_specs = [pl.BlockSpec((tm, k), lambda i, j: (i, kblk)),
                pl.BlockSpec((None, k, tn), lambda i, j: (layer, kblk, j))]
    args = [a, w]
    body = _mm_body
    if res is not None:
        in_specs.append(pl.BlockSpec((tm, tn), lambda i, j: (i, j)))
        args.append(res)
        body = _mm_res_body
    return pl.pallas_call(
        body,
        grid=(m // tm, n // tn),
        in_specs=in_specs,
        out_specs=pl.BlockSpec((tm, tn), lambda i, j: (i, j)),
        out_shape=jax.ShapeDtypeStruct((m, n), F32),
        compiler_params=_cparams(("parallel", "parallel"), VMEM_LIMIT),
        name=name,
    )(*args)


def _ffn_up_body(a_ref, wg_ref, wu_ref, o_ref):
    a = a_ref[...]
    g = jnp.dot(a, wg_ref[...].astype(BF16), preferred_element_type=F32)
    u = jnp.dot(a, wu_ref[...].astype(BF16), preferred_element_type=F32)
    o_ref[...] = (g * _sigmoid(g) * u).astype(o_ref.dtype)


def _ffn_up(a, wg, wu, layer, tm, tn, name):
    m, k = a.shape
    n = wg.shape[2]
    wspec = pl.BlockSpec((None, k, tn), lambda i, j: (layer, 0, j))
    return pl.pallas_call(
        _ffn_up_body,
        grid=(m // tm, n // tn),
        in_specs=[pl.BlockSpec((tm, k), lambda i, j: (i, 0)), wspec, wspec],
        out_specs=pl.BlockSpec((tm, tn), lambda i, j: (i, j)),
        out_shape=jax.ShapeDtypeStruct((m, n), BF16),
        compiler_params=_cparams(("parallel", "parallel"), VMEM_LIMIT),
        name=name,
    )(a, wg, wu)


def _out_proj_body(ya_ref, yb_ref, yc_ref, w_ref, r_ref, o_ref):
    acc = jnp.dot(ya_ref[...], w_ref[0:W_A, :].astype(BF16), preferred_element_type=F32)
    acc += jnp.dot(yb_ref[...], w_ref[W_A:W_A + W_B, :].astype(BF16), preferred_element_type=F32)
    acc += jnp.dot(yc_ref[...], w_ref[W_A + W_B:, :].astype(BF16), preferred_element_type=F32)
    o_ref[...] = r_ref[...] + acc


def _out_proj(ya, yb, yc, w, layer, h, row_tile0, tm, tn, name):
    rows = ya.shape[0]
    n = w.shape[2]
    hspec = pl.BlockSpec((tm, tn), lambda i, j: (row_tile0 + i, j))
    return pl.pallas_call(
        _out_proj_body,
        grid=(rows // tm, n // tn),
        in_specs=[pl.BlockSpec((tm, W_A), lambda i, j: (i, 0)),
                  pl.BlockSpec((tm, W_B), lambda i, j: (i, 0)),
                  pl.BlockSpec((tm, W_C), lambda i, j: (i, 0)),
                  pl.BlockSpec((None, W_A + W_B + W_C, tn), lambda i, j: (layer, 0, j)),
                  hspec],
        out_specs=hspec,
        out_shape=jax.ShapeDtypeStruct(h.shape, F32),
        input_output_aliases={4: 0},
        compiler_params=_cparams(("parallel", "parallel"), VMEM_LIMIT),
        name=name,
    )(ya, yb, yc, w, h)


def _final_norm_body(a_ref, b_ref, g_ref, o_ref):
    x = jnp.concatenate([a_ref[N_META:, :], b_ref[0:N_META, :]], axis=0)
    ms = jnp.mean(x * x, axis=-1, keepdims=True)
    o_ref[...] = x * lax.rsqrt(ms + EPS) * g_ref[...]


def _final_norm_prompt(h, g, nb, seq, nblk, name):
    assert seq % BLK == 0 and N_META % 8 == 0 and N_META <= BLK and seq // BLK < nblk
    d = h.shape[1]
    return pl.pallas_call(
        _final_norm_body,
        grid=(nb, seq // BLK),
        in_specs=[pl.BlockSpec((BLK, d), lambda b, j: (b * nblk + j, 0)),
                  pl.BlockSpec((BLK, d), lambda b, j: (b * nblk + j + 1, 0)),
                  pl.BlockSpec((1, d), lambda b, j: (0, 0))],
        out_specs=pl.BlockSpec((None, BLK, d), lambda b, j: (b, j, 0)),
        out_shape=jax.ShapeDtypeStruct((nb, seq, d), F32),
        compiler_params=_cparams(("parallel", "parallel")),
        name=name,
    )(h, h, g.reshape(1, d))


def _kv_out_body(*refs, depth, t_valid):
    ko_ref, vo_ref = refs[2 * depth:]
    for l in range(depth):
        ko_ref[l] = refs[2 * l][0:t_valid, :]
        vo_ref[l] = refs[2 * l + 1][0:t_valid, :]


def _kv_out(projs, nb, nblk, t_valid, name):
    depth = len(projs)
    tp = nblk * BLK
    in_specs, args = [], []
    for proj in projs:
        in_specs += [pl.BlockSpec((tp, HEAD_DIM), lambda b, h: (b, COL_KA + h)),
                     pl.BlockSpec((tp, HEAD_DIM), lambda b, h: (b, COL_VA + h))]
        args += [proj, proj]
    ospec = pl.BlockSpec((depth, None, None, t_valid, HEAD_DIM), lambda b, h: (0, b, h, 0, 0))
    oshape = jax.ShapeDtypeStruct((depth, nb, H_A, t_valid, HEAD_DIM), F32)
    return pl.pallas_call(
        functools.partial(_kv_out_body, depth=depth, t_valid=t_valid),
        grid=(nb, H_A),
        in_specs=in_specs,
        out_specs=[ospec, ospec],
        out_shape=[oshape, oshape],
        compiler_params=_cparams(("parallel", "parallel")),
        name=name,
    )(*args)


def _lambda(lamp_ref, lam_init):
    lp = lamp_ref[...]
    s1 = jnp.sum(lp[0:1] * lp[1:2], axis=-1, keepdims=True)
    s2 = jnp.sum(lp[2:3] * lp[3:4], axis=-1, keepdims=True)
    return jnp.exp(s1) - jnp.exp(s2) + lam_init


def _split_maps(q):
    lane = lax.broadcasted_iota(jnp.int32, q.shape, 1)
    lo = jnp.where(lane < HALF, q, 0.0)
    hi = jnp.where(lane >= HALF, q, 0.0)
    return jnp.concatenate([lo, hi], axis=0)


def _subln(a, sub_ref, lam_init):
    y = a * lax.rsqrt(jnp.mean(a * a, axis=-1, keepdims=True) + EPS) * sub_ref[...]
    return y * (1.0 - lam_init)


ATT_HG = 2


def _attn_prompt_body(q_ref, k_ref, v_ref, band_ref, lamp_ref, sub_ref, o_ref, kb_ref, vt_ref,
                      s_ref, p_ref, *, lam_init, nblk):
    qi = pl.program_id(2)

    @pl.when(qi == 0)
    def _():
        kb_ref[...] = k_ref[...].astype(BF16)
        for g in range(ATT_HG):
            for j in range(nblk):
                vt = v_ref[j * BLK:(j + 1) * BLK, g * HEAD_DIM:(g + 1) * HEAD_DIM].T
                vt_ref[g, :, j * BLK:(j + 1) * BLK] = vt.astype(BF16)

    def attend(nt):
        row0 = (nblk - 1 - qi) * BLK
        lanes = [slice(g * HEAD_DIM, (g + 1) * HEAD_DIM) for g in range(ATT_HG)]
        lam = _lambda(lamp_ref, lam_init)
        mrun = [None] * ATT_HG
        lrun = [None] * ATT_HG
        m = [None] * ATT_HG
        qst = [None] * ATT_HG

        def score_tile(g, j):
            if j == 0:
                qst[g] = _split_maps(q_ref[:, lanes[g]] * (HALF ** -0.5)).T.astype(BF16)
            bt = band_ref[g, pl.ds(pl.multiple_of(row0 + j * BLK, BLK), BLK), :]
            s = jnp.dot(kb_ref[j * BLK:(j + 1) * BLK, lanes[g]], qst[g],
                        preferred_element_type=F32)
            s = s + jnp.concatenate([bt, bt], axis=1)
            s_ref[g, j] = s
            mrun[g] = s if mrun[g] is None else jnp.maximum(mrun[g], s)
            if j == nt - 1:
                m[g] = jnp.max(mrun[g], axis=0, keepdims=True)

        def exp_tile(g, j):
            p = jnp.exp(s_ref[g, j] - m[g])
            lrun[g] = p if lrun[g] is None else lrun[g] + p
            p_ref[g, j * BLK:(j + 1) * BLK, :] = p.astype(BF16)
            if j == nt - 1:
                l = jnp.sum(lrun[g], axis=0, keepdims=True)
                ot = jnp.dot(vt_ref[g, :, 0:nt * BLK], p_ref[g, 0:nt * BLK, :],
                             preferred_element_type=F32) / l
                at = ot[:, :BLK] - lam * ot[:, BLK:]
                at = at * lax.rsqrt(jnp.mean(at * at, axis=0, keepdims=True) + EPS)
                o_ref[:, lanes[g]] = (at.T * sub_ref[...] * (1.0 - lam_init)).astype(o_ref.dtype)

        for step in range(ATT_HG + 1):
            for j in range(nt):
                if step < ATT_HG:
                    score_tile(step, j)
                if step >= 1:
                    exp_tile(step - 1, j)

    lo = 0
    for hi in sorted(set(range(2, nblk, 3)) | {nblk}):
        pl.when(jnp.logical_and(qi >= lo, qi < hi))(functools.partial(attend, hi))
        lo = hi


def _attn_prompt(proj, band, lamp, sub, nb, nblk, lam_init, name):
    tp = nblk * BLK
    wg = ATT_HG * HEAD_DIM
    body = functools.partial(_attn_prompt_body, lam_init=lam_init, nblk=nblk)
    return pl.pallas_call(
        body,
        grid=(nb, H_A // ATT_HG, nblk),
        in_specs=[
            pl.BlockSpec((BLK, wg), lambda b, h, qi: (b * nblk + qi, COL_QA // ATT_HG + h)),
            pl.BlockSpec((tp, wg), lambda b, h, qi: (b, COL_KA // ATT_HG + h)),
            pl.BlockSpec((tp, wg), lambda b, h, qi: (b, COL_VA // ATT_HG + h)),
            pl.BlockSpec((ATT_HG, band.shape[1], BLK), lambda b, h, qi: (h, 0, 0)),
            pl.BlockSpec((4, HALF), lambda b, h, qi: (0, 0)),
            pl.BlockSpec((1, HEAD_DIM), lambda b, h, qi: (0, 0)),
        ],
        out_specs=pl.BlockSpec((BLK, wg), lambda b, h, qi: (b * nblk + qi, h)),
        out_shape=jax.ShapeDtypeStruct((nb * tp, W_A), BF16),
        scratch_shapes=[pltpu.VMEM((tp, wg), BF16), pltpu.VMEM((ATT_HG, HEAD_DIM, tp), BF16),
                        pltpu.VMEM((ATT_HG, nblk, BLK, 2 * BLK), F32),
                        pltpu.VMEM((ATT_HG, tp, 2 * BLK), BF16)],
        compiler_params=_cparams(("parallel", "parallel", "arbitrary"), VMEM_LIMIT),
        name=name,
    )(proj, proj, proj, band, lamp, sub)


PPS = 4


def _attn_sample_body(pt_ref, q_ref, kn_ref, vn_ref, *rest, lam_init, n_pages, n_new):
    kp_refs, vp_refs = rest[:PPS], rest[PPS:2 * PPS]
    bias_ref, lamp_ref, sub_ref, o_ref, qs_ref, m_ref, l_ref, acc_ref = rest[2 * PPS:]
    p = pl.program_id(1)
    n_steps = n_pages // PPS

    @pl.when(p == 0)
    def _():
        for h in range(H_A):
            q = q_ref[0:n_new, h * HEAD_DIM:(h + 1) * HEAD_DIM] * (HALF ** -0.5)
            qs = _split_maps(q)
            pad = jnp.zeros((SROWS - 2 * n_new, HEAD_DIM), F32)
            qs_ref[h] = jnp.concatenate([qs, pad], axis=0).astype(BF16)
        m_ref[...] = jnp.full(m_ref.shape, M_INIT, F32)
        l_ref[...] = jnp.zeros(l_ref.shape, F32)
        acc_ref[...] = jnp.zeros(acc_ref.shape, F32)

    def update(get_k, get_v, bias, n_rep):
        s = jnp.stack([lax.dot_general(qs_ref[h], get_k(h).astype(BF16), (((1,), (1,)), ((), ())),
                                       preferred_element_type=F32) for h in range(H_A)])
        s = s + bias
        m_prev = m_ref[...]
        m_new = jnp.maximum(m_prev, jnp.max(s, axis=-1, keepdims=True))
        alpha = jnp.exp(m_prev - m_new)
        pm = jnp.exp(s - jnp.concatenate([m_new] * n_rep, axis=-1))
        l_ref[...] = alpha * l_ref[...] + jnp.sum(pm, axis=-1, keepdims=True)
        m_ref[...] = m_new
        pb = pm.astype(BF16)
        for h in range(H_A):
            acc_ref[h] = alpha[h] * acc_ref[h] + jnp.dot(pb[h], get_v(h).astype(BF16),
                                                         preferred_element_type=F32)

    @pl.when(p < n_steps)
    def _():
        def pages(refs, h):
            return jnp.concatenate([r[h] for r in refs], axis=0)

        last_idx = jnp.where(p == n_steps - 1, 1, 0)
        bias = jnp.concatenate([bias_ref[0]] * (PPS - 1) + [bias_ref[last_idx]], axis=-1)
        update(functools.partial(pages, kp_refs), functools.partial(pages, vp_refs), bias, PPS)

    @pl.when(p == n_steps)
    def _():
        zpad = jnp.zeros((PAGE_SIZE - SROWS, HEAD_DIM), F32)

        def new_rows(ref, h):
            return jnp.concatenate([ref[:, h * HEAD_DIM:(h + 1) * HEAD_DIM], zpad], axis=0)

        update(functools.partial(new_rows, kn_ref), functools.partial(new_rows, vn_ref),
               bias_ref[2], 1)
        lam = _lambda(lamp_ref, lam_init)
        for h in range(H_A):
            o = acc_ref[h] / l_ref[h]
            a = o[0:n_new] - lam * o[n_new:2 * n_new]
            y = _subln(a, sub_ref, lam_init)
            pad = jnp.zeros((SROWS - n_new, HEAD_DIM), F32)
            o_ref[:, h * HEAD_DIM:(h + 1) * HEAD_DIM] = (
                jnp.concatenate([y, pad], axis=0).astype(o_ref.dtype))


def _attn_sample(page_table, proj, cache_k, cache_v, bias, lamp, sub, layer, row_blk0, n_new,
                 lam_init, name):
    db, n_pages = page_table.shape
    assert n_pages % PPS == 0
    body = functools.partial(_attn_sample_body, lam_init=lam_init, n_pages=n_pages, n_new=n_new)

    def page_spec(t):
        return pl.BlockSpec(
            (None, None, H_A, PAGE_SIZE, HEAD_DIM),
            lambda b, p, pt: (layer, pt[b, jnp.minimum(p * PPS + t, n_pages - 1)], 0, 0, 0))

    page_specs = [page_spec(t) for t in range(PPS)]
    grid_spec = pltpu.PrefetchScalarGridSpec(
        num_scalar_prefetch=1,
        grid=(db, n_pages // PPS + 1),
        in_specs=[
            pl.BlockSpec((SROWS, W_A), lambda b, p, pt: (row_blk0 + b, 0)),
            pl.BlockSpec((SROWS, W_A), lambda b, p, pt: (row_blk0 + b, 1)),
            pl.BlockSpec((SROWS, W_A), lambda b, p, pt: (row_blk0 + b, 2)),
            *page_specs, *page_specs,
            pl.BlockSpec((3, H_A, SROWS, PAGE_SIZE), lambda b, p, pt: (0, 0, 0, 0)),
            pl.BlockSpec((4, HALF), lambda b, p, pt: (0, 0)),
            pl.BlockSpec((1, HEAD_DIM), lambda b, p, pt: (0, 0)),
        ],
        out_specs=pl.BlockSpec((SROWS, W_A), lambda b, p, pt: (b, 0)),
        scratch_shapes=[pltpu.VMEM((H_A, SROWS, HEAD_DIM), BF16),
                        pltpu.VMEM((H_A, SROWS, HEAD_DIM), F32),
                        pltpu.VMEM((H_A, SROWS, HEAD_DIM), F32),
                        pltpu.VMEM((H_A, SROWS, HEAD_DIM), F32)],
    )
    return pl.pallas_call(
        body,
        grid_spec=grid_spec,
        out_shape=jax.ShapeDtypeStruct((db * SROWS, W_A), BF16),
        compiler_params=_cparams(("parallel", "arbitrary"), VMEM_LIMIT),
        name=name,
    )(page_table, proj, proj, proj, *([cache_k] * PPS), *([cache_v] * PPS), bias, lamp, sub)


RET_HG = 4


def _retention_body(q_ref, k_ref, v_ref, g_ref, c2_ref, s2_ref, lg_ref, gn_ref, s0_ref,
                    y_ref, sout_ref, st_ref, *, rows, n_chunks, last_valid):
    c = pl.program_id(2)

    @pl.when(c == 0)
    def _():
        st_ref[...] = s0_ref[...]

    lc = jnp.where(c == n_chunks - 1, float(last_valid), float(BLK))
    ii = lax.broadcasted_iota(jnp.int32, (BLK, BLK), 0).astype(F32)
    jj = lax.broadcasted_iota(jnp.int32, (BLK, BLK), 1).astype(F32)
    rel = ii - jj

    def rows128(x):
        if rows == BLK:
            return x
        return jnp.concatenate([x, jnp.zeros((BLK - rows, x.shape[1]), x.dtype)], axis=0)

    c2 = rows128(c2_ref[...])
    s2 = rows128(s2_ref[...])

    def rot(x):
        return x * c2 + pltpu.roll(x, HALF, 1) * s2

    heads = []
    for j in range(RET_HG):
        sl = slice(j * HEAD_DIM, (j + 1) * HEAD_DIM)
        lgv = lg_ref[:, sl]
        v = rows128(v_ref[:, sl]).astype(BF16)
        qr = rot(rows128(q_ref[:, sl])).astype(BF16)
        kr = rot(rows128(k_ref[:, sl])) * (HEAD_DIM ** -0.5)
        st = st_ref[j]
        sc = lax.dot_general(qr, kr.astype(BF16), (((1,), (1,)), ((), ())),
                             preferred_element_type=F32)
        cross = jnp.dot(qr, st.astype(BF16), preferred_element_type=F32)
        kdec = kr * jnp.where(ii < lc, jnp.exp((lc - 1.0 - ii) * lgv), 0.0)
        st_ref[j] = jnp.exp(lc * lgv) * st + jnp.dot(kdec.T.astype(BF16), v,
                                                     preferred_element_type=F32)
        heads.append((sl, lgv, v, sc, cross))
    for sl, lgv, v, sc, cross in heads:
        decay = jnp.where(rel >= 0, jnp.exp(lgv * jnp.maximum(rel, 0.0)), 0.0)
        o = jnp.dot((sc * decay).astype(BF16), v, preferred_element_type=F32)
        o = o + cross * jnp.exp((ii + 1.0) * lgv)
        y = o[:rows]
        y = y * lax.rsqrt(jnp.mean(y * y, axis=-1, keepdims=True) + EPS) * gn_ref[...]
        g = g_ref[:, sl]
        y_ref[:, sl] = (y * (g * _sigmoid(g))).astype(y_ref.dtype)

    @pl.when(c == n_chunks - 1)
    def _():
        sout_ref[...] = st_ref[...]


def _retention(proj, c2, s2, lgrow, gn, s0, nb, n_chunks, rows, row_blk0, last_valid, name):
    body = functools.partial(_retention_body, rows=rows, n_chunks=n_chunks, last_valid=last_valid)
    wb = RET_HG * HEAD_DIM
    cb = wb // 128

    def in_spec(col0):
        return pl.BlockSpec((rows, wb),
                            lambda b, hg, c: (row_blk0 + b * n_chunks + c, col0 // cb + hg))

    st_spec = pl.BlockSpec((None, RET_HG, HEAD_DIM, HEAD_DIM), lambda b, hg, c: (b, hg, 0, 0))
    return pl.pallas_call(
        body,
        grid=(nb, H_B // RET_HG, n_chunks),
        in_specs=[in_spec(COL_QB), in_spec(COL_KB), in_spec(COL_VB), in_spec(COL_GB),
                  pl.BlockSpec((rows, HEAD_DIM), lambda b, hg, c: (c, 0)),
                  pl.BlockSpec((rows, HEAD_DIM), lambda b, hg, c: (c, 0)),
                  pl.BlockSpec((1, wb), lambda b, hg, c: (0, hg)),
                  pl.BlockSpec((1, HEAD_DIM), lambda b, hg, c: (0, 0)),
                  st_spec],
        out_specs=[pl.BlockSpec((rows, wb), lambda b, hg, c: (b * n_chunks + c, hg)), st_spec],
        out_shape=[jax.ShapeDtypeStruct((nb * n_chunks * rows, W_B), BF16),
                   jax.ShapeDtypeStruct((nb, H_B, HEAD_DIM, HEAD_DIM), F32)],
        scratch_shapes=[pltpu.VMEM((RET_HG, HEAD_DIM, HEAD_DIM), F32)],
        compiler_params=_cparams(("parallel", "parallel", "arbitrary"), VMEM_LIMIT),
        name=name,
    )(proj, proj, proj, proj, c2, s2, lgrow, gn, s0)


def _conv_body(ca0_ref, ca1_ref, cb0_ref, cb1_ref, ctx_ref, w_ref, b_ref, lg_ref, lb_ref,
               y_ref, tail_ref, uc_ref, *, rows, n_chunks, last_valid):
    c = pl.program_id(1)

    @pl.when(c == 0)
    def _():
        uc_ref[0:CTX_ROWS, :] = ctx_ref[...]

    ca = jnp.concatenate([ca0_ref[...], ca1_ref[...]], axis=1)
    cb = jnp.concatenate([cb0_ref[...], cb1_ref[...]], axis=1)
    uc_ref[CTX_ROWS:CTX_ROWS + rows, :] = ca * _sigmoid(cb)
    acc = jnp.broadcast_to(b_ref[...], (rows, W_C))
    for w in range(CONV_W):
        acc = acc + uc_ref[pl.ds(CTX_ROWS - (CONV_W - 1) + w, rows), :] * w_ref[w:w + 1, :]
    mu = jnp.mean(acc, axis=-1, keepdims=True)
    xc = acc - mu
    var = jnp.mean(xc * xc, axis=-1, keepdims=True)
    y = xc * lax.rsqrt(var + EPS) * lg_ref[...] + lb_ref[...]
    y_ref[...] = (y * _sigmoid(y)).astype(y_ref.dtype)

    @pl.when(c == n_chunks - 1)
    def _():
        tail_ref[...] = uc_ref[pl.ds(CTX_ROWS + last_valid - (CONV_W - 1), CONV_W - 1), :]

    nxt = uc_ref[rows:rows + CTX_ROWS, :]
    uc_ref[0:CTX_ROWS, :] = nxt


def _conv(proj, ctx0, conv_w, conv_b, ln_g, ln_b, layer, nb, n_chunks, rows, row_blk0,
          last_valid, name):
    body = functools.partial(_conv_body, rows=rows, n_chunks=n_chunks, last_valid=last_valid)
    half = W_C // 2
    hb = half // 128

    def in_spec(col):
        return pl.BlockSpec((rows, half), lambda b, c: (row_blk0 + b * n_chunks + c, col))

    def par_spec():
        return pl.BlockSpec((None, 1, W_C), lambda b, c: (layer, 0, 0))

    return pl.pallas_call(
        body,
        grid=(nb, n_chunks),
        in_specs=[in_spec(COL_CA // hb), in_spec(COL_CA // hb + 1),
                  in_spec(COL_CB // hb), in_spec(COL_CB // hb + 1),
                  pl.BlockSpec((None, CTX_ROWS, W_C), lambda b, c: (b, 0, 0)),
                  pl.BlockSpec((None, CONV_W, W_C), lambda b, c: (layer, 0, 0)),
                  par_spec(), par_spec(), par_spec()],
        out_specs=[pl.BlockSpec((rows, W_C), lambda b, c: (b * n_chunks + c, 0)),
                   pl.BlockSpec((None, CONV_W - 1, W_C), lambda b, c: (b, 0, 0))],
        out_shape=[jax.ShapeDtypeStruct((nb * n_chunks * rows, W_C), BF16),
                   jax.ShapeDtypeStruct((nb, CONV_W - 1, W_C), F32)],
        scratch_shapes=[pltpu.VMEM((CTX_ROWS + rows, W_C), F32)],
        compiler_params=_cparams(("parallel", "arbitrary"), VMEM_LIMIT),
        name=name,
    )(proj, proj, proj, proj, ctx0, conv_w, conv_b, ln_g, ln_b)


def _t5_bias(rel_bias, dist):
    n = jnp.maximum(dist, 0)
    nf = jnp.maximum(n, 1).astype(F32)
    large = REL_EXACT + (jnp.log(nf / REL_EXACT) / math.log(REL_MAX_DIST / REL_EXACT)
                         * (N_BUCKETS - REL_EXACT)).astype(jnp.int32)
    bucket = jnp.where(n < REL_EXACT, n, jnp.minimum(large, N_BUCKETS - 1))
    onehot = (bucket[..., None] == jnp.arange(N_BUCKETS, dtype=jnp.int32)).astype(F32)
    bias = jnp.einsum("...k,kh->h...", onehot, rel_bias.astype(F32),
                      precision=lax.Precision.HIGHEST)
    return jnp.where(dist[None] >= 0, bias, NEG_INF)


def _rotary_tables(pos):
    inv = ROPE_BASE ** (-jnp.arange(HALF, dtype=F32) / HALF)
    ang = pos.astype(F32)[:, None] * inv[None, :]
    cos, sin = jnp.cos(ang), jnp.sin(ang)
    return jnp.concatenate([cos, cos], axis=1), jnp.concatenate([-sin, sin], axis=1)


def kernel(x_prompt, x_sample, cache_k, cache_v, state_ret, state_conv, page_table, meta, rel_bias,
           norm_mix, w_in, lam_q1, lam_k1, lam_q2, lam_k2, subln_a, gn_b, conv_w, conv_b, conv_ln_g,
           conv_ln_b, w_out, norm_ffn, w_gate, w_up, w_down, norm_final):
    nb, seq = x_prompt.shape[:2]
    db, n_new = x_sample.shape[:2]
    depth = w_in.shape[0]
    n_pages = page_table.shape[1]
    past = n_pages * PAGE_SIZE
    t_valid = seq + N_META
    nblk = -(-t_valid // BLK)
    tp = nblk * BLK
    last_valid = t_valid - (nblk - 1) * BLK
    mp = nb * tp
    m_all = mp + db * SROWS
    s_blk0 = mp // SROWS
    d_ff = w_gate.shape[2]

    tm = m_all // 6
    tr = m_all // 24

    pad_p = jnp.zeros((tp - t_valid, D_MODEL), F32)
    pad_s = jnp.zeros((SROWS - n_new, D_MODEL), F32)
    pieces = []
    for b in range(nb):
        pieces += [meta.astype(F32), x_prompt[b], pad_p]
    for b in range(db):
        pieces += [x_sample[b], pad_s]
    h = jnp.concatenate(pieces, axis=0)

    assert BLK >= REL_MAX_DIST and PAGE_SIZE >= REL_MAX_DIST and nblk >= 2
    far = _t5_bias(rel_bias, jnp.full((1, 1), REL_MAX_DIST, jnp.int32))
    d0 = (nblk - 1) * BLK
    ii = jnp.arange(BLK, dtype=jnp.int32)[None, :]
    cn = jnp.arange(d0 - BLK, d0 + BLK, dtype=jnp.int32)[:, None]
    near = _t5_bias(rel_bias, d0 + ii - cn)
    band = jnp.concatenate([jnp.broadcast_to(far, (H_A, d0 - BLK, BLK)), near,
                            jnp.full((H_A, d0, BLK), NEG_INF, F32)], axis=1)
    r_new = jnp.arange(SROWS, dtype=jnp.int32) % n_new
    jk = jnp.arange(PAGE_SIZE, dtype=jnp.int32)[None, :]
    dist_s = jnp.stack([jnp.full((SROWS, PAGE_SIZE), REL_MAX_DIST, jnp.int32),
                        PAGE_SIZE + r_new[:, None] - jk,
                        jnp.where(jk < n_new, r_new[:, None] - jk, -1)])
    bias_s = jnp.transpose(_t5_bias(rel_bias, dist_s), (1, 0, 2, 3))
    cache_kh = jnp.transpose(cache_k, (0, 1, 3, 2, 4))
    cache_vh = jnp.transpose(cache_v, (0, 1, 3, 2, 4))
    c2_p, s2_p = _rotary_tables(jnp.arange(tp, dtype=jnp.int32))
    c2_s, s2_s = _rotary_tables(past + jnp.arange(SROWS, dtype=jnp.int32))
    log_gamma = jnp.log1p(-jnp.exp2(-5.0 - jnp.arange(H_B, dtype=F32)))
    lgrow = jnp.repeat(log_gamma, HEAD_DIM)[None, :]

    zero_state = jnp.zeros((nb, H_B, HEAD_DIM, HEAD_DIM), F32)
    zero_ctx = jnp.zeros((nb, CTX_ROWS, W_C), F32)
    conv_b3 = conv_b[:, None, :]
    ln_g3 = conv_ln_g[:, None, :]
    ln_b3 = conv_ln_b[:, None, :]

    projs, ret_p, conv_p, k_s, v_s, ret_s, conv_s = [], [], [], [], [], [], []
    for l in range(depth):
        lam_init = 0.8 - 0.6 * math.exp(-0.3 * l)
        lamp = jnp.stack([lam_q1[l], lam_k1[l], lam_q2[l], lam_k2[l]]).astype(F32)
        sub = subln_a[l][None, :]
        gn = gn_b[l][None, :]

        n1 = _rmsnorm(h, norm_mix[l], BF16, tr, f"norm_mix{l}")
        proj = _matmul(n1, w_in, l, tm, 256, f"in_proj{l}")

        ya_p = _attn_prompt(proj, band, lamp, sub, nb, nblk, lam_init, f"attn_prompt{l}")
        ya_s = _attn_sample(page_table, proj, cache_kh, cache_vh, bias_s, lamp, sub, l, s_blk0,
                            n_new, lam_init, f"attn_sample{l}")
        yb_p, st_p = _retention(proj, c2_p, s2_p, lgrow, gn, zero_state, nb, nblk, BLK, 0,
                                last_valid, f"ret_prompt{l}")
        yb_s, st_s = _retention(proj, c2_s, s2_s, lgrow, gn, state_ret[l], db, 1, SROWS,
                                s_blk0, n_new, f"ret_sample{l}")
        yc_p, tail_p = _conv(proj, zero_ctx, conv_w, conv_b3, ln_g3, ln_b3, l, nb, nblk, BLK, 0,
                             last_valid, f"conv_prompt{l}")
        ctx_s = jnp.pad(state_conv[l], ((0, 0), (CTX_ROWS - (CONV_W - 1), 0), (0, 0)))
        yc_s, tail_s = _conv(proj, ctx_s, conv_w, conv_b3, ln_g3, ln_b3, l, db, 1, SROWS, s_blk0,
                             n_new, f"conv_sample{l}")

        h = _out_proj(ya_p, yb_p, yc_p, w_out, l, h, 0, mp // 8, 256, f"out_proj_p{l}")
        h = _out_proj(ya_s, yb_s, yc_s, w_out, l, h, mp // (db * SROWS), db * SROWS, 256,
                      f"out_proj_s{l}")
        n2 = _rmsnorm(h, norm_ffn[l], BF16, tr, f"norm_ffn{l}")
        hid = _ffn_up(n2, w_gate, w_up, l, tm, 256, f"ffn_up{l}")
        for kb in range(2):
            h = _matmul(hid, w_down, l, m_all // 8, 256, f"ffn_down{l}_{kb}", res=h,
                        tk=d_ff // 2, kblk=kb)

        projs.append(proj)
        kvs = proj[mp:, W_A:3 * W_A].reshape(db, SROWS, 2, H_A, HEAD_DIM)[:, :n_new]
        k_s.append(kvs[:, :, 0])
        v_s.append(kvs[:, :, 1])
        ret_p.append(st_p)
        ret_s.append(st_s)
        conv_p.append(tail_p)
        conv_s.append(tail_s)

    y_prompt = _final_norm_prompt(h, norm_final, nb, seq, nblk, "norm_final_p")
    hs = _rmsnorm(h[mp:], norm_final, F32, db * SROWS, "norm_final_s")
    y_sample = hs.reshape(db, SROWS, D_MODEL)[:, :n_new]
    k_hm, v_hm = _kv_out(projs, nb, nblk, t_valid, "kv_out")
    k_prompt = jnp.transpose(k_hm, (0, 1, 3, 2, 4))
    v_prompt = jnp.transpose(v_hm, (0, 1, 3, 2, 4))
    return (y_prompt, y_sample, k_prompt, v_prompt, jnp.stack(ret_p), jnp.stack(conv_p),
            jnp.stack(k_s), jnp.stack(v_s), jnp.stack(ret_s), jnp.stack(conv_s))
```

```python
import functools
import math

import jax
import jax.numpy as jnp
from jax import lax
from jax.experimental import pallas as pl
from jax.experimental.pallas import tpu as pltpu

F32 = jnp.float32
BF16 = jnp.bfloat16

D_MODEL = 4096
N_META = 16
HEAD_DIM = 128
HALF = HEAD_DIM // 2
H_A = 12
H_B = 12
W_A = H_A * HEAD_DIM
W_B = H_B * HEAD_DIM
W_C = D_MODEL - W_A - W_B
N_IN = 3 * W_A + 4 * W_B + 2 * W_C
CONV_W = 31
N_BUCKETS = 32
REL_EXACT = 16
REL_MAX_DIST = 128
ROPE_BASE = 10000.0
EPS = 1e-6
NEG_INF = -1e30
PAGE_SIZE = 128

BLK = 128
SROWS = 16
CTX_ROWS = 32
M_INIT = -3.0e38
VMEM_LIMIT = 56 * 1024 * 1024

COL_QA, COL_KA, COL_VA = 0, W_A // 128, 2 * W_A // 128
COL_QB = 3 * W_A // 128
COL_KB = COL_QB + W_B // 128
COL_VB = COL_KB + W_B // 128
COL_GB = COL_VB + W_B // 128
COL_CA = COL_GB + W_B // 128
COL_CB = COL_CA + W_C // 128


def _cparams(sem, vmem=None):
    return pltpu.CompilerParams(dimension_semantics=sem, vmem_limit_bytes=vmem)


def _sigmoid(x):
    return 1.0 / (1.0 + jnp.exp(-x))


def _rmsnorm_body(x_ref, g_ref, o_ref):
    x = x_ref[...]
    ms = jnp.mean(x * x, axis=-1, keepdims=True)
    o_ref[...] = (x * lax.rsqrt(ms + EPS) * g_ref[...]).astype(o_ref.dtype)


def _rmsnorm(x, g, out_dtype, tr, name):
    m, d = x.shape
    return pl.pallas_call(
        _rmsnorm_body,
        grid=(m // tr,),
        in_specs=[pl.BlockSpec((tr, d), lambda i: (i, 0)),
                  pl.BlockSpec((1, d), lambda i: (0, 0))],
        out_specs=pl.BlockSpec((tr, d), lambda i: (i, 0)),
        out_shape=jax.ShapeDtypeStruct((m, d), out_dtype),
        compiler_params=_cparams(("parallel",)),
        name=name,
    )(x, g.reshape(1, d))


def _mm_body(a_ref, w_ref, o_ref):
    o_ref[...] = jnp.dot(a_ref[...], w_ref[...].astype(BF16), preferred_element_type=F32)


def _mm_res_body(a_ref, w_ref, r_ref, o_ref):
    o_ref[...] = r_ref[...] + jnp.dot(a_ref[...], w_ref[...].astype(BF16),
                                      preferred_element_type=F32)


def _matmul(a, w, layer, tm, tn, name, res=None, tk=None, kblk=0):
    m = a.shape[0]
    k = a.shape[1] if tk is None else tk
    n = w.shape[2]
    in_specs = [pl.BlockSpec((tm, k), lambda i, j: (i, kblk)),
                pl.BlockSpec((None, k, tn), lambda i, j: (layer, kblk, j))]
    args = [a, w]
    body = _mm_body
    if res is not None:
        in_specs.append(pl.BlockSpec((tm, tn), lambda i, j: (i, j)))
        args.append(res)
        body = _mm_res_body
    return pl.pallas_call(
        body,
        grid=(m // tm, n // tn),
        in_specs=in_specs,
        out_specs=pl.BlockSpec((tm, tn), lambda i, j: (i, j)),
        out_shape=jax.ShapeDtypeStruct((m, n), F32),
        compiler_params=_cparams(("parallel", "parallel"), VMEM_LIMIT),
        name=name,
    )(*args)


def _ffn_up_body(a_ref, wg_ref, wu_ref, o_ref):
    a = a_ref[...]
    g = jnp.dot(a, wg_ref[...].astype(BF16), preferred_element_type=F32)
    u = jnp.dot(a, wu_ref[...].astype(BF16), preferred_element_type=F32)
    o_ref[...] = (g * _sigmoid(g) * u).astype(o_ref.dtype)


def _ffn_up(a, wg, wu, layer, tm, tn, name):
    m, k = a.shape
    n = wg.shape[2]
    wspec = pl.BlockSpec((None, k, tn), lambda i, j: (layer, 0, j))
    return pl.pallas_call(
        _ffn_up_body,
        grid=(m // tm, n // tn),
        in_specs=[pl.BlockSpec((tm, k), lambda i, j: (i, 0)), wspec, wspec],
        out_specs=pl.BlockSpec((tm, tn), lambda i, j: (i, j)),
        out_shape=jax.ShapeDtypeStruct((m, n), BF16),
        compiler_params=_cparams(("parallel", "parallel"), VMEM_LIMIT),
        name=name,
    )(a, wg, wu)


def _out_proj_body(ya_ref, yb_ref, yc_ref, w_ref, r_ref, o_ref):
    acc = jnp.dot(ya_ref[...], w_ref[0:W_A, :].astype(BF16), preferred_element_type=F32)
    acc += jnp.dot(yb_ref[...], w_ref[W_A:W_A + W_B, :].astype(BF16), preferred_element_type=F32)
    acc += jnp.dot(yc_ref[...], w_ref[W_A + W_B:, :].astype(BF16), preferred_element_type=F32)
    o_ref[...] = r_ref[...] + acc


def _out_proj(ya, yb, yc, w, layer, h, row_tile0, tm, tn, name):
    rows = ya.shape[0]
    n = w.shape[2]
    hspec = pl.BlockSpec((tm, tn), lambda i, j: (row_tile0 + i, j))
    return pl.pallas_call(
        _out_proj_body,
        grid=(rows // tm, n // tn),
        in_specs=[pl.BlockSpec((tm, W_A), lambda i, j: (i, 0)),
                  pl.BlockSpec((tm, W_B), lambda i, j: (i, 0)),
                  pl.BlockSpec((tm, W_C), lambda i, j: (i, 0)),
                  pl.BlockSpec((None, W_A + W_B + W_C, tn), lambda i, j: (layer, 0, j)),
                  hspec],
        out_specs=hspec,
        out_shape=jax.ShapeDtypeStruct(h.shape, F32),
        input_output_aliases={4: 0},
        compiler_params=_cparams(("parallel", "parallel"), VMEM_LIMIT),
        name=name,
    )(ya, yb, yc, w, h)


def _final_norm_body(a_ref, b_ref, g_ref, o_ref):
    x = jnp.concatenate([a_ref[N_META:, :], b_ref[0:N_META, :]], axis=0)
    ms = jnp.mean(x * x, axis=-1, keepdims=True)
    o_ref[...] = x * lax.rsqrt(ms + EPS) * g_ref[...]


def _final_norm_prompt(h, g, nb, seq, nblk, name):
    assert seq % BLK == 0 and N_META % 8 == 0 and N_META <= BLK and seq // BLK < nblk
    d = h.shape[1]
    return pl.pallas_call(
        _final_norm_body,
        grid=(nb, seq // BLK),
        in_specs=[pl.BlockSpec((BLK, d), lambda b, j: (b * nblk + j, 0)),
                  pl.BlockSpec((BLK, d), lambda b, j: (b * nblk + j + 1, 0)),
                  pl.BlockSpec((1, d), lambda b, j: (0, 0))],
        out_specs=pl.BlockSpec((None, BLK, d), lambda b, j: (b, j, 0)),
        out_shape=jax.ShapeDtypeStruct((nb, seq, d), F32),
        compiler_params=_cparams(("parallel", "parallel")),
        name=name,
    )(h, h, g.reshape(1, d))


def _kv_out_body(*refs, depth, t_valid):
    ko_ref, vo_ref = refs[2 * depth:]
    for l in range(depth):
        ko_ref[l] = refs[2 * l][0:t_valid, :]
        vo_ref[l] = refs[2 * l + 1][0:t_valid, :]


def _kv_out(projs, nb, nblk, t_valid, name):
    depth = len(projs)
    tp = nblk * BLK
    in_specs, args = [], []
    for proj in projs:
        in_specs += [pl.BlockSpec((tp, HEAD_DIM), lambda b, h: (b, COL_KA + h)),
                     pl.BlockSpec((tp, HEAD_DIM), lambda b, h: (b, COL_VA + h))]
        args += [proj, proj]
    ospec = pl.BlockSpec((depth, None, None, t_valid, HEAD_DIM), lambda b, h: (0, b, h, 0, 0))
    oshape = jax.ShapeDtypeStruct((depth, nb, H_A, t_valid, HEAD_DIM), F32)
    return pl.pallas_call(
        functools.partial(_kv_out_body, depth=depth, t_valid=t_valid),
        grid=(nb, H_A),
        in_specs=in_specs,
        out_specs=[ospec, ospec],
        out_shape=[oshape, oshape],
        compiler_params=_cparams(("parallel", "parallel")),
        name=name,
    )(*args)


def _lambda(lamp_ref, lam_init):
    lp = lamp_ref[...]
    s1 = jnp.sum(lp[0:1] * lp[1:2], axis=-1, keepdims=True)
    s2 = jnp.sum(lp[2:3] * lp[3:4], axis=-1, keepdims=True)
    return jnp.exp(s1) - jnp.exp(s2) + lam_init


def _split_maps(q):
    lane = lax.broadcasted_iota(jnp.int32, q.shape, 1)
    lo = jnp.where(lane < HALF, q, 0.0)
    hi = jnp.where(lane >= HALF, q, 0.0)
    return jnp.concatenate([lo, hi], axis=0)


def _subln(a, sub_ref, lam_init):
    y = a * lax.rsqrt(jnp.mean(a * a, axis=-1, keepdims=True) + EPS) * sub_ref[...]
    return y * (1.0 - lam_init)


ATT_HG = 4


def _attn_prompt_body(q_ref, k_ref, v_ref, band_ref, lamp_ref, sub_ref, o_ref, kb_ref, vt_ref,
                      s_ref, p_ref, *, lam_init, nblk):
    qi = pl.program_id(2)

    @pl.when(qi == 0)
    def _():
        kb_ref[...] = k_ref[...].astype(BF16)
        for g in range(ATT_HG):
            for j in range(nblk):
                vt = v_ref[j * BLK:(j + 1) * BLK, g * HEAD_DIM:(g + 1) * HEAD_DIM].T
                vt_ref[g, :, j * BLK:(j + 1) * BLK] = vt.astype(BF16)

    def attend(nt):
        lanes = [slice(g * HEAD_DIM, (g + 1) * HEAD_DIM) for g in range(ATT_HG)]
        lam = _lambda(lamp_ref, lam_init)
        mrun = [None] * ATT_HG
        lrun = [None] * ATT_HG
        m = [None] * ATT_HG
        qst = [None] * ATT_HG

        def score_tile(g, j):
            if j == 0:
                qst[g] = _split_maps(q_ref[:, lanes[g]] * (HALF ** -0.5)).T.astype(BF16)
            bt = band_ref[g, jnp.clip(j - qi + 2, 0, 3)]
            s = jnp.dot(kb_ref[j * BLK:(j + 1) * BLK, lanes[g]], qst[g],
                        preferred_element_type=F32)
            s = s + jnp.concatenate([bt, bt], axis=1)
            s_ref[g, j] = s
            mrun[g] = s if mrun[g] is None else jnp.maximum(mrun[g], s)
            if j == nt - 1:
                m[g] = jnp.max(mrun[g], axis=0, keepdims=True)

        def exp_tile(g, j):
            p = jnp.exp(s_ref[g, j] - m[g])
            lrun[g] = p if lrun[g] is None else lrun[g] + p
            p_ref[g, j * BLK:(j + 1) * BLK, :] = p.astype(BF16)
            if j == nt - 1:
                l = jnp.sum(lrun[g], axis=0, keepdims=True)
                ot = jnp.dot(vt_ref[g, :, 0:nt * BLK], p_ref[g, 0:nt * BLK, :],
                             preferred_element_type=F32) / l
                at = ot[:, :BLK] - lam * ot[:, BLK:]
                at = at * lax.rsqrt(jnp.mean(at * at, axis=0, keepdims=True) + EPS)
                o_ref[:, lanes[g]] = (at.T * sub_ref[...] * (1.0 - lam_init)).astype(o_ref.dtype)

        for step in range(ATT_HG + 1):
            for j in range(nt):
                if step < ATT_HG:
                    score_tile(step, j)
                if step >= 1:
                    exp_tile(step - 1, j)

    lo = 0
    for hi in sorted(set(range(2, nblk, 3)) | {nblk}):
        pl.when(jnp.logical_and(qi >= lo, qi < hi))(functools.partial(attend, hi))
        lo = hi


def _attn_prompt(proj, band, lamp, sub, nb, nblk, lam_init, name):
    tp = nblk * BLK
    wg = ATT_HG * HEAD_DIM
    body = functools.partial(_attn_prompt_body, lam_init=lam_init, nblk=nblk)
    return pl.pallas_call(
        body,
        grid=(nb, H_A // ATT_HG, nblk),
        in_specs=[
            pl.BlockSpec((BLK, wg), lambda b, h, qi: (b * nblk + qi, COL_QA // ATT_HG + h)),
            pl.BlockSpec((tp, wg), lambda b, h, qi: (b, COL_KA // ATT_HG + h)),
            pl.BlockSpec((tp, wg), lambda b, h, qi: (b, COL_VA // ATT_HG + h)),
            pl.BlockSpec((ATT_HG, 4, BLK, BLK), lambda b, h, qi: (h, 0, 0, 0)),
            pl.BlockSpec((4, HALF), lambda b, h, qi: (0, 0)),
            pl.BlockSpec((1, HEAD_DIM), lambda b, h, qi: (0, 0)),
        ],
        out_specs=pl.BlockSpec((BLK, wg), lambda b, h, qi: (b * nblk + qi, h)),
        out_shape=jax.ShapeDtypeStruct((nb * tp, W_A), BF16),
        scratch_shapes=[pltpu.VMEM((tp, wg), BF16), pltpu.VMEM((ATT_HG, HEAD_DIM, tp), BF16),
                        pltpu.VMEM((ATT_HG, nblk, BLK, 2 * BLK), F32),
                        pltpu.VMEM((ATT_HG, tp, 2 * BLK), BF16)],
        compiler_params=_cparams(("parallel", "parallel", "arbitrary"), VMEM_LIMIT),
        name=name,
    )(proj, proj, proj, band, lamp, sub)


PPS = 4


def _attn_sample_body(pt_ref, q_ref, kn_ref, vn_ref, *rest, lam_init, n_pages, n_new):
    kp_refs, vp_refs = rest[:PPS], rest[PPS:2 * PPS]
    bias_ref, lamp_ref, sub_ref, o_ref, qs_ref, m_ref, l_ref, acc_ref = rest[2 * PPS:]
    p = pl.program_id(1)
    n_steps = n_pages // PPS

    @pl.when(p == 0)
    def _():
        for h in range(H_A):
            q = q_ref[0:n_new, h * HEAD_DIM:(h + 1) * HEAD_DIM] * (HALF ** -0.5)
            qs = _split_maps(q)
            pad = jnp.zeros((SROWS - 2 * n_new, HEAD_DIM), F32)
            qs_ref[h] = jnp.concatenate([qs, pad], axis=0).astype(BF16)
        m_ref[...] = jnp.full(m_ref.shape, M_INIT, F32)
        l_ref[...] = jnp.zeros(l_ref.shape, F32)
        acc_ref[...] = jnp.zeros(acc_ref.shape, F32)

    def update(get_k, get_v, bias, n_rep):
        s = jnp.stack([lax.dot_general(qs_ref[h], get_k(h).astype(BF16), (((1,), (1,)), ((), ())),
                                       preferred_element_type=F32) for h in range(H_A)])
        s = s + bias
        m_prev = m_ref[...]
        m_new = jnp.maximum(m_prev, jnp.max(s, axis=-1, keepdims=True))
        alpha = jnp.exp(m_prev - m_new)
        pm = jnp.exp(s - jnp.concatenate([m_new] * n_rep, axis=-1))
        l_ref[...] = alpha * l_ref[...] + jnp.sum(pm, axis=-1, keepdims=True)
        m_ref[...] = m_new
        pb = pm.astype(BF16)
        for h in range(H_A):
            acc_ref[h] = alpha[h] * acc_ref[h] + jnp.dot(pb[h], get_v(h).astype(BF16),
                                                         preferred_element_type=F32)

    @pl.when(p < n_steps)
    def _():
        def pages(refs, h):
            return jnp.concatenate([r[h] for r in refs], axis=0)

        last_idx = jnp.where(p == n_steps - 1, 1, 0)
        bias = jnp.concatenate([bias_ref[0]] * (PPS - 1) + [bias_ref[last_idx]], axis=-1)
        update(functools.partial(pages, kp_refs), functools.partial(pages, vp_refs), bias, PPS)

    @pl.when(p == n_steps)
    def _():
        zpad = jnp.zeros((PAGE_SIZE - SROWS, HEAD_DIM), F32)

        def new_rows(ref, h):
            return jnp.concatenate([ref[:, h * HEAD_DIM:(h + 1) * HEAD_DIM], zpad], axis=0)

        update(functools.partial(new_rows, kn_ref), functools.partial(new_rows, vn_ref),
               bias_ref[2], 1)
        lam = _lambda(lamp_ref, lam_init)
        for h in range(H_A):
            o = acc_ref[h] / l_ref[h]
            a = o[0:n_new] - lam * o[n_new:2 * n_new]
            y = _subln(a, sub_ref, lam_init)
            pad = jnp.zeros((SROWS - n_new, HEAD_DIM), F32)
            o_ref[:, h * HEAD_DIM:(h + 1) * HEAD_DIM] = (
                jnp.concatenate([y, pad], axis=0).astype(o_ref.dtype))


def _attn_sample(page_table, proj, cache_k, cache_v, bias, lamp, sub, layer, row_blk0, n_new,
                 lam_init, name):
    db, n_pages = page_table.shape
    assert n_pages % PPS == 0
    body = functools.partial(_attn_sample_body, lam_init=lam_init, n_pages=n_pages, n_new=n_new)

    def page_spec(t):
        return pl.BlockSpec(
            (None, None, H_A, PAGE_SIZE, HEAD_DIM),
            lambda b, p, pt: (layer, pt[b, jnp.minimum(p * PPS + t, n_pages - 1)], 0, 0, 0))

    page_specs = [page_spec(t) for t in range(PPS)]
    grid_spec = pltpu.PrefetchScalarGridSpec(
        num_scalar_prefetch=1,
        grid=(db, n_pages // PPS + 1),
        in_specs=[
            pl.BlockSpec((SROWS, W_A), lambda b, p, pt: (row_blk0 + b, 0)),
            pl.BlockSpec((SROWS, W_A), lambda b, p, pt: (row_blk0 + b, 1)),
            pl.BlockSpec((SROWS, W_A), lambda b, p, pt: (row_blk0 + b, 2)),
            *page_specs, *page_specs,
            pl.BlockSpec((3, H_A, SROWS, PAGE_SIZE), lambda b, p, pt: (0, 0, 0, 0)),
            pl.BlockSpec((4, HALF), lambda b, p, pt: (0, 0)),
            pl.BlockSpec((1, HEAD_DIM), lambda b, p, pt: (0, 0)),
        ],
        out_specs=pl.BlockSpec((SROWS, W_A), lambda b, p, pt: (b, 0)),
        scratch_shapes=[pltpu.VMEM((H_A, SROWS, HEAD_DIM), BF16),
                        pltpu.VMEM((H_A, SROWS, HEAD_DIM), F32),
                        pltpu.VMEM((H_A, SROWS, HEAD_DIM), F32),
                        pltpu.VMEM((H_A, SROWS, HEAD_DIM), F32)],
    )
    return pl.pallas_call(
        body,
        grid_spec=grid_spec,
        out_shape=jax.ShapeDtypeStruct((db * SROWS, W_A), BF16),
        compiler_params=_cparams(("parallel", "arbitrary"), VMEM_LIMIT),
        name=name,
    )(page_table, proj, proj, proj, *([cache_k] * PPS), *([cache_v] * PPS), bias, lamp, sub)


RET_HG = 4


def _retention_body(q_ref, k_ref, v_ref, g_ref, c2_ref, s2_ref, lg_ref, gn_ref, s0_ref,
                    y_ref, sout_ref, st_ref, *, rows, n_chunks, last_valid):
    c = pl.program_id(2)

    @pl.when(c == 0)
    def _():
        st_ref[...] = s0_ref[...]

    lc = jnp.where(c == n_chunks - 1, float(last_valid), float(BLK))
    ii = lax.broadcasted_iota(jnp.int32, (BLK, BLK), 0).astype(F32)
    jj = lax.broadcasted_iota(jnp.int32, (BLK, BLK), 1).astype(F32)
    rel = ii - jj

    def rows128(x):
        if rows == BLK:
            return x
        return jnp.concatenate([x, jnp.zeros((BLK - rows, x.shape[1]), x.dtype)], axis=0)

    c2 = rows128(c2_ref[...])
    s2 = rows128(s2_ref[...])

    def rot(x):
        return x * c2 + pltpu.roll(x, HALF, 1) * s2

    heads = []
    for j in range(RET_HG):
        sl = slice(j * HEAD_DIM, (j + 1) * HEAD_DIM)
        lgv = lg_ref[:, sl]
        v = rows128(v_ref[:, sl]).astype(BF16)
        qr = rot(rows128(q_ref[:, sl])).astype(BF16)
        kr = rot(rows128(k_ref[:, sl])) * (HEAD_DIM ** -0.5)
        st = st_ref[j]
        sc = lax.dot_general(qr, kr.astype(BF16), (((1,), (1,)), ((), ())),
                             preferred_element_type=F32)
        cross = jnp.dot(qr, st.astype(BF16), preferred_element_type=F32)
        kdec = kr * jnp.where(ii < lc, jnp.exp((lc - 1.0 - ii) * lgv), 0.0)
        st_ref[j] = jnp.exp(lc * lgv) * st + jnp.dot(kdec.T.astype(BF16), v,
                                                     preferred_element_type=F32)
        heads.append((sl, lgv, v, sc, cross))
    for sl, lgv, v, sc, cross in heads:
        decay = jnp.where(rel >= 0, jnp.exp(lgv * jnp.maximum(rel, 0.0)), 0.0)
        o = jnp.dot((sc * decay).astype(BF16), v, preferred_element_type=F32)
        o = o + cross * jnp.exp((ii + 1.0) * lgv)
        y = o[:rows]
        y = y * lax.rsqrt(jnp.mean(y * y, axis=-1, keepdims=True) + EPS) * gn_ref[...]
        g = g_ref[:, sl]
        y_ref[:, sl] = (y * (g * _sigmoid(g))).astype(y_ref.dtype)

    @pl.when(c == n_chunks - 1)
    def _():
        sout_ref[...] = st_ref[...]


def _retention(proj, c2, s2, lgrow, gn, s0, nb, n_chunks, rows, row_blk0, last_valid, name):
    body = functools.partial(_retention_body, rows=rows, n_chunks=n_chunks, last_valid=last_valid)
    wb = RET_HG * HEAD_DIM
    cb = wb // 128

    def in_spec(col0):
        return pl.BlockSpec((rows, wb),
                            lambda b, hg, c: (row_blk0 + b * n_chunks + c, col0 // cb + hg))

    st_spec = pl.BlockSpec((None, RET_HG, HEAD_DIM, HEAD_DIM), lambda b, hg, c: (b, hg, 0, 0))
    return pl.pallas_call(
        body,
        grid=(nb, H_B // RET_HG, n_chunks),
        in_specs=[in_spec(COL_QB), in_spec(COL_KB), in_spec(COL_VB), in_spec(COL_GB),
                  pl.BlockSpec((rows, HEAD_DIM), lambda b, hg, c: (c, 0)),
                  pl.BlockSpec((rows, HEAD_DIM), lambda b, hg, c: (c, 0)),
                  pl.BlockSpec((1, wb), lambda b, hg, c: (0, hg)),
                  pl.BlockSpec((1, HEAD_DIM), lambda b, hg, c: (0, 0)),
                  st_spec],
        out_specs=[pl.BlockSpec((rows, wb), lambda b, hg, c: (b * n_chunks + c, hg)), st_spec],
        out_shape=[jax.ShapeDtypeStruct((nb * n_chunks * rows, W_B), BF16),
                   jax.ShapeDtypeStruct((nb, H_B, HEAD_DIM, HEAD_DIM), F32)],
        scratch_shapes=[pltpu.VMEM((RET_HG, HEAD_DIM, HEAD_DIM), F32)],
        compiler_params=_cparams(("parallel", "parallel", "arbitrary"), VMEM_LIMIT),
        name=name,
    )(proj, proj, proj, proj, c2, s2, lgrow, gn, s0)


CONV_LANES = 256


def _conv_body(ca0_ref, ca1_ref, cb0_ref, cb1_ref, ctx_ref, w_ref, b_ref, lg_ref, lb_ref,
               y_ref, tail_ref, uc_ref, us_ref, cv_ref, *, rows, n_chunks, last_valid):
    c = pl.program_id(1)

    @pl.when(c == 0)
    def _():
        uc_ref[0:CTX_ROWS, :] = ctx_ref[...]

    ca = jnp.concatenate([ca0_ref[...], ca1_ref[...]], axis=1)
    cb = jnp.concatenate([cb0_ref[...], cb1_ref[...]], axis=1)
    uc_ref[CTX_ROWS:CTX_ROWS + rows, :] = ca * _sigmoid(cb)
    span = CTX_ROWS + rows - 8
    for r in range(1, 8):
        us_ref[r] = uc_ref[pl.ds(r, span), :]
    for cc in range(0, W_C, CONV_LANES):
        cs = slice(cc, cc + CONV_LANES)
        acc = jnp.broadcast_to(b_ref[:, cs], (rows, CONV_LANES))
        for w in range(CONV_W):
            off = CTX_ROWS - (CONV_W - 1) + w
            r, a = off % 8, off - off % 8
            src = uc_ref[a:a + rows, cs] if r == 0 else us_ref[r, a:a + rows, cs]
            acc = acc + src * w_ref[w:w + 1, cs]
        cv_ref[:, cs] = acc
    acc = cv_ref[...]
    mu = jnp.mean(acc, axis=-1, keepdims=True)
    xc = acc - mu
    var = jnp.mean(xc * xc, axis=-1, keepdims=True)
    y = xc * lax.rsqrt(var + EPS) * lg_ref[...] + lb_ref[...]
    y_ref[...] = (y * _sigmoid(y)).astype(y_ref.dtype)

    @pl.when(c == n_chunks - 1)
    def _():
        tail_ref[...] = uc_ref[pl.ds(CTX_ROWS + last_valid - (CONV_W - 1), CONV_W - 1), :]

    nxt = uc_ref[rows:rows + CTX_ROWS, :]
    uc_ref[0:CTX_ROWS, :] = nxt


def _conv(proj, ctx0, conv_w, conv_b, ln_g, ln_b, layer, nb, n_chunks, rows, row_blk0,
          last_valid, name):
    body = functools.partial(_conv_body, rows=rows, n_chunks=n_chunks, last_valid=last_valid)
    half = W_C // 2
    hb = half // 128

    def in_spec(col):
        return pl.BlockSpec((rows, half), lambda b, c: (row_blk0 + b * n_chunks + c, col))

    def par_spec():
        return pl.BlockSpec((None, 1, W_C), lambda b, c: (layer, 0, 0))

    return pl.pallas_call(
        body,
        grid=(nb, n_chunks),
        in_specs=[in_spec(COL_CA // hb), in_spec(COL_CA // hb + 1),
                  in_spec(COL_CB // hb), in_spec(COL_CB // hb + 1),
                  pl.BlockSpec((None, CTX_ROWS, W_C), lambda b, c: (b, 0, 0)),
                  pl.BlockSpec((None, CONV_W, W_C), lambda b, c: (layer, 0, 0)),
                  par_spec(), par_spec(), par_spec()],
        out_specs=[pl.BlockSpec((rows, W_C), lambda b, c: (b * n_chunks + c, 0)),
                   pl.BlockSpec((None, CONV_W - 1, W_C), lambda b, c: (b, 0, 0))],
        out_shape=[jax.ShapeDtypeStruct((nb * n_chunks * rows, W_C), BF16),
                   jax.ShapeDtypeStruct((nb, CONV_W - 1, W_C), F32)],
        scratch_shapes=[pltpu.VMEM((CTX_ROWS + rows, W_C), F32),
                        pltpu.VMEM((8, CTX_ROWS + rows - 8, W_C), F32),
                        pltpu.VMEM((rows, W_C), F32)],
        compiler_params=_cparams(("parallel", "arbitrary"), VMEM_LIMIT),
        name=name,
    )(proj, proj, proj, proj, ctx0, conv_w, conv_b, ln_g, ln_b)


def _t5_bias(rel_bias, dist):
    n = jnp.maximum(dist, 0)
    nf = jnp.maximum(n, 1).astype(F32)
    large = REL_EXACT + (jnp.log(nf / REL_EXACT) / math.log(REL_MAX_DIST / REL_EXACT)
                         * (N_BUCKETS - REL_EXACT)).astype(jnp.int32)
    bucket = jnp.where(n < REL_EXACT, n, jnp.minimum(large, N_BUCKETS - 1))
    onehot = (bucket[..., None] == jnp.arange(N_BUCKETS, dtype=jnp.int32)).astype(F32)
    bias = jnp.einsum("...k,kh->h...", onehot, rel_bias.astype(F32),
                      precision=lax.Precision.HIGHEST)
    return jnp.where(dist[None] >= 0, bias, NEG_INF)


def _rotary_tables(pos):
    inv = ROPE_BASE ** (-jnp.arange(HALF, dtype=F32) / HALF)
    ang = pos.astype(F32)[:, None] * inv[None, :]
    cos, sin = jnp.cos(ang), jnp.sin(ang)
    return jnp.concatenate([cos, cos], axis=1), jnp.concatenate([-sin, sin], axis=1)


def kernel(x_prompt, x_sample, cache_k, cache_v, state_ret, state_conv, page_table, meta, rel_bias,
           norm_mix, w_in, lam_q1, lam_k1, lam_q2, lam_k2, subln_a, gn_b, conv_w, conv_b, conv_ln_g,
           conv_ln_b, w_out, norm_ffn, w_gate, w_up, w_down, norm_final):
    nb, seq = x_prompt.shape[:2]
    db, n_new = x_sample.shape[:2]
    depth = w_in.shape[0]
    n_pages = page_table.shape[1]
    past = n_pages * PAGE_SIZE
    t_valid = seq + N_META
    nblk = -(-t_valid // BLK)
    tp = nblk * BLK
    last_valid = t_valid - (nblk - 1) * BLK
    mp = nb * tp
    m_all = mp + db * SROWS
    s_blk0 = mp // SROWS
    d_ff = w_gate.shape[2]

    tm = m_all // 6
    tr = m_all // 24

    hp = jnp.concatenate([jnp.broadcast_to(meta[None].astype(F32), (nb, N_META, D_MODEL)),
                          x_prompt, jnp.zeros((nb, tp - t_valid, D_MODEL), F32)], axis=1)
    hs = jnp.pad(x_sample, ((0, 0), (0, SROWS - n_new), (0, 0)))
    h = jnp.concatenate([hp.reshape(mp, D_MODEL), hs.reshape(db * SROWS, D_MODEL)], axis=0)

    assert BLK >= REL_MAX_DIST and PAGE_SIZE >= REL_MAX_DIST and nblk >= 2
    far = _t5_bias(rel_bias, jnp.full((1, 1), REL_MAX_DIST, jnp.int32))
    d0 = (nblk - 1) * BLK
    ii = jnp.arange(BLK, dtype=jnp.int32)[None, :]
    cn = jnp.arange(d0 - BLK, d0 + BLK, dtype=jnp.int32)[:, None]
    near = _t5_bias(rel_bias, d0 + ii - cn)
    band = jnp.stack([jnp.broadcast_to(far, (H_A, BLK, BLK)), near[:, :BLK], near[:, BLK:],
                      jnp.full((H_A, BLK, BLK), NEG_INF, F32)], axis=1)
    r_new = jnp.arange(SROWS, dtype=jnp.int32) % n_new
    jk = jnp.arange(PAGE_SIZE, dtype=jnp.int32)[None, :]
    dist_s = jnp.stack([jnp.full((SROWS, PAGE_SIZE), REL_MAX_DIST, jnp.int32),
                        PAGE_SIZE + r_new[:, None] - jk,
                        jnp.where(jk < n_new, r_new[:, None] - jk, -1)])
    bias_s = jnp.transpose(_t5_bias(rel_bias, dist_s), (1, 0, 2, 3))
    cache_kh = jnp.transpose(cache_k, (0, 1, 3, 2, 4))
    cache_vh = jnp.transpose(cache_v, (0, 1, 3, 2, 4))
    c2_p, s2_p = _rotary_tables(jnp.arange(tp, dtype=jnp.int32))
    c2_s, s2_s = _rotary_tables(past + jnp.arange(SROWS, dtype=jnp.int32))
    log_gamma = jnp.log1p(-jnp.exp2(-5.0 - jnp.arange(H_B, dtype=F32)))
    lgrow = jnp.repeat(log_gamma, HEAD_DIM)[None, :]

    zero_state = jnp.zeros((nb, H_B, HEAD_DIM, HEAD_DIM), F32)
    zero_ctx = jnp.zeros((nb, CTX_ROWS, W_C), F32)
    conv_b3 = conv_b[:, None, :]
    ln_g3 = conv_ln_g[:, None, :]
    ln_b3 = conv_ln_b[:, None, :]

    projs, ret_p, conv_p, k_s, v_s, ret_s, conv_s = [], [], [], [], [], [], []
    for l in range(depth):
        lam_init = 0.8 - 0.6 * math.exp(-0.3 * l)
        lamp = jnp.stack([lam_q1[l], lam_k1[l], lam_q2[l], lam_k2[l]]).astype(F32)
        sub = subln_a[l][None, :]
        gn = gn_b[l][None, :]

        n1 = _rmsnorm(h, norm_mix[l], BF16, tr, f"norm_mix{l}")
        proj = _matmul(n1, w_in, l, tm, 512, f"in_proj{l}")

        ya_p = _attn_prompt(proj, band, lamp, sub, nb, nblk, lam_init, f"attn_prompt{l}")
        ya_s = _attn_sample(page_table, proj, cache_kh, cache_vh, bias_s, lamp, sub, l, s_blk0,
                            n_new, lam_init, f"attn_sample{l}")
        yb_p, st_p = _retention(proj, c2_p, s2_p, lgrow, gn, zero_state, nb, nblk, BLK, 0,
                                last_valid, f"ret_prompt{l}")
        yb_s, st_s = _retention(proj, c2_s, s2_s, lgrow, gn, state_ret[l], db, 1, SROWS,
                                s_blk0, n_new, f"ret_sample{l}")
        yc_p, tail_p = _conv(proj, zero_ctx, conv_w, conv_b3, ln_g3, ln_b3, l, nb, nblk, BLK, 0,
                             last_valid, f"conv_prompt{l}")
        ctx_s = jnp.pad(state_conv[l], ((0, 0), (CTX_ROWS - (CONV_W - 1), 0), (0, 0)))
        yc_s, tail_s = _conv(proj, ctx_s, conv_w, conv_b3, ln_g3, ln_b3, l, db, 1, SROWS, s_blk0,
                             n_new, f"conv_sample{l}")

        h = _out_proj(ya_p, yb_p, yc_p, w_out, l, h, 0, mp // 8, 512, f"out_proj_p{l}")
        h = _out_proj(ya_s, yb_s, yc_s, w_out, l, h, mp // (db * SROWS), db * SROWS, 256,
                      f"out_proj_s{l}")
        n2 = _rmsnorm(h, norm_ffn[l], BF16, tr, f"norm_ffn{l}")
        hid = _ffn_up(n2, w_gate, w_up, l, tm, 256, f"ffn_up{l}")
        for kb in range(2):
            h = _matmul(hid, w_down, l, m_all // 8, 256, f"ffn_down{l}_{kb}", res=h,
                        tk=d_ff // 2, kblk=kb)

        projs.append(proj)
        kvs = proj[mp:, W_A:3 * W_A].reshape(db, SROWS, 2, H_A, HEAD_DIM)[:, :n_new]
        k_s.append(kvs[:, :, 0])
        v_s.append(kvs[:, :, 1])
        ret_p.append(st_p)
        ret_s.append(st_s)
        conv_p.append(tail_p)
        conv_s.append(tail_s)

    y_prompt = _final_norm_prompt(h, norm_final, nb, seq, nblk, "norm_final_p")
    hs = _rmsnorm(h[mp:], norm_final, F32, db * SROWS, "norm_final_s")
    y_sample = hs.reshape(db, SROWS, D_MODEL)[:, :n_new]
    k_hm, v_hm = _kv_out(projs, nb, nblk, t_valid, "kv_out")
    k_prompt = jnp.transpose(k_hm, (0, 1, 3, 2, 4))
    v_prompt = jnp.transpose(v_hm, (0, 1, 3, 2, 4))
    return (y_prompt, y_sample, k_prompt, v_prompt, jnp.stack(ret_p), jnp.stack(conv_p),
            jnp.stack(k_s), jnp.stack(v_s), jnp.stack(ret_s), jnp.stack(conv_s))
```

```python
import functools
import math

import jax
import jax.numpy as jnp
from jax import lax
from jax.experimental import pallas as pl
from jax.experimental.pallas import tpu as pltpu

F32 = jnp.float32
BF16 = jnp.bfloat16

D_MODEL = 4096
N_META = 16
HEAD_DIM = 128
HALF = HEAD_DIM // 2
H_A = 12
H_B = 12
W_A = H_A * HEAD_DIM
W_B = H_B * HEAD_DIM
W_C = D_MODEL - W_A - W_B
N_IN = 3 * W_A + 4 * W_B + 2 * W_C
CONV_W = 31
N_BUCKETS = 32
REL_EXACT = 16
REL_MAX_DIST = 128
ROPE_BASE = 10000.0
EPS = 1e-6
NEG_INF = -1e30
PAGE_SIZE = 128

BLK = 128
SROWS = 16
CTX_ROWS = 32
M_INIT = -3.0e38
VMEM_LIMIT = 56 * 1024 * 1024

COL_QA, COL_KA, COL_VA = 0, W_A // 128, 2 * W_A // 128
COL_QB = 3 * W_A // 128
COL_KB = COL_QB + W_B // 128
COL_VB = COL_KB + W_B // 128
COL_GB = COL_VB + W_B // 128
COL_CA = COL_GB + W_B // 128
COL_CB = COL_CA + W_C // 128


def _cparams(sem, vmem=None):
    return pltpu.CompilerParams(dimension_semantics=sem, vmem_limit_bytes=vmem)


def _sigmoid(x):
    return 1.0 / (1.0 + jnp.exp(-x))


def _rmsnorm_body(x_ref, g_ref, o_ref):
    x = x_ref[...]
    ms = jnp.mean(x * x, axis=-1, keepdims=True)
    o_ref[...] = (x * lax.rsqrt(ms + EPS) * g_ref[...]).astype(o_ref.dtype)


def _rmsnorm(x, g, out_dtype, tr, name):
    m, d = x.shape
    return pl.pallas_call(
        _rmsnorm_body,
        grid=(m // tr,),
        in_specs=[pl.BlockSpec((tr, d), lambda i: (i, 0)),
                  pl.BlockSpec((1, d), lambda i: (0, 0))],
        out_specs=pl.BlockSpec((tr, d), lambda i: (i, 0)),
        out_shape=jax.ShapeDtypeStruct((m, d), out_dtype),
        compiler_params=_cparams(("parallel",)),
        name=name,
    )(x, g.reshape(1, d))


def _mm_body(a_ref, w_ref, o_ref):
    o_ref[...] = jnp.dot(a_ref[...], w_ref[...].astype(BF16), preferred_element_type=F32)


def _mm_res_body(a_ref, w_ref, r_ref, o_ref):
    o_ref[...] = r_ref[...] + jnp.dot(a_ref[...], w_ref[...].astype(BF16),
                                      preferred_element_type=F32)


def _matmul(a, w, layer, tm, tn, name, res=None, tk=None, kblk=0):
    m = a.shape[0]
    k = a.shape[1] if tk is None else tk
    n = w.shape[2]
    in_specs = [pl.BlockSpec((tm, k), lambda i, j: (i, kblk)),
                pl.BlockSpec((None, k, tn), lambda i, j: (layer, kblk, j))]
    args = [a, w]
    body = _mm_body
    if res is not None:
        in_specs.append(pl.BlockSpec((tm, tn), lambda i, j: (i, j)))
        args.append(res)
        body = _mm_res_body
    return pl.pallas_call(
        body,
        grid=(m // tm, n // tn),
        in_specs=in_specs,
        out_specs=pl.BlockSpec((tm, tn), lambda i, j: (i, j)),
        out_shape=jax.ShapeDtypeStruct((m, n), F32),
        compiler_params=_cparams(("parallel", "parallel"), VMEM_LIMIT),
        name=name,
    )(*args)


def _ffn_up_body(a_ref, wg_ref, wu_ref, o_ref):
    a = a_ref[...]
    g = jnp.dot(a, wg_ref[...].astype(BF16), preferred_element_type=F32)
    u = jnp.dot(a, wu_ref[...].astype(BF16), preferred_element_type=F32)
    o_ref[...] = (g * _sigmoid(g) * u).astype(o_ref.dtype)


def _ffn_up(a, wg, wu, layer, tm, tn, name):
    m, k = a.shape
    n = wg.shape[2]
    wspec = pl.BlockSpec((None, k, tn), lambda i, j: (layer, 0, j))
    return pl.pallas_call(
        _ffn_up_body,
        grid=(m // tm, n // tn),
        in_specs=[pl.BlockSpec((tm, k), lambda i, j: (i, 0)), wspec, wspec],
        out_specs=pl.BlockSpec((tm, tn), lambda i, j: (i, j)),
        out_shape=jax.ShapeDtypeStruct((m, n), BF16),
        compiler_params=_cparams(("parallel", "parallel"), VMEM_LIMIT),
        name=name,
    )(a, wg, wu)


def _out_proj_body(ya_ref, yb_ref, yc_ref, w_ref, r_ref, o_ref):
    acc = jnp.dot(ya_ref[...], w_ref[0:W_A, :].astype(BF16), preferred_element_type=F32)
    acc += jnp.dot(yb_ref[...], w_ref[W_A:W_A + W_B, :].astype(BF16), preferred_element_type=F32)
    acc += jnp.dot(yc_ref[...], w_ref[W_A + W_B:, :].astype(BF16), preferred_element_type=F32)
    o_ref[...] = r_ref[...] + acc


def _out_proj(ya, yb, yc, w, layer, h, row_tile0, tm, tn, name):
    rows = ya.shape[0]
    n = w.shape[2]
    hspec = pl.BlockSpec((tm, tn), lambda i, j: (row_tile0 + i, j))
    return pl.pallas_call(
        _out_proj_body,
        grid=(rows // tm, n // tn),
        in_specs=[pl.BlockSpec((tm, W_A), lambda i, j: (i, 0)),
                  pl.BlockSpec((tm, W_B), lambda i, j: (i, 0)),
                  pl.BlockSpec((tm, W_C), lambda i, j: (i, 0)),
                  pl.BlockSpec((None, W_A + W_B + W_C, tn), lambda i, j: (layer, 0, j)),
                  hspec],
        out_specs=hspec,
        out_shape=jax.ShapeDtypeStruct(h.shape, F32),
        input_output_aliases={4: 0},
        compiler_params=_cparams(("parallel", "parallel"), VMEM_LIMIT),
        name=name,
    )(ya, yb, yc, w, h)


def _final_norm_body(a_ref, b_ref, g_ref, o_ref):
    x = jnp.concatenate([a_ref[N_META:, :], b_ref[0:N_META, :]], axis=0)
    ms = jnp.mean(x * x, axis=-1, keepdims=True)
    o_ref[...] = x * lax.rsqrt(ms + EPS) * g_ref[...]


def _final_norm_prompt(h, g, nb, seq, nblk, name):
    assert seq % BLK == 0 and N_META % 8 == 0 and N_META <= BLK and seq // BLK < nblk
    d = h.shape[1]
    return pl.pallas_call(
        _final_norm_body,
        grid=(nb, seq // BLK),
        in_specs=[pl.BlockSpec((BLK, d), lambda b, j: (b * nblk + j, 0)),
                  pl.BlockSpec((BLK, d), lambda b, j: (b * nblk + j + 1, 0)),
                  pl.BlockSpec((1, d), lambda b, j: (0, 0))],
        out_specs=pl.BlockSpec((None, BLK, d), lambda b, j: (b, j, 0)),
        out_shape=jax.ShapeDtypeStruct((nb, seq, d), F32),
        compiler_params=_cparams(("parallel", "parallel")),
        name=name,
    )(h, h, g.reshape(1, d))


def _kv_out_body(*refs, depth, t_valid):
    ko_ref, vo_ref = refs[2 * depth:]
    for l in range(depth):
        ko_ref[l] = refs[2 * l][0:t_valid, :]
        vo_ref[l] = refs[2 * l + 1][0:t_valid, :]


def _kv_out(projs, nb, nblk, t_valid, name):
    depth = len(projs)
    tp = nblk * BLK
    in_specs, args = [], []
    for proj in projs:
        in_specs += [pl.BlockSpec((tp, HEAD_DIM), lambda b, h: (b, COL_KA + h)),
                     pl.BlockSpec((tp, HEAD_DIM), lambda b, h: (b, COL_VA + h))]
        args += [proj, proj]
    ospec = pl.BlockSpec((depth, None, None, t_valid, HEAD_DIM), lambda b, h: (0, b, h, 0, 0))
    oshape = jax.ShapeDtypeStruct((depth, nb, H_A, t_valid, HEAD_DIM), F32)
    return pl.pallas_call(
        functools.partial(_kv_out_body, depth=depth, t_valid=t_valid),
        grid=(nb, H_A),
        in_specs=in_specs,
        out_specs=[ospec, ospec],
        out_shape=[oshape, oshape],
        compiler_params=_cparams(("parallel", "parallel")),
        name=name,
    )(*args)


def _lambda(lamp_ref, lam_init):
    lp = lamp_ref[...]
    s1 = jnp.sum(lp[0:1] * lp[1:2], axis=-1, keepdims=True)
    s2 = jnp.sum(lp[2:3] * lp[3:4], axis=-1, keepdims=True)
    return jnp.exp(s1) - jnp.exp(s2) + lam_init


def _split_maps(q):
    lane = lax.broadcasted_iota(jnp.int32, q.shape, 1)
    lo = jnp.where(lane < HALF, q, 0.0)
    hi = jnp.where(lane >= HALF, q, 0.0)
    return jnp.concatenate([lo, hi], axis=0)


def _subln(a, sub_ref, lam_init):
    y = a * lax.rsqrt(jnp.mean(a * a, axis=-1, keepdims=True) + EPS) * sub_ref[...]
    return y * (1.0 - lam_init)


ATT_HG = 4


def _attn_prompt_body(q_ref, k_ref, v_ref, band_ref, lamp_ref, sub_ref, o_ref, kb_ref, vt_ref,
                      s_ref, p_ref, *, lam_init, nblk):
    qi = pl.program_id(2)

    @pl.when(qi == 0)
    def _():
        kb_ref[...] = k_ref[...].astype(BF16)
        for g in range(ATT_HG):
            for j in range(nblk):
                vt = v_ref[j * BLK:(j + 1) * BLK, g * HEAD_DIM:(g + 1) * HEAD_DIM].T
                vt_ref[g, :, j * BLK:(j + 1) * BLK] = vt.astype(BF16)

    def attend(nt):
        lanes = [slice(g * HEAD_DIM, (g + 1) * HEAD_DIM) for g in range(ATT_HG)]
        lam = _lambda(lamp_ref, lam_init)
        mrun = [None] * ATT_HG
        lrun = [None] * ATT_HG
        m = [None] * ATT_HG
        qst = [None] * ATT_HG

        def score_tile(g, j):
            if j == 0:
                qst[g] = _split_maps(q_ref[:, lanes[g]] * (HALF ** -0.5)).T.astype(BF16)
            bt = band_ref[g, jnp.clip(j - qi + 2, 0, 3)]
            s = jnp.dot(kb_ref[j * BLK:(j + 1) * BLK, lanes[g]], qst[g],
                        preferred_element_type=F32)
            s = s + jnp.concatenate([bt, bt], axis=1)
            s_ref[g, j] = s
            mrun[g] = s if mrun[g] is None else jnp.maximum(mrun[g], s)
            if j == nt - 1:
                m[g] = jnp.max(mrun[g], axis=0, keepdims=True)

        def exp_tile(g, j):
            p = jnp.exp(s_ref[g, j] - m[g])
            lrun[g] = p if lrun[g] is None else lrun[g] + p
            p_ref[g, j * BLK:(j + 1) * BLK, :] = p.astype(BF16)
            if j == nt - 1:
                l = jnp.sum(lrun[g], axis=0, keepdims=True)
                ot = jnp.dot(vt_ref[g, :, 0:nt * BLK], p_ref[g, 0:nt * BLK, :],
                             preferred_element_type=F32) / l
                at = ot[:, :BLK] - lam * ot[:, BLK:]
                at = at * lax.rsqrt(jnp.mean(at * at, axis=0, keepdims=True) + EPS)
                o_ref[:, lanes[g]] = (at.T * sub_ref[...] * (1.0 - lam_init)).astype(o_ref.dtype)

        for step in range(ATT_HG + 1):
            for j in range(nt):
                if step < ATT_HG:
                    score_tile(step, j)
                if step >= 1:
                    exp_tile(step - 1, j)

    lo = 0
    for hi in sorted(set(range(2, nblk, 3)) | {nblk}):
        pl.when(jnp.logical_and(qi >= lo, qi < hi))(functools.partial(attend, hi))
        lo = hi


def _attn_prompt(proj, band, lamp, sub, nb, nblk, lam_init, name):
    tp = nblk * BLK
    wg = ATT_HG * HEAD_DIM
    body = functools.partial(_attn_prompt_body, lam_init=lam_init, nblk=nblk)
    return pl.pallas_call(
        body,
        grid=(nb, H_A // ATT_HG, nblk),
        in_specs=[
            pl.BlockSpec((BLK, wg), lambda b, h, qi: (b * nblk + qi, COL_QA // ATT_HG + h)),
            pl.BlockSpec((tp, wg), lambda b, h, qi: (b, COL_KA // ATT_HG + h)),
            pl.BlockSpec((tp, wg), lambda b, h, qi: (b, COL_VA // ATT_HG + h)),
            pl.BlockSpec((ATT_HG, 4, BLK, BLK), lambda b, h, qi: (h, 0, 0, 0)),
            pl.BlockSpec((4, HALF), lambda b, h, qi: (0, 0)),
            pl.BlockSpec((1, HEAD_DIM), lambda b, h, qi: (0, 0)),
        ],
        out_specs=pl.BlockSpec((BLK, wg), lambda b, h, qi: (b * nblk + qi, h)),
        out_shape=jax.ShapeDtypeStruct((nb * tp, W_A), BF16),
        scratch_shapes=[pltpu.VMEM((tp, wg), BF16), pltpu.VMEM((ATT_HG, HEAD_DIM, tp), BF16),
                        pltpu.VMEM((ATT_HG, nblk, BLK, 2 * BLK), F32),
                        pltpu.VMEM((ATT_HG, tp, 2 * BLK), BF16)],
        compiler_params=_cparams(("parallel", "parallel", "arbitrary"), VMEM_LIMIT),
        name=name,
    )(proj, proj, proj, band, lamp, sub)


PPS = 4


def _attn_sample_body(pt_ref, q_ref, kn_ref, vn_ref, *rest, lam_init, n_pages, n_new):
    kp_refs, vp_refs = rest[:PPS], rest[PPS:2 * PPS]
    bias_ref, lamp_ref, sub_ref, o_ref, qs_ref, m_ref, l_ref, acc_ref = rest[2 * PPS:]
    p = pl.program_id(1)
    n_steps = n_pages // PPS

    @pl.when(p == 0)
    def _():
        for h in range(H_A):
            q = q_ref[0:n_new, h * HEAD_DIM:(h + 1) * HEAD_DIM] * (HALF ** -0.5)
            qs = _split_maps(q)
            pad = jnp.zeros((SROWS - 2 * n_new, HEAD_DIM), F32)
            qs_ref[h] = jnp.concatenate([qs, pad], axis=0).astype(BF16)
        m_ref[...] = jnp.full(m_ref.shape, M_INIT, F32)
        l_ref[...] = jnp.zeros(l_ref.shape, F32)
        acc_ref[...] = jnp.zeros(acc_ref.shape, F32)

    def update(get_k, get_v, bias, n_rep):
        s = jnp.stack([lax.dot_general(qs_ref[h], get_k(h).astype(BF16), (((1,), (1,)), ((), ())),
                                       preferred_element_type=F32) for h in range(H_A)])
        s = s + bias
        m_prev = m_ref[...]
        m_new = jnp.maximum(m_prev, jnp.max(s, axis=-1, keepdims=True))
        alpha = jnp.exp(m_prev - m_new)
        pm = jnp.exp(s - jnp.concatenate([m_new] * n_rep, axis=-1))
        l_ref[...] = alpha * l_ref[...] + jnp.sum(pm, axis=-1, keepdims=True)
        m_ref[...] = m_new
        pb = pm.astype(BF16)
        for h in range(H_A):
            acc_ref[h] = alpha[h] * acc_ref[h] + jnp.dot(pb[h], get_v(h).astype(BF16),
                                                         preferred_element_type=F32)

    @pl.when(p < n_steps)
    def _():
        def pages(refs, h):
            return jnp.concatenate([r[h] for r in refs], axis=0)

        last_idx = jnp.where(p == n_steps - 1, 1, 0)
        bias = jnp.concatenate([bias_ref[0]] * (PPS - 1) + [bias_ref[last_idx]], axis=-1)
        update(functools.partial(pages, kp_refs), functools.partial(pages, vp_refs), bias, PPS)

    @pl.when(p == n_steps)
    def _():
        zpad = jnp.zeros((PAGE_SIZE - SROWS, HEAD_DIM), F32)

        def new_rows(ref, h):
            return jnp.concatenate([ref[:, h * HEAD_DIM:(h + 1) * HEAD_DIM], zpad], axis=0)

        update(functools.partial(new_rows, kn_ref), functools.partial(new_rows, vn_ref),
               bias_ref[2], 1)
        lam = _lambda(lamp_ref, lam_init)
        for h in range(H_A):
            o = acc_ref[h] / l_ref[h]
            a = o[0:n_new] - lam * o[n_new:2 * n_new]
            y = _subln(a, sub_ref, lam_init)
            pad = jnp.zeros((SROWS - n_new, HEAD_DIM), F32)
            o_ref[:, h * HEAD_DIM:(h + 1) * HEAD_DIM] = (
                jnp.concatenate([y, pad], axis=0).astype(o_ref.dtype))


def _attn_sample(page_table, proj, cache_k, cache_v, bias, lamp, sub, layer, row_blk0, n_new,
                 lam_init, name):
    db, n_pages = page_table.shape
    assert n_pages % PPS == 0
    body = functools.partial(_attn_sample_body, lam_init=lam_init, n_pages=n_pages, n_new=n_new)

    def page_spec(t):
        return pl.BlockSpec(
            (None, None, H_A, PAGE_SIZE, HEAD_DIM),
            lambda b, p, pt: (layer, pt[b, jnp.minimum(p * PPS + t, n_pages - 1)], 0, 0, 0))

    page_specs = [page_spec(t) for t in range(PPS)]
    grid_spec = pltpu.PrefetchScalarGridSpec(
        num_scalar_prefetch=1,
        grid=(db, n_pages // PPS + 1),
        in_specs=[
            pl.BlockSpec((SROWS, W_A), lambda b, p, pt: (row_blk0 + b, 0)),
            pl.BlockSpec((SROWS, W_A), lambda b, p, pt: (row_blk0 + b, 1)),
            pl.BlockSpec((SROWS, W_A), lambda b, p, pt: (row_blk0 + b, 2)),
            *page_specs, *page_specs,
            pl.BlockSpec((3, H_A, SROWS, PAGE_SIZE), lambda b, p, pt: (0, 0, 0, 0)),
            pl.BlockSpec((4, HALF), lambda b, p, pt: (0, 0)),
            pl.BlockSpec((1, HEAD_DIM), lambda b, p, pt: (0, 0)),
        ],
        out_specs=pl.BlockSpec((SROWS, W_A), lambda b, p, pt: (b, 0)),
        scratch_shapes=[pltpu.VMEM((H_A, SROWS, HEAD_DIM), BF16),
                        pltpu.VMEM((H_A, SROWS, HEAD_DIM), F32),
                        pltpu.VMEM((H_A, SROWS, HEAD_DIM), F32),
                        pltpu.VMEM((H_A, SROWS, HEAD_DIM), F32)],
    )
    return pl.pallas_call(
        body,
        grid_spec=grid_spec,
        out_shape=jax.ShapeDtypeStruct((db * SROWS, W_A), BF16),
        compiler_params=_cparams(("parallel", "arbitrary"), VMEM_LIMIT),
        name=name,
    )(page_table, proj, proj, proj, *([cache_k] * PPS), *([cache_v] * PPS), bias, lamp, sub)


RET_HG = 12


def _retention_body(q_ref, k_ref, v_ref, g_ref, c2_ref, s2_ref, lg_ref, gn_ref, s0_ref,
                    y_ref, sout_ref, st_ref, *, rows, n_chunks, last_valid):
    c = pl.program_id(2)

    @pl.when(c == 0)
    def _():
        st_ref[...] = s0_ref[...]

    lc = jnp.where(c == n_chunks - 1, float(last_valid), float(BLK))
    ii = lax.broadcasted_iota(jnp.int32, (BLK, BLK), 0).astype(F32)
    jj = lax.broadcasted_iota(jnp.int32, (BLK, BLK), 1).astype(F32)
    rel = ii - jj

    def rows128(x):
        if rows == BLK:
            return x
        return jnp.concatenate([x, jnp.zeros((BLK - rows, x.shape[1]), x.dtype)], axis=0)

    c2 = rows128(c2_ref[...])
    s2 = rows128(s2_ref[...])

    def rot(x):
        return x * c2 + pltpu.roll(x, HALF, 1) * s2

    heads = []
    for j in range(RET_HG):
        sl = slice(j * HEAD_DIM, (j + 1) * HEAD_DIM)
        lgv = lg_ref[:, sl]
        v = rows128(v_ref[:, sl]).astype(BF16)
        qr = rot(rows128(q_ref[:, sl])).astype(BF16)
        kr = rot(rows128(k_ref[:, sl])) * (HEAD_DIM ** -0.5)
        st = st_ref[j]
        sc = lax.dot_general(qr, kr.astype(BF16), (((1,), (1,)), ((), ())),
                             preferred_element_type=F32)
        cross = jnp.dot(qr, st.astype(BF16), preferred_element_type=F32)
        kdec = kr * jnp.where(ii < lc, jnp.exp((lc - 1.0 - ii) * lgv), 0.0)
        st_ref[j] = jnp.exp(lc * lgv) * st + jnp.dot(kdec.T.astype(BF16), v,
                                                     preferred_element_type=F32)
        heads.append((sl, lgv, v, sc, cross))
    for sl, lgv, v, sc, cross in heads:
        decay = jnp.where(rel >= 0, jnp.exp(lgv * jnp.maximum(rel, 0.0)), 0.0)
        o = jnp.dot((sc * decay).astype(BF16), v, preferred_element_type=F32)
        o = o + cross * jnp.exp((ii + 1.0) * lgv)
        y = o[:rows]
        y = y * lax.rsqrt(jnp.mean(y * y, axis=-1, keepdims=True) + EPS) * gn_ref[...]
        g = g_ref[:, sl]
        y_ref[:, sl] = (y * (g * _sigmoid(g))).astype(y_ref.dtype)

    @pl.when(c == n_chunks - 1)
    def _():
        sout_ref[...] = st_ref[...]


def _retention(proj, c2, s2, lgrow, gn, s0, nb, n_chunks, rows, row_blk0, last_valid, name):
    body = functools.partial(_retention_body, rows=rows, n_chunks=n_chunks, last_valid=last_valid)
    wb = RET_HG * HEAD_DIM
    cb = wb // 128

    def in_spec(col0):
        return pl.BlockSpec((rows, wb),
                            lambda b, hg, c: (row_blk0 + b * n_chunks + c, col0 // cb + hg))

    st_spec = pl.BlockSpec((None, RET_HG, HEAD_DIM, HEAD_DIM), lambda b, hg, c: (b, hg, 0, 0))
    return pl.pallas_call(
        body,
        grid=(nb, H_B // RET_HG, n_chunks),
        in_specs=[in_spec(COL_QB), in_spec(COL_KB), in_spec(COL_VB), in_spec(COL_GB),
                  pl.BlockSpec((rows, HEAD_DIM), lambda b, hg, c: (c, 0)),
                  pl.BlockSpec((rows, HEAD_DIM), lambda b, hg, c: (c, 0)),
                  pl.BlockSpec((1, wb), lambda b, hg, c: (0, hg)),
                  pl.BlockSpec((1, HEAD_DIM), lambda b, hg, c: (0, 0)),
                  st_spec],
        out_specs=[pl.BlockSpec((rows, wb), lambda b, hg, c: (b * n_chunks + c, hg)), st_spec],
        out_shape=[jax.ShapeDtypeStruct((nb * n_chunks * rows, W_B), BF16),
                   jax.ShapeDtypeStruct((nb, H_B, HEAD_DIM, HEAD_DIM), F32)],
        scratch_shapes=[pltpu.VMEM((RET_HG, HEAD_DIM, HEAD_DIM), F32)],
        compiler_params=_cparams(("parallel", "parallel", "arbitrary"), VMEM_LIMIT),
        name=name,
    )(proj, proj, proj, proj, c2, s2, lgrow, gn, s0)


CONV_LANES = 256


def _conv_body(ca0_ref, ca1_ref, cb0_ref, cb1_ref, ctx_ref, w_ref, b_ref, lg_ref, lb_ref,
               y_ref, tail_ref, uc_ref, us_ref, cv_ref, *, rows, n_chunks, last_valid):
    c = pl.program_id(1)

    @pl.when(c == 0)
    def _():
        uc_ref[0:CTX_ROWS, :] = ctx_ref[...]

    ca = jnp.concatenate([ca0_ref[...], ca1_ref[...]], axis=1)
    cb = jnp.concatenate([cb0_ref[...], cb1_ref[...]], axis=1)
    uc_ref[CTX_ROWS:CTX_ROWS + rows, :] = ca * _sigmoid(cb)
    span = CTX_ROWS + rows - 8
    for r in range(1, 8):
        us_ref[r] = uc_ref[pl.ds(r, span), :]
    for cc in range(0, W_C, CONV_LANES):
        cs = slice(cc, cc + CONV_LANES)
        acc = jnp.broadcast_to(b_ref[:, cs], (rows, CONV_LANES))
        for w in range(CONV_W):
            off = CTX_ROWS - (CONV_W - 1) + w
            r, a = off % 8, off - off % 8
            src = uc_ref[a:a + rows, cs] if r == 0 else us_ref[r, a:a + rows, cs]
            acc = acc + src * w_ref[w:w + 1, cs]
        cv_ref[:, cs] = acc
    acc = cv_ref[...]
    mu = jnp.mean(acc, axis=-1, keepdims=True)
    xc = acc - mu
    var = jnp.mean(xc * xc, axis=-1, keepdims=True)
    y = xc * lax.rsqrt(var + EPS) * lg_ref[...] + lb_ref[...]
    y_ref[...] = (y * _sigmoid(y)).astype(y_ref.dtype)

    @pl.when(c == n_chunks - 1)
    def _():
        tail_ref[...] = uc_ref[pl.ds(CTX_ROWS + last_valid - (CONV_W - 1), CONV_W - 1), :]

    nxt = uc_ref[rows:rows + CTX_ROWS, :]
    uc_ref[0:CTX_ROWS, :] = nxt


def _conv(proj, ctx0, conv_w, conv_b, ln_g, ln_b, layer, nb, n_chunks, rows, row_blk0,
          last_valid, name):
    body = functools.partial(_conv_body, rows=rows, n_chunks=n_chunks, last_valid=last_valid)
    half = W_C // 2
    hb = half // 128

    def in_spec(col):
        return pl.BlockSpec((rows, half), lambda b, c: (row_blk0 + b * n_chunks + c, col))

    def par_spec():
        return pl.BlockSpec((None, 1, W_C), lambda b, c: (layer, 0, 0))

    return pl.pallas_call(
        body,
        grid=(nb, n_chunks),
        in_specs=[in_spec(COL_CA // hb), in_spec(COL_CA // hb + 1),
                  in_spec(COL_CB // hb), in_spec(COL_CB // hb + 1),
                  pl.BlockSpec((None, CTX_ROWS, W_C), lambda b, c: (b, 0, 0)),
                  pl.BlockSpec((None, CONV_W, W_C), lambda b, c: (layer, 0, 0)),
                  par_spec(), par_spec(), par_spec()],
        out_specs=[pl.BlockSpec((rows, W_C), lambda b, c: (b * n_chunks + c, 0)),
                   pl.BlockSpec((None, CONV_W - 1, W_C), lambda b, c: (b, 0, 0))],
        out_shape=[jax.ShapeDtypeStruct((nb * n_chunks * rows, W_C), BF16),
                   jax.ShapeDtypeStruct((nb, CONV_W - 1, W_C), F32)],
        scratch_shapes=[pltpu.VMEM((CTX_ROWS + rows, W_C), F32),
                        pltpu.VMEM((8, CTX_ROWS + rows - 8, W_C), F32),
                        pltpu.VMEM((rows, W_C), F32)],
        compiler_params=_cparams(("parallel", "arbitrary"), VMEM_LIMIT),
        name=name,
    )(proj, proj, proj, proj, ctx0, conv_w, conv_b, ln_g, ln_b)


def _t5_bias(rel_bias, dist):
    n = jnp.maximum(dist, 0)
    nf = jnp.maximum(n, 1).astype(F32)
    large = REL_EXACT + (jnp.log(nf / REL_EXACT) / math.log(REL_MAX_DIST / REL_EXACT)
                         * (N_BUCKETS - REL_EXACT)).astype(jnp.int32)
    bucket = jnp.where(n < REL_EXACT, n, jnp.minimum(large, N_BUCKETS - 1))
    onehot = (bucket[..., None] == jnp.arange(N_BUCKETS, dtype=jnp.int32)).astype(F32)
    bias = jnp.einsum("...k,kh->h...", onehot, rel_bias.astype(F32),
                      precision=lax.Precision.HIGHEST)
    return jnp.where(dist[None] >= 0, bias, NEG_INF)


def _rotary_tables(pos):
    inv = ROPE_BASE ** (-jnp.arange(HALF, dtype=F32) / HALF)
    ang = pos.astype(F32)[:, None] * inv[None, :]
    cos, sin = jnp.cos(ang), jnp.sin(ang)
    return jnp.concatenate([cos, cos], axis=1), jnp.concatenate([-sin, sin], axis=1)


def kernel(x_prompt, x_sample, cache_k, cache_v, state_ret, state_conv, page_table, meta, rel_bias,
           norm_mix, w_in, lam_q1, lam_k1, lam_q2, lam_k2, subln_a, gn_b, conv_w, conv_b, conv_ln_g,
           conv_ln_b, w_out, norm_ffn, w_gate, w_up, w_down, norm_final):
    nb, seq = x_prompt.shape[:2]
    db, n_new = x_sample.shape[:2]
    depth = w_in.shape[0]
    n_pages = page_table.shape[1]
    past = n_pages * PAGE_SIZE
    t_valid = seq + N_META
    nblk = -(-t_valid // BLK)
    tp = nblk * BLK
    last_valid = t_valid - (nblk - 1) * BLK
    mp = nb * tp
    m_all = mp + db * SROWS
    s_blk0 = mp // SROWS
    d_ff = w_gate.shape[2]

    tm = m_all // 6
    tr = m_all // 24

    hp = jnp.concatenate([jnp.broadcast_to(meta[None].astype(F32), (nb, N_META, D_MODEL)),
                          x_prompt, jnp.zeros((nb, tp - t_valid, D_MODEL), F32)], axis=1)
    hs = jnp.pad(x_sample, ((0, 0), (0, SROWS - n_new), (0, 0)))
    h = jnp.concatenate([hp.reshape(mp, D_MODEL), hs.reshape(db * SROWS, D_MODEL)], axis=0)

    assert BLK >= REL_MAX_DIST and PAGE_SIZE >= REL_MAX_DIST and nblk >= 2
    far = _t5_bias(rel_bias, jnp.full((1, 1), REL_MAX_DIST, jnp.int32))
    d0 = (nblk - 1) * BLK
    ii = jnp.arange(BLK, dtype=jnp.int32)[None, :]
    cn = jnp.arange(d0 - BLK, d0 + BLK, dtype=jnp.int32)[:, None]
    near = _t5_bias(rel_bias, d0 + ii - cn)
    band = jnp.stack([jnp.broadcast_to(far, (H_A, BLK, BLK)), near[:, :BLK], near[:, BLK:],
                      jnp.full((H_A, BLK, BLK), NEG_INF, F32)], axis=1)
    r_new = jnp.arange(SROWS, dtype=jnp.int32) % n_new
    jk = jnp.arange(PAGE_SIZE, dtype=jnp.int32)[None, :]
    dist_s = jnp.stack([jnp.full((SROWS, PAGE_SIZE), REL_MAX_DIST, jnp.int32),
                        PAGE_SIZE + r_new[:, None] - jk,
                        jnp.where(jk < n_new, r_new[:, None] - jk, -1)])
    bias_s = jnp.transpose(_t5_bias(rel_bias, dist_s), (1, 0, 2, 3))
    cache_kh = jnp.transpose(cache_k, (0, 1, 3, 2, 4))
    cache_vh = jnp.transpose(cache_v, (0, 1, 3, 2, 4))
    c2_p, s2_p = _rotary_tables(jnp.arange(tp, dtype=jnp.int32))
    c2_s, s2_s = _rotary_tables(past + jnp.arange(SROWS, dtype=jnp.int32))
    log_gamma = jnp.log1p(-jnp.exp2(-5.0 - jnp.arange(H_B, dtype=F32)))
    lgrow = jnp.repeat(log_gamma, HEAD_DIM)[None, :]

    zero_state = jnp.zeros((nb, H_B, HEAD_DIM, HEAD_DIM), F32)
    zero_ctx = jnp.zeros((nb, CTX_ROWS, W_C), F32)
    conv_b3 = conv_b[:, None, :]
    ln_g3 = conv_ln_g[:, None, :]
    ln_b3 = conv_ln_b[:, None, :]

    projs, ret_p, conv_p, k_s, v_s, ret_s, conv_s = [], [], [], [], [], [], []
    for l in range(depth):
        lam_init = 0.8 - 0.6 * math.exp(-0.3 * l)
        lamp = jnp.stack([lam_q1[l], lam_k1[l], lam_q2[l], lam_k2[l]]).astype(F32)
        sub = subln_a[l][None, :]
        gn = gn_b[l][None, :]

        n1 = _rmsnorm(h, norm_mix[l], BF16, tr, f"norm_mix{l}")
        proj = _matmul(n1, w_in, l, tm, 512, f"in_proj{l}")

        ya_p = _attn_prompt(proj, band, lamp, sub, nb, nblk, lam_init, f"attn_prompt{l}")
        ya_s = _attn_sample(page_table, proj, cache_kh, cache_vh, bias_s, lamp, sub, l, s_blk0,
                            n_new, lam_init, f"attn_sample{l}")
        yb_p, st_p = _retention(proj, c2_p, s2_p, lgrow, gn, zero_state, nb, nblk, BLK, 0,
                                last_valid, f"ret_prompt{l}")
        yb_s, st_s = _retention(proj, c2_s, s2_s, lgrow, gn, state_ret[l], db, 1, SROWS,
                                s_blk0, n_new, f"ret_sample{l}")
        yc_p, tail_p = _conv(proj, zero_ctx, conv_w, conv_b3, ln_g3, ln_b3, l, nb, nblk, BLK, 0,
                             last_valid, f"conv_prompt{l}")
        ctx_s = jnp.pad(state_conv[l], ((0, 0), (CTX_ROWS - (CONV_W - 1), 0), (0, 0)))
        yc_s, tail_s = _conv(proj, ctx_s, conv_w, conv_b3, ln_g3, ln_b3, l, db, 1, SROWS, s_blk0,
                             n_new, f"conv_sample{l}")

        h = _out_proj(ya_p, yb_p, yc_p, w_out, l, h, 0, mp // 8, 512, f"out_proj_p{l}")
        h = _out_proj(ya_s, yb_s, yc_s, w_out, l, h, mp // (db * SROWS), db * SROWS, 256,
                      f"out_proj_s{l}")
        n2 = _rmsnorm(h, norm_ffn[l], BF16, tr, f"norm_ffn{l}")
        hid = _ffn_up(n2, w_gate, w_up, l, tm, 256, f"ffn_up{l}")
        for kb in range(2):
            h = _matmul(hid, w_down, l, tm, 256, f"ffn_down{l}_{kb}", res=h,
                        tk=d_ff // 2, kblk=kb)

        projs.append(proj)
        kvs = proj[mp:, W_A:3 * W_A].reshape(db, SROWS, 2, H_A, HEAD_DIM)[:, :n_new]
        k_s.append(kvs[:, :, 0])
        v_s.append(kvs[:, :, 1])
        ret_p.append(st_p)
        ret_s.append(st_s)
        conv_p.append(tail_p)
        conv_s.append(tail_s)

    y_prompt = _final_norm_prompt(h, norm_final, nb, seq, nblk, "norm_final_p")
    hs = _rmsnorm(h[mp:], norm_final, F32, db * SROWS, "norm_final_s")
    y_sample = hs.reshape(db, SROWS, D_MODEL)[:, :n_new]
    k_hm, v_hm = _kv_out(projs, nb, nblk, t_valid, "kv_out")
    k_prompt = jnp.transpose(k_hm, (0, 1, 3, 2, 4))
    v_prompt = jnp.transpose(v_hm, (0, 1, 3, 2, 4))
    return (y_prompt, y_sample, k_prompt, v_prompt, jnp.stack(ret_p), jnp.stack(conv_p),
            jnp.stack(k_s), jnp.stack(v_s), jnp.stack(ret_s), jnp.stack(conv_s))
```

```python
import functools
import math

import jax
import jax.numpy as jnp
from jax import lax
from jax.experimental import pallas as pl
from jax.experimental.pallas import tpu as pltpu

F32 = jnp.float32
BF16 = jnp.bfloat16

D_MODEL = 4096
N_META = 16
HEAD_DIM = 128
HALF = HEAD_DIM // 2
H_A = 12
H_B = 12
W_A = H_A * HEAD_DIM
W_B = H_B * HEAD_DIM
W_C = D_MODEL - W_A - W_B
N_IN = 3 * W_A + 4 * W_B + 2 * W_C
CONV_W = 31
N_BUCKETS = 32
REL_EXACT = 16
REL_MAX_DIST = 128
ROPE_BASE = 10000.0
EPS = 1e-6
NEG_INF = -1e30
PAGE_SIZE = 128

BLK = 128
SROWS = 16
CTX_ROWS = 32
M_INIT = -3.0e38
VMEM_LIMIT = 56 * 1024 * 1024

COL_QA, COL_KA, COL_VA = 0, W_A // 128, 2 * W_A // 128
COL_QB = 3 * W_A // 128
COL_KB = COL_QB + W_B // 128
COL_VB = COL_KB + W_B // 128
COL_GB = COL_VB + W_B // 128
COL_CA = COL_GB + W_B // 128
COL_CB = COL_CA + W_C // 128


def _cparams(sem, vmem=None):
    return pltpu.CompilerParams(dimension_semantics=sem, vmem_limit_bytes=vmem)


def _sigmoid(x):
    return 1.0 / (1.0 + jnp.exp(-x))


def _rmsnorm_body(x_ref, g_ref, o_ref):
    x = x_ref[...]
    ms = jnp.mean(x * x, axis=-1, keepdims=True)
    o_ref[...] = (x * lax.rsqrt(ms + EPS) * g_ref[...]).astype(o_ref.dtype)


def _rmsnorm(x, g, out_dtype, tr, name, rows=None):
    m, d = x.shape
    m = m if rows is None else rows
    return pl.pallas_call(
        _rmsnorm_body,
        grid=(m // tr,),
        in_specs=[pl.BlockSpec((tr, d), lambda i: (i, 0)),
                  pl.BlockSpec((1, d), lambda i: (0, 0))],
        out_specs=pl.BlockSpec((tr, d), lambda i: (i, 0)),
        out_shape=jax.ShapeDtypeStruct((m, d), out_dtype),
        compiler_params=_cparams(("parallel",)),
        name=name,
    )(x, g.reshape(1, d))


def _mm_body(a_ref, w_ref, o_ref):
    o_ref[...] = jnp.dot(a_ref[...], w_ref[...].astype(BF16), preferred_element_type=F32)


def _mm_res_body(a_ref, w_ref, r_ref, o_ref):
    o_ref[...] = r_ref[...] + jnp.dot(a_ref[...], w_ref[...].astype(BF16),
                                      preferred_element_type=F32)


def _matmul(a, w, layer, tm, tn, name, res=None, tk=None, kblk=0):
    m = a.shape[0]
    k = a.shape[1] if tk is None else tk
    n = w.shape[2]
    in_specs = [pl.BlockSpec((tm, k), lambda i, j: (i, kblk)),
                pl.BlockSpec((None, k, tn), lambda i, j: (layer, kblk, j))]
    args = [a, w]
    body = _mm_body
    if res is not None:
        in_specs.append(pl.BlockSpec((tm, tn), lambda i, j: (i, j)))
        args.append(res)
        body = _mm_res_body
    return pl.pallas_call(
        body,
        grid=(m // tm, n // tn),
        in_specs=in_specs,
        out_specs=pl.BlockSpec((tm, tn), lambda i, j: (i, j)),
        out_shape=jax.ShapeDtypeStruct((m, n), F32),
        compiler_params=_cparams(("parallel", "parallel"), VMEM_LIMIT),
        name=name,
    )(*args)


def _ffn_up_body(a_ref, wg_ref, wu_ref, o_ref):
    a = a_ref[...]
    g = jnp.dot(a, wg_ref[...].astype(BF16), preferred_element_type=F32)
    u = jnp.dot(a, wu_ref[...].astype(BF16), preferred_element_type=F32)
    o_ref[...] = (g * _sigmoid(g) * u).astype(o_ref.dtype)


def _ffn_up(a, wg, wu, layer, tm, tn, name):
    m, k = a.shape
    n = wg.shape[2]
    wspec = pl.BlockSpec((None, k, tn), lambda i, j: (layer, 0, j))
    return pl.pallas_call(
        _ffn_up_body,
        grid=(m // tm, n // tn),
        in_specs=[pl.BlockSpec((tm, k), lambda i, j: (i, 0)), wspec, wspec],
        out_specs=pl.BlockSpec((tm, tn), lambda i, j: (i, j)),
        out_shape=jax.ShapeDtypeStruct((m, n), BF16),
        compiler_params=_cparams(("parallel", "parallel"), VMEM_LIMIT),
        name=name,
    )(a, wg, wu)


def _out_proj_body(ya_ref, yb_ref, yc_ref, w_ref, r_ref, o_ref):
    acc = jnp.dot(ya_ref[...], w_ref[0:W_A, :].astype(BF16), preferred_element_type=F32)
    acc += jnp.dot(yb_ref[...], w_ref[W_A:W_A + W_B, :].astype(BF16), preferred_element_type=F32)
    acc += jnp.dot(yc_ref[...], w_ref[W_A + W_B:, :].astype(BF16), preferred_element_type=F32)
    o_ref[...] = r_ref[...] + acc


def _out_proj(ya, yb, yc, w, layer, h, row_tile0, tm, tn, name):
    rows = ya.shape[0]
    n = w.shape[2]
    hspec = pl.BlockSpec((tm, tn), lambda i, j: (row_tile0 + i, j))
    return pl.pallas_call(
        _out_proj_body,
        grid=(rows // tm, n // tn),
        in_specs=[pl.BlockSpec((tm, W_A), lambda i, j: (i, 0)),
                  pl.BlockSpec((tm, W_B), lambda i, j: (i, 0)),
                  pl.BlockSpec((tm, W_C), lambda i, j: (i, 0)),
                  pl.BlockSpec((None, W_A + W_B + W_C, tn), lambda i, j: (layer, 0, j)),
                  hspec],
        out_specs=hspec,
        out_shape=jax.ShapeDtypeStruct(h.shape, F32),
        input_output_aliases={4: 0},
        compiler_params=_cparams(("parallel", "parallel"), VMEM_LIMIT),
        name=name,
    )(ya, yb, yc, w, h)


def _kv_out_body(*refs, depth, seq):
    ko_ref, vo_ref = refs[4 * depth:]
    for l in range(depth):
        km_ref, kr_ref, vm_ref, vr_ref = refs[4 * l:4 * l + 4]
        ko_ref[l, 0:N_META] = km_ref[...]
        ko_ref[l, N_META:N_META + seq] = kr_ref[...]
        vo_ref[l, 0:N_META] = vm_ref[...]
        vo_ref[l, N_META:N_META + seq] = vr_ref[...]


def _kv_out(projs, nb, seq, meta_blk, name):
    depth = len(projs)
    t_valid = N_META + seq
    in_specs, args = [], []
    for proj in projs:
        for col in (COL_KA, COL_VA):
            in_specs += [pl.BlockSpec((N_META, HEAD_DIM), lambda b, h, col=col: (meta_blk, col + h)),
                         pl.BlockSpec((seq, HEAD_DIM), lambda b, h, col=col: (b, col + h))]
            args += [proj, proj]
    ospec = pl.BlockSpec((depth, None, None, t_valid, HEAD_DIM), lambda b, h: (0, b, h, 0, 0))
    oshape = jax.ShapeDtypeStruct((depth, nb, H_A, t_valid, HEAD_DIM), F32)
    return pl.pallas_call(
        functools.partial(_kv_out_body, depth=depth, seq=seq),
        grid=(nb, H_A),
        in_specs=in_specs,
        out_specs=[ospec, ospec],
        out_shape=[oshape, oshape],
        compiler_params=_cparams(("parallel", "parallel")),
        name=name,
    )(*args)


def _lambda(lamp_ref, lam_init):
    lp = lamp_ref[...]
    s1 = jnp.sum(lp[0:1] * lp[1:2], axis=-1, keepdims=True)
    s2 = jnp.sum(lp[2:3] * lp[3:4], axis=-1, keepdims=True)
    return jnp.exp(s1) - jnp.exp(s2) + lam_init


def _split_maps(q):
    lane = lax.broadcasted_iota(jnp.int32, q.shape, 1)
    lo = jnp.where(lane < HALF, q, 0.0)
    hi = jnp.where(lane >= HALF, q, 0.0)
    return jnp.concatenate([lo, hi], axis=0)


def _subln(a, sub_ref, lam_init):
    y = a * lax.rsqrt(jnp.mean(a * a, axis=-1, keepdims=True) + EPS) * sub_ref[...]
    return y * (1.0 - lam_init)


ATT_HG = 4


def _attn_prompt_body(q_ref, k_ref, v_ref, km_ref, vm_ref, band_ref, lamp_ref, sub_ref, o_ref,
                      kb_ref, vt_ref, s_ref, p_ref, *, lam_init, nblk):
    qi = pl.program_id(2)
    lanes = [slice(g * HEAD_DIM, (g + 1) * HEAD_DIM) for g in range(ATT_HG)]

    @pl.when(qi == 0)
    def _():
        zpad = jnp.zeros((BLK - N_META, ATT_HG * HEAD_DIM), F32)
        kb_ref[0:BLK, :] = jnp.concatenate([km_ref[...], zpad], axis=0).astype(BF16)
        kb_ref[BLK:, :] = k_ref[...].astype(BF16)
        vm = jnp.concatenate([vm_ref[...], zpad], axis=0)
        for g in range(ATT_HG):
            vt_ref[g, :, 0:BLK] = vm[:, lanes[g]].T.astype(BF16)
            for j in range(1, nblk):
                vt = v_ref[(j - 1) * BLK:j * BLK, lanes[g]].T
                vt_ref[g, :, j * BLK:(j + 1) * BLK] = vt.astype(BF16)

    def attend(nt):
        lam = _lambda(lamp_ref, lam_init)
        mrun = [None] * ATT_HG
        lrun = [None] * ATT_HG
        m = [None] * ATT_HG
        qst = [None] * ATT_HG

        def score_tile(g, j):
            if j == 0:
                qst[g] = _split_maps(q_ref[:, lanes[g]] * (HALF ** -0.5)).T.astype(BF16)
            if j == 0:
                bt = band_ref[g, jnp.where(qi == 0, 4, 5)]
            else:
                bt = band_ref[g, jnp.clip(j - qi + 1, 0, 3)]
            s = jnp.dot(kb_ref[j * BLK:(j + 1) * BLK, lanes[g]], qst[g],
                        preferred_element_type=F32)
            s = s + jnp.concatenate([bt, bt], axis=1)
            s_ref[g, j] = s
            mrun[g] = s if mrun[g] is None else jnp.maximum(mrun[g], s)
            if j == nt - 1:
                m[g] = jnp.max(mrun[g], axis=0, keepdims=True)

        def exp_tile(g, j):
            p = jnp.exp(s_ref[g, j] - m[g])
            lrun[g] = p if lrun[g] is None else lrun[g] + p
            p_ref[g, j * BLK:(j + 1) * BLK, :] = p.astype(BF16)
            if j == nt - 1:
                l = jnp.sum(lrun[g], axis=0, keepdims=True)
                ot = jnp.dot(vt_ref[g, :, 0:nt * BLK], p_ref[g, 0:nt * BLK, :],
                             preferred_element_type=F32) / l
                at = ot[:, :BLK] - lam * ot[:, BLK:]
                at = at * lax.rsqrt(jnp.mean(at * at, axis=0, keepdims=True) + EPS)
                o_ref[:, lanes[g]] = (at.T * sub_ref[...] * (1.0 - lam_init)).astype(o_ref.dtype)

        for step in range(ATT_HG + 1):
            for j in range(nt):
                if step < ATT_HG:
                    score_tile(step, j)
                if step >= 1:
                    exp_tile(step - 1, j)

    lo = 0
    for hi in sorted(set(range(2, nblk, 3)) | {nblk}):
        pl.when(jnp.logical_and(qi + 2 > lo, qi + 2 <= hi))(functools.partial(attend, hi))
        lo = hi


def _attn_prompt(proj, band, lamp, sub, nb, seq, meta_blk, lam_init, name):
    nq = seq // BLK
    nblk = nq + 1
    tp = nblk * BLK
    wg = ATT_HG * HEAD_DIM
    body = functools.partial(_attn_prompt_body, lam_init=lam_init, nblk=nblk)
    return pl.pallas_call(
        body,
        grid=(nb, H_A // ATT_HG, nq),
        in_specs=[
            pl.BlockSpec((BLK, wg), lambda b, h, qi: (b * nq + qi, COL_QA // ATT_HG + h)),
            pl.BlockSpec((seq, wg), lambda b, h, qi: (b, COL_KA // ATT_HG + h)),
            pl.BlockSpec((seq, wg), lambda b, h, qi: (b, COL_VA // ATT_HG + h)),
            pl.BlockSpec((N_META, wg), lambda b, h, qi: (meta_blk, COL_KA // ATT_HG + h)),
            pl.BlockSpec((N_META, wg), lambda b, h, qi: (meta_blk, COL_VA // ATT_HG + h)),
            pl.BlockSpec((ATT_HG, 6, BLK, BLK), lambda b, h, qi: (h, 0, 0, 0)),
            pl.BlockSpec((4, HALF), lambda b, h, qi: (0, 0)),
            pl.BlockSpec((1, HEAD_DIM), lambda b, h, qi: (0, 0)),
        ],
        out_specs=pl.BlockSpec((BLK, wg), lambda b, h, qi: (b * nq + qi, h)),
        out_shape=jax.ShapeDtypeStruct((nb * seq, W_A), BF16),
        scratch_shapes=[pltpu.VMEM((tp, wg), BF16), pltpu.VMEM((ATT_HG, HEAD_DIM, tp), BF16),
                        pltpu.VMEM((ATT_HG, nblk, BLK, 2 * BLK), F32),
                        pltpu.VMEM((ATT_HG, tp, 2 * BLK), BF16)],
        compiler_params=_cparams(("parallel", "parallel", "arbitrary"), VMEM_LIMIT),
        name=name,
    )(proj, proj, proj, proj, proj, band, lamp, sub)


def _attn_meta_body(q_ref, k_ref, v_ref, bias_ref, lamp_ref, sub_ref, o_ref, *, lam_init):
    zpad = jnp.zeros((BLK - N_META, HEAD_DIM), F32)
    lam = _lambda(lamp_ref, lam_init)
    for h in range(H_A):
        sl = slice(h * HEAD_DIM, (h + 1) * HEAD_DIM)
        qs = _split_maps(q_ref[:, sl] * (HALF ** -0.5)).astype(BF16)
        kp = jnp.concatenate([k_ref[:, sl], zpad], axis=0).astype(BF16)
        vp = jnp.concatenate([v_ref[:, sl], zpad], axis=0).astype(BF16)
        s = lax.dot_general(qs, kp, (((1,), (1,)), ((), ())),
                            preferred_element_type=F32) + bias_ref[h]
        p = jnp.exp(s - jnp.max(s, axis=1, keepdims=True))
        o = jnp.dot(p.astype(BF16), vp, preferred_element_type=F32) / jnp.sum(p, axis=1,
                                                                              keepdims=True)
        a = o[0:N_META] - lam * o[N_META:]
        o_ref[:, sl] = _subln(a, sub_ref, lam_init).astype(o_ref.dtype)


def _attn_meta(proj, bias, lamp, sub, meta_blk, lam_init, name):
    def spec(col):
        return pl.BlockSpec((N_META, W_A), lambda i: (meta_blk, col))

    return pl.pallas_call(
        functools.partial(_attn_meta_body, lam_init=lam_init),
        grid=(1,),
        in_specs=[spec(0), spec(1), spec(2),
                  pl.BlockSpec((H_A, 2 * N_META, BLK), lambda i: (0, 0, 0)),
                  pl.BlockSpec((4, HALF), lambda i: (0, 0)),
                  pl.BlockSpec((1, HEAD_DIM), lambda i: (0, 0))],
        out_specs=pl.BlockSpec((N_META, W_A), lambda i: (0, 0)),
        out_shape=jax.ShapeDtypeStruct((N_META, W_A), BF16),
        compiler_params=_cparams(("arbitrary",)),
        name=name,
    )(proj, proj, proj, bias, lamp, sub)


PPS = 4


def _attn_sample_body(pt_ref, q_ref, kn_ref, vn_ref, *rest, lam_init, n_pages, n_new):
    kp_refs, vp_refs = rest[:PPS], rest[PPS:2 * PPS]
    bias_ref, lamp_ref, sub_ref, o_ref, qs_ref, m_ref, l_ref, acc_ref = rest[2 * PPS:]
    p = pl.program_id(1)
    n_steps = n_pages // PPS

    @pl.when(p == 0)
    def _():
        for h in range(H_A):
            q = q_ref[0:n_new, h * HEAD_DIM:(h + 1) * HEAD_DIM] * (HALF ** -0.5)
            qs = _split_maps(q)
            pad = jnp.zeros((SROWS - 2 * n_new, HEAD_DIM), F32)
            qs_ref[h] = jnp.concatenate([qs, pad], axis=0).astype(BF16)
        m_ref[...] = jnp.full(m_ref.shape, M_INIT, F32)
        l_ref[...] = jnp.zeros(l_ref.shape, F32)
        acc_ref[...] = jnp.zeros(acc_ref.shape, F32)

    def update(get_k, get_v, bias, n_rep):
        s = jnp.stack([lax.dot_general(qs_ref[h], get_k(h).astype(BF16), (((1,), (1,)), ((), ())),
                                       preferred_element_type=F32) for h in range(H_A)])
        s = s + bias
        m_prev = m_ref[...]
        m_new = jnp.maximum(m_prev, jnp.max(s, axis=-1, keepdims=True))
        alpha = jnp.exp(m_prev - m_new)
        pm = jnp.exp(s - jnp.concatenate([m_new] * n_rep, axis=-1))
        l_ref[...] = alpha * l_ref[...] + jnp.sum(pm, axis=-1, keepdims=True)
        m_ref[...] = m_new
        pb = pm.astype(BF16)
        for h in range(H_A):
            acc_ref[h] = alpha[h] * acc_ref[h] + jnp.dot(pb[h], get_v(h).astype(BF16),
                                                         preferred_element_type=F32)

    @pl.when(p < n_steps)
    def _():
        def pages(refs, h):
            return jnp.concatenate([r[h] for r in refs], axis=0)

        last_idx = jnp.where(p == n_steps - 1, 1, 0)
        bias = jnp.concatenate([bias_ref[0]] * (PPS - 1) + [bias_ref[last_idx]], axis=-1)
        update(functools.partial(pages, kp_refs), functools.partial(pages, vp_refs), bias, PPS)

    @pl.when(p == n_steps)
    def _():
        zpad = jnp.zeros((PAGE_SIZE - SROWS, HEAD_DIM), F32)

        def new_rows(ref, h):
            return jnp.concatenate([ref[:, h * HEAD_DIM:(h + 1) * HEAD_DIM], zpad], axis=0)

        update(functools.partial(new_rows, kn_ref), functools.partial(new_rows, vn_ref),
               bias_ref[2], 1)
        lam = _lambda(lamp_ref, lam_init)
        for h in range(H_A):
            o = acc_ref[h] / l_ref[h]
            a = o[0:n_new] - lam * o[n_new:2 * n_new]
            y = _subln(a, sub_ref, lam_init)
            pad = jnp.zeros((SROWS - n_new, HEAD_DIM), F32)
            o_ref[:, h * HEAD_DIM:(h + 1) * HEAD_DIM] = (
                jnp.concatenate([y, pad], axis=0).astype(o_ref.dtype))


def _attn_sample(page_table, proj, cache_k, cache_v, bias, lamp, sub, layer, row_blk0, n_new,
                 lam_init, name):
    db, n_pages = page_table.shape
    assert n_pages % PPS == 0
    body = functools.partial(_attn_sample_body, lam_init=lam_init, n_pages=n_pages, n_new=n_new)

    def page_spec(t):
        return pl.BlockSpec(
            (None, None, H_A, PAGE_SIZE, HEAD_DIM),
            lambda b, p, pt: (layer, pt[b, jnp.minimum(p * PPS + t, n_pages - 1)], 0, 0, 0))

    page_specs = [page_spec(t) for t in range(PPS)]
    grid_spec = pltpu.PrefetchScalarGridSpec(
        num_scalar_prefetch=1,
        grid=(db, n_pages // PPS + 1),
        in_specs=[
            pl.BlockSpec((SROWS, W_A), lambda b, p, pt: (row_blk0 + b, 0)),
            pl.BlockSpec((SROWS, W_A), lambda b, p, pt: (row_blk0 + b, 1)),
            pl.BlockSpec((SROWS, W_A), lambda b, p, pt: (row_blk0 + b, 2)),
            *page_specs, *page_specs,
            pl.BlockSpec((3, H_A, SROWS, PAGE_SIZE), lambda b, p, pt: (0, 0, 0, 0)),
            pl.BlockSpec((4, HALF), lambda b, p, pt: (0, 0)),
            pl.BlockSpec((1, HEAD_DIM), lambda b, p, pt: (0, 0)),
        ],
        out_specs=pl.BlockSpec((SROWS, W_A), lambda b, p, pt: (b, 0)),
        scratch_shapes=[pltpu.VMEM((H_A, SROWS, HEAD_DIM), BF16),
                        pltpu.VMEM((H_A, SROWS, HEAD_DIM), F32),
                        pltpu.VMEM((H_A, SROWS, HEAD_DIM), F32),
                        pltpu.VMEM((H_A, SROWS, HEAD_DIM), F32)],
    )
    return pl.pallas_call(
        body,
        grid_spec=grid_spec,
        out_shape=jax.ShapeDtypeStruct((db * SROWS, W_A), BF16),
        compiler_params=_cparams(("parallel", "arbitrary"), VMEM_LIMIT),
        name=name,
    )(page_table, proj, proj, proj, *([cache_k] * PPS), *([cache_v] * PPS), bias, lamp, sub)


RET_HG = 12


def _retention_body(q_ref, k_ref, v_ref, g_ref, c2_ref, s2_ref, lg_ref, gn_ref, s0_ref,
                    y_ref, sout_ref, st_ref, *, rows, n_chunks, last_valid):
    c = pl.program_id(2)

    @pl.when(c == 0)
    def _():
        st_ref[...] = s0_ref[...]

    lc = jnp.where(c == n_chunks - 1, float(last_valid), float(BLK))
    ii = lax.broadcasted_iota(jnp.int32, (BLK, BLK), 0).astype(F32)
    jj = lax.broadcasted_iota(jnp.int32, (BLK, BLK), 1).astype(F32)
    rel = ii - jj

    def rows128(x):
        if rows == BLK:
            return x
        return jnp.concatenate([x, jnp.zeros((BLK - rows, x.shape[1]), x.dtype)], axis=0)

    c2 = rows128(c2_ref[...])
    s2 = rows128(s2_ref[...])

    def rot(x):
        return x * c2 + pltpu.roll(x, HALF, 1) * s2

    heads = []
    for j in range(RET_HG):
        sl = slice(j * HEAD_DIM, (j + 1) * HEAD_DIM)
        lgv = lg_ref[:, sl]
        v = rows128(v_ref[:, sl]).astype(BF16)
        qr = rot(rows128(q_ref[:, sl])).astype(BF16)
        kr = rot(rows128(k_ref[:, sl])) * (HEAD_DIM ** -0.5)
        st = st_ref[j]
        sc = lax.dot_general(qr, kr.astype(BF16), (((1,), (1,)), ((), ())),
                             preferred_element_type=F32)
        cross = jnp.dot(qr, st.astype(BF16), preferred_element_type=F32)
        kdec = kr * jnp.where(ii < lc, jnp.exp((lc - 1.0 - ii) * lgv), 0.0)
        st_ref[j] = jnp.exp(lc * lgv) * st + jnp.dot(kdec.T.astype(BF16), v,
                                                     preferred_element_type=F32)
        heads.append((sl, lgv, v, sc, cross))
    for sl, lgv, v, sc, cross in heads:
        decay = jnp.where(rel >= 0, jnp.exp(lgv * jnp.maximum(rel, 0.0)), 0.0)
        o = jnp.dot((sc * decay).astype(BF16), v, preferred_element_type=F32)
        o = o + cross * jnp.exp((ii + 1.0) * lgv)
        y = o[:rows]
        y = y * lax.rsqrt(jnp.mean(y * y, axis=-1, keepdims=True) + EPS) * gn_ref[...]
        g = g_ref[:, sl]
        y_ref[:, sl] = (y * (g * _sigmoid(g))).astype(y_ref.dtype)

    @pl.when(c == n_chunks - 1)
    def _():
        sout_ref[...] = st_ref[...]


def _retention(proj, c2, s2, lgrow, gn, s0, nb, n_chunks, rows, row_blk0, last_valid, name,
               shared_init=False):
    body = functools.partial(_retention_body, rows=rows, n_chunks=n_chunks, last_valid=last_valid)
    wb = RET_HG * HEAD_DIM
    cb = wb // 128

    def in_spec(col0):
        return pl.BlockSpec((rows, wb),
                            lambda b, hg, c: (row_blk0 + b * n_chunks + c, col0 // cb + hg))

    st_block = (None, RET_HG, HEAD_DIM, HEAD_DIM)
    st_spec = pl.BlockSpec(st_block, lambda b, hg, c: (b, hg, 0, 0))
    s0_spec = pl.BlockSpec(st_block, lambda b, hg, c: (0, hg, 0, 0)) if shared_init else st_spec
    return pl.pallas_call(
        body,
        grid=(nb, H_B // RET_HG, n_chunks),
        in_specs=[in_spec(COL_QB), in_spec(COL_KB), in_spec(COL_VB), in_spec(COL_GB),
                  pl.BlockSpec((rows, HEAD_DIM), lambda b, hg, c: (c, 0)),
                  pl.BlockSpec((rows, HEAD_DIM), lambda b, hg, c: (c, 0)),
                  pl.BlockSpec((1, wb), lambda b, hg, c: (0, hg)),
                  pl.BlockSpec((1, HEAD_DIM), lambda b, hg, c: (0, 0)),
                  s0_spec],
        out_specs=[pl.BlockSpec((rows, wb), lambda b, hg, c: (b * n_chunks + c, hg)), st_spec],
        out_shape=[jax.ShapeDtypeStruct((nb * n_chunks * rows, W_B), BF16),
                   jax.ShapeDtypeStruct((nb, H_B, HEAD_DIM, HEAD_DIM), F32)],
        scratch_shapes=[pltpu.VMEM((RET_HG, HEAD_DIM, HEAD_DIM), F32)],
        compiler_params=_cparams(("parallel", "parallel", "arbitrary"), VMEM_LIMIT),
        name=name,
    )(proj, proj, proj, proj, c2, s2, lgrow, gn, s0)


CONV_LANES = 256


def _conv_body(ca0_ref, ca1_ref, cb0_ref, cb1_ref, ctx_ref, w_ref, b_ref, lg_ref, lb_ref,
               y_ref, tail_ref, uc_ref, us_ref, cv_ref, *, rows, n_chunks, last_valid):
    c = pl.program_id(1)

    @pl.when(c == 0)
    def _():
        uc_ref[0:CTX_ROWS, :] = ctx_ref[...]

    ca = jnp.concatenate([ca0_ref[...], ca1_ref[...]], axis=1)
    cb = jnp.concatenate([cb0_ref[...], cb1_ref[...]], axis=1)
    uc_ref[CTX_ROWS:CTX_ROWS + rows, :] = ca * _sigmoid(cb)
    span = CTX_ROWS + rows - 8
    for r in range(1, 8):
        us_ref[r] = uc_ref[pl.ds(r, span), :]
    for cc in range(0, W_C, CONV_LANES):
        cs = slice(cc, cc + CONV_LANES)
        acc = jnp.broadcast_to(b_ref[:, cs], (rows, CONV_LANES))
        for w in range(CONV_W):
            off = CTX_ROWS - (CONV_W - 1) + w
            r, a = off % 8, off - off % 8
            src = uc_ref[a:a + rows, cs] if r == 0 else us_ref[r, a:a + rows, cs]
            acc = acc + src * w_ref[w:w + 1, cs]
        cv_ref[:, cs] = acc
    acc = cv_ref[...]
    mu = jnp.mean(acc, axis=-1, keepdims=True)
    xc = acc - mu
    var = jnp.mean(xc * xc, axis=-1, keepdims=True)
    y = xc * lax.rsqrt(var + EPS) * lg_ref[...] + lb_ref[...]
    y_ref[...] = (y * _sigmoid(y)).astype(y_ref.dtype)

    @pl.when(c == n_chunks - 1)
    def _():
        tail_ref[...] = uc_ref[pl.ds(CTX_ROWS + last_valid - (CONV_W - 1), CONV_W - 1), :]

    nxt = uc_ref[rows:rows + CTX_ROWS, :]
    uc_ref[0:CTX_ROWS, :] = nxt


def _conv(proj, ctx0, conv_w, conv_b, ln_g, ln_b, layer, nb, n_chunks, rows, row_blk0,
          last_valid, name, shared_ctx=False):
    body = functools.partial(_conv_body, rows=rows, n_chunks=n_chunks, last_valid=last_valid)
    half = W_C // 2
    hb = half // 128

    def in_spec(col):
        return pl.BlockSpec((rows, half), lambda b, c: (row_blk0 + b * n_chunks + c, col))

    def par_spec():
        return pl.BlockSpec((None, 1, W_C), lambda b, c: (layer, 0, 0))

    return pl.pallas_call(
        body,
        grid=(nb, n_chunks),
        in_specs=[in_spec(COL_CA // hb), in_spec(COL_CA // hb + 1),
                  in_spec(COL_CB // hb), in_spec(COL_CB // hb + 1),
                  pl.BlockSpec((None, CTX_ROWS, W_C),
                               lambda b, c: (0 if shared_ctx else b, 0, 0)),
                  pl.BlockSpec((None, CONV_W, W_C), lambda b, c: (layer, 0, 0)),
                  par_spec(), par_spec(), par_spec()],
        out_specs=[pl.BlockSpec((rows, W_C), lambda b, c: (b * n_chunks + c, 0)),
                   pl.BlockSpec((None, CONV_W - 1, W_C), lambda b, c: (b, 0, 0))],
        out_shape=[jax.ShapeDtypeStruct((nb * n_chunks * rows, W_C), BF16),
                   jax.ShapeDtypeStruct((nb, CONV_W - 1, W_C), F32)],
        scratch_shapes=[pltpu.VMEM((CTX_ROWS + rows, W_C), F32),
                        pltpu.VMEM((8, CTX_ROWS + rows - 8, W_C), F32),
                        pltpu.VMEM((rows, W_C), F32)],
        compiler_params=_cparams(("parallel", "arbitrary"), VMEM_LIMIT),
        name=name,
    )(proj, proj, proj, proj, ctx0, conv_w, conv_b, ln_g, ln_b)


def _t5_bias(rel_bias, dist):
    n = jnp.maximum(dist, 0)
    nf = jnp.maximum(n, 1).astype(F32)
    large = REL_EXACT + (jnp.log(nf / REL_EXACT) / math.log(REL_MAX_DIST / REL_EXACT)
                         * (N_BUCKETS - REL_EXACT)).astype(jnp.int32)
    bucket = jnp.where(n < REL_EXACT, n, jnp.minimum(large, N_BUCKETS - 1))
    onehot = (bucket[..., None] == jnp.arange(N_BUCKETS, dtype=jnp.int32)).astype(F32)
    bias = jnp.einsum("...k,kh->h...", onehot, rel_bias.astype(F32),
                      precision=lax.Precision.HIGHEST)
    return jnp.where(dist[None] >= 0, bias, NEG_INF)


def _rotary_tables(pos):
    inv = ROPE_BASE ** (-jnp.arange(HALF, dtype=F32) / HALF)
    ang = pos.astype(F32)[:, None] * inv[None, :]
    cos, sin = jnp.cos(ang), jnp.sin(ang)
    return jnp.concatenate([cos, cos], axis=1), jnp.concatenate([-sin, sin], axis=1)


def kernel(x_prompt, x_sample, cache_k, cache_v, state_ret, state_conv, page_table, meta, rel_bias,
           norm_mix, w_in, lam_q1, lam_k1, lam_q2, lam_k2, subln_a, gn_b, conv_w, conv_b, conv_ln_g,
           conv_ln_b, w_out, norm_ffn, w_gate, w_up, w_down, norm_final):
    nb, seq = x_prompt.shape[:2]
    db, n_new = x_sample.shape[:2]
    depth = w_in.shape[0]
    n_pages = page_table.shape[1]
    past = n_pages * PAGE_SIZE
    assert seq % BLK == 0 and N_META == SROWS and BLK >= REL_MAX_DIST and PAGE_SIZE >= REL_MAX_DIST
    real = nb * seq
    samp0 = real + BLK
    m_all = samp0 + db * SROWS
    meta_blk = real // SROWS
    s_blk0 = samp0 // SROWS
    d_ff = w_gate.shape[2]

    tm = m_all // 6
    tr = m_all // 24

    h = jnp.concatenate([x_prompt.reshape(real, D_MODEL), meta.astype(F32),
                         jnp.zeros((BLK - N_META, D_MODEL), F32),
                         jnp.pad(x_sample, ((0, 0), (0, SROWS - n_new), (0, 0))
                                 ).reshape(db * SROWS, D_MODEL)], axis=0)

    kk = jnp.arange(BLK, dtype=jnp.int32)[:, None]
    ii = jnp.arange(BLK, dtype=jnp.int32)[None, :]
    is_meta = kk < N_META
    far_d = jnp.full((BLK, BLK), REL_MAX_DIST, jnp.int32)
    band = _t5_bias(rel_bias, jnp.stack([
        far_d, BLK + ii - kk, ii - kk, jnp.full((BLK, BLK), -1, jnp.int32),
        jnp.where(is_meta, N_META + ii - kk, -1), jnp.where(is_meta, far_d, -1)]))
    rm = jnp.arange(2 * N_META, dtype=jnp.int32)[:, None] % N_META
    jm = jnp.arange(BLK, dtype=jnp.int32)[None, :]
    bias_m = _t5_bias(rel_bias, jnp.where(jm < N_META, rm - jm, -1))
    r_new = jnp.arange(SROWS, dtype=jnp.int32) % n_new
    jk = jnp.arange(PAGE_SIZE, dtype=jnp.int32)[None, :]
    dist_s = jnp.stack([jnp.full((SROWS, PAGE_SIZE), REL_MAX_DIST, jnp.int32),
                        PAGE_SIZE + r_new[:, None] - jk,
                        jnp.where(jk < n_new, r_new[:, None] - jk, -1)])
    bias_s = jnp.transpose(_t5_bias(rel_bias, dist_s), (1, 0, 2, 3))
    cache_kh = jnp.transpose(cache_k, (0, 1, 3, 2, 4))
    cache_vh = jnp.transpose(cache_v, (0, 1, 3, 2, 4))
    c2_m, s2_m = _rotary_tables(jnp.arange(N_META, dtype=jnp.int32))
    c2_p, s2_p = _rotary_tables(N_META + jnp.arange(seq, dtype=jnp.int32))
    c2_s, s2_s = _rotary_tables(past + jnp.arange(SROWS, dtype=jnp.int32))
    log_gamma = jnp.log1p(-jnp.exp2(-5.0 - jnp.arange(H_B, dtype=F32)))
    lgrow = jnp.repeat(log_gamma, HEAD_DIM)[None, :]

    zero_state = jnp.zeros((1, H_B, HEAD_DIM, HEAD_DIM), F32)
    zero_ctx = jnp.zeros((1, CTX_ROWS, W_C), F32)
    ctx_pad = ((0, 0), (CTX_ROWS - (CONV_W - 1), 0), (0, 0))

    def small_rows(y_meta, y_sample):
        return jnp.concatenate([y_meta, jnp.zeros((BLK - N_META, y_meta.shape[1]), y_meta.dtype),
                                y_sample], axis=0)

    conv_b3 = conv_b[:, None, :]
    ln_g3 = conv_ln_g[:, None, :]
    ln_b3 = conv_ln_b[:, None, :]

    projs, ret_p, conv_p, k_s, v_s, ret_s, conv_s = [], [], [], [], [], [], []
    for l in range(depth):
        lam_init = 0.8 - 0.6 * math.exp(-0.3 * l)
        lamp = jnp.stack([lam_q1[l], lam_k1[l], lam_q2[l], lam_k2[l]]).astype(F32)
        sub = subln_a[l][None, :]
        gn = gn_b[l][None, :]

        n1 = _rmsnorm(h, norm_mix[l], BF16, tr, f"norm_mix{l}")
        proj = _matmul(n1, w_in, l, tm, 512, f"in_proj{l}")

        ya_m = _attn_meta(proj, bias_m, lamp, sub, meta_blk, lam_init, f"attn_meta{l}")
        yb_m, st_m = _retention(proj, c2_m, s2_m, lgrow, gn, zero_state, 1, 1, N_META, meta_blk,
                                N_META, f"ret_meta{l}")
        yc_m, tail_m = _conv(proj, zero_ctx, conv_w, conv_b3, ln_g3, ln_b3, l, 1, 1, N_META,
                             meta_blk, N_META, f"conv_meta{l}")

        ya_p = _attn_prompt(proj, band, lamp, sub, nb, seq, meta_blk, lam_init, f"attn_prompt{l}")
        ya_s = _attn_sample(page_table, proj, cache_kh, cache_vh, bias_s, lamp, sub, l, s_blk0,
                            n_new, lam_init, f"attn_sample{l}")
        yb_p, st_p = _retention(proj, c2_p, s2_p, lgrow, gn, st_m, nb, seq // BLK, BLK, 0, BLK,
                                f"ret_prompt{l}", shared_init=True)
        yb_s, st_s = _retention(proj, c2_s, s2_s, lgrow, gn, state_ret[l], db, 1, SROWS,
                                s_blk0, n_new, f"ret_sample{l}")
        yc_p, tail_p = _conv(proj, jnp.pad(tail_m, ctx_pad), conv_w, conv_b3, ln_g3, ln_b3, l, nb,
                             seq // BLK, BLK, 0, BLK, f"conv_prompt{l}", shared_ctx=True)
        yc_s, tail_s = _conv(proj, jnp.pad(state_conv[l], ctx_pad), conv_w, conv_b3, ln_g3, ln_b3,
                             l, db, 1, SROWS, s_blk0, n_new, f"conv_sample{l}")

        h = _out_proj(ya_p, yb_p, yc_p, w_out, l, h, 0, real // 8, 512, f"out_proj_p{l}")
        n_small = m_all - real
        h = _out_proj(small_rows(ya_m, ya_s), small_rows(yb_m, yb_s), small_rows(yc_m, yc_s),
                      w_out, l, h, real // n_small, n_small, 256, f"out_proj_s{l}")
        n2 = _rmsnorm(h, norm_ffn[l], BF16, tr, f"norm_ffn{l}")
        hid = _ffn_up(n2, w_gate, w_up, l, tm, 256, f"ffn_up{l}")
        for kb in range(2):
            h = _matmul(hid, w_down, l, tm, 256, f"ffn_down{l}_{kb}", res=h,
                        tk=d_ff // 2, kblk=kb)

        projs.append(proj)
        kvs = proj[samp0:, W_A:3 * W_A].reshape(db, SROWS, 2, H_A, HEAD_DIM)[:, :n_new]
        k_s.append(kvs[:, :, 0])
        v_s.append(kvs[:, :, 1])
        ret_p.append(st_p)
        ret_s.append(st_s)
        conv_p.append(tail_p)
        conv_s.append(tail_s)

    y_prompt = _rmsnorm(h, norm_final, F32, 2 * BLK, "norm_final_p", rows=real).reshape(
        nb, seq, D_MODEL)
    hs = _rmsnorm(h[samp0:], norm_final, F32, db * SROWS, "norm_final_s")
    y_sample = hs.reshape(db, SROWS, D_MODEL)[:, :n_new]
    k_hm, v_hm = _kv_out(projs, nb, seq, meta_blk, "kv_out")
    k_prompt = jnp.transpose(k_hm, (0, 1, 3, 2, 4))
    v_prompt = jnp.transpose(v_hm, (0, 1, 3, 2, 4))
    return (y_prompt, y_sample, k_prompt, v_prompt, jnp.stack(ret_p), jnp.stack(conv_p),
            jnp.stack(k_s), jnp.stack(v_s), jnp.stack(ret_s), jnp.stack(conv_s))
```

```python
import functools
import math

import jax
import jax.numpy as jnp
from jax import lax
from jax.experimental import pallas as pl
from jax.experimental.pallas import tpu as pltpu

F32 = jnp.float32
BF16 = jnp.bfloat16

D_MODEL = 4096
N_META = 16
HEAD_DIM = 128
HALF = HEAD_DIM // 2
H_A = 12
H_B = 12
W_A = H_A * HEAD_DIM
W_B = H_B * HEAD_DIM
W_C = D_MODEL - W_A - W_B
N_IN = 3 * W_A + 4 * W_B + 2 * W_C
CONV_W = 31
N_BUCKETS = 32
REL_EXACT = 16
REL_MAX_DIST = 128
ROPE_BASE = 10000.0
EPS = 1e-6
NEG_INF = -1e30
PAGE_SIZE = 128

BLK = 128
SROWS = 16
CTX_ROWS = 32
M_INIT = -3.0e38
VMEM_LIMIT = 56 * 1024 * 1024

COL_QA, COL_KA, COL_VA = 0, W_A // 128, 2 * W_A // 128
COL_QB = 3 * W_A // 128
COL_KB = COL_QB + W_B // 128
COL_VB = COL_KB + W_B // 128
COL_GB = COL_VB + W_B // 128
COL_CA = COL_GB + W_B // 128
COL_CB = COL_CA + W_C // 128


def _cparams(sem, vmem=None):
    return pltpu.CompilerParams(dimension_semantics=sem, vmem_limit_bytes=vmem)


def _sigmoid(x):
    return 1.0 / (1.0 + jnp.exp(-x))


def _rmsnorm_body(x_ref, g_ref, o_ref):
    x = x_ref[...]
    ms = jnp.mean(x * x, axis=-1, keepdims=True)
    o_ref[...] = (x * lax.rsqrt(ms + EPS) * g_ref[...]).astype(o_ref.dtype)


def _rmsnorm(x, g, out_dtype, tr, name, rows=None):
    m, d = x.shape
    m = m if rows is None else rows
    return pl.pallas_call(
        _rmsnorm_body,
        grid=(m // tr,),
        in_specs=[pl.BlockSpec((tr, d), lambda i: (i, 0)),
                  pl.BlockSpec((1, d), lambda i: (0, 0))],
        out_specs=pl.BlockSpec((tr, d), lambda i: (i, 0)),
        out_shape=jax.ShapeDtypeStruct((m, d), out_dtype),
        compiler_params=_cparams(("parallel",)),
        name=name,
    )(x, g.reshape(1, d))


def _mm_body(a_ref, w_ref, o_ref):
    o_ref[...] = jnp.dot(a_ref[...], w_ref[...].astype(BF16), preferred_element_type=F32)


def _mm_res_body(a_ref, w_ref, r_ref, o_ref):
    o_ref[...] = r_ref[...] + jnp.dot(a_ref[...], w_ref[...].astype(BF16),
                                      preferred_element_type=F32)


def _matmul(a, w, layer, tm, tn, name, res=None, tk=None, kblk=0):
    m = a.shape[0]
    k = a.shape[1] if tk is None else tk
    n = w.shape[2]
    in_specs = [pl.BlockSpec((tm, k), lambda i, j: (i, kblk)),
                pl.BlockSpec((None, k, tn), lambda i, j: (layer, kblk, j))]
    args = [a, w]
    body = _mm_body
    if res is not None:
        in_specs.append(pl.BlockSpec((tm, tn), lambda i, j: (i, j)))
        args.append(res)
        body = _mm_res_body
    return pl.pallas_call(
        body,
        grid=(m // tm, n // tn),
        in_specs=in_specs,
        out_specs=pl.BlockSpec((tm, tn), lambda i, j: (i, j)),
        out_shape=jax.ShapeDtypeStruct((m, n), F32),
        compiler_params=_cparams(("parallel", "parallel"), VMEM_LIMIT),
        name=name,
    )(*args)


def _ffn_up_body(a_ref, wg_ref, wu_ref, o_ref):
    a = a_ref[...]
    g = jnp.dot(a, wg_ref[...].astype(BF16), preferred_element_type=F32)
    u = jnp.dot(a, wu_ref[...].astype(BF16), preferred_element_type=F32)
    o_ref[...] = (g * _sigmoid(g) * u).astype(o_ref.dtype)


def _ffn_up(a, wg, wu, layer, tm, tn, name):
    m, k = a.shape
    n = wg.shape[2]
    wspec = pl.BlockSpec((None, k, tn), lambda i, j: (layer, 0, j))
    return pl.pallas_call(
        _ffn_up_body,
        grid=(m // tm, n // tn),
        in_specs=[pl.BlockSpec((tm, k), lambda i, j: (i, 0)), wspec, wspec],
        out_specs=pl.BlockSpec((tm, tn), lambda i, j: (i, j)),
        out_shape=jax.ShapeDtypeStruct((m, n), BF16),
        compiler_params=_cparams(("parallel", "parallel"), VMEM_LIMIT),
        name=name,
    )(a, wg, wu)


def _out_proj_body(ya_ref, yb_ref, yc_ref, w_ref, r_ref, o_ref):
    acc = jnp.dot(ya_ref[...], w_ref[0:W_A, :].astype(BF16), preferred_element_type=F32)
    acc += jnp.dot(yb_ref[...], w_ref[W_A:W_A + W_B, :].astype(BF16), preferred_element_type=F32)
    acc += jnp.dot(yc_ref[...], w_ref[W_A + W_B:, :].astype(BF16), preferred_element_type=F32)
    o_ref[...] = r_ref[...] + acc


def _out_proj(ya, yb, yc, w, layer, h, row_tile0, tm, tn, name):
    rows = ya.shape[0]
    n = w.shape[2]
    hspec = pl.BlockSpec((tm, tn), lambda i, j: (row_tile0 + i, j))
    return pl.pallas_call(
        _out_proj_body,
        grid=(rows // tm, n // tn),
        in_specs=[pl.BlockSpec((tm, W_A), lambda i, j: (i, 0)),
                  pl.BlockSpec((tm, W_B), lambda i, j: (i, 0)),
                  pl.BlockSpec((tm, W_C), lambda i, j: (i, 0)),
                  pl.BlockSpec((None, W_A + W_B + W_C, tn), lambda i, j: (layer, 0, j)),
                  hspec],
        out_specs=hspec,
        out_shape=jax.ShapeDtypeStruct(h.shape, F32),
        input_output_aliases={4: 0},
        compiler_params=_cparams(("parallel", "parallel"), VMEM_LIMIT),
        name=name,
    )(ya, yb, yc, w, h)


def _kv_out_body(*refs, depth, seq):
    ko_ref, vo_ref = refs[4 * depth:]
    for l in range(depth):
        km_ref, kr_ref, vm_ref, vr_ref = refs[4 * l:4 * l + 4]
        ko_ref[l, 0:N_META] = km_ref[...]
        ko_ref[l, N_META:N_META + seq] = kr_ref[...]
        vo_ref[l, 0:N_META] = vm_ref[...]
        vo_ref[l, N_META:N_META + seq] = vr_ref[...]


def _kv_out(projs, nb, seq, meta_blk, name):
    depth = len(projs)
    t_valid = N_META + seq
    in_specs, args = [], []
    for proj in projs:
        for col in (COL_KA, COL_VA):
            in_specs += [pl.BlockSpec((N_META, HEAD_DIM), lambda b, h, col=col: (meta_blk, col + h)),
                         pl.BlockSpec((seq, HEAD_DIM), lambda b, h, col=col: (b, col + h))]
            args += [proj, proj]
    ospec = pl.BlockSpec((depth, None, None, t_valid, HEAD_DIM), lambda b, h: (0, b, h, 0, 0))
    oshape = jax.ShapeDtypeStruct((depth, nb, H_A, t_valid, HEAD_DIM), F32)
    return pl.pallas_call(
        functools.partial(_kv_out_body, depth=depth, seq=seq),
        grid=(nb, H_A),
        in_specs=in_specs,
        out_specs=[ospec, ospec],
        out_shape=[oshape, oshape],
        compiler_params=_cparams(("parallel", "parallel")),
        name=name,
    )(*args)


def _lambda(lamp_ref, lam_init):
    lp = lamp_ref[...]
    s1 = jnp.sum(lp[0:1] * lp[1:2], axis=-1, keepdims=True)
    s2 = jnp.sum(lp[2:3] * lp[3:4], axis=-1, keepdims=True)
    return jnp.exp(s1) - jnp.exp(s2) + lam_init


def _split_maps(q):
    lane = lax.broadcasted_iota(jnp.int32, q.shape, 1)
    lo = jnp.where(lane < HALF, q, 0.0)
    hi = jnp.where(lane >= HALF, q, 0.0)
    return jnp.concatenate([lo, hi], axis=0)


def _subln(a, sub_ref, lam_init):
    y = a * lax.rsqrt(jnp.mean(a * a, axis=-1, keepdims=True) + EPS) * sub_ref[...]
    return y * (1.0 - lam_init)


ATT_HG = 4


def _attn_prompt_body(q_ref, k_ref, v_ref, km_ref, vm_ref, band_ref, lamp_ref, sub_ref, o_ref,
                      kb_ref, vt_ref, s_ref, p_ref, *, lam_init, nblk):
    qi = pl.program_id(2)
    lanes = [slice(g * HEAD_DIM, (g + 1) * HEAD_DIM) for g in range(ATT_HG)]

    @pl.when(qi == 0)
    def _():
        zpad = jnp.zeros((BLK - N_META, ATT_HG * HEAD_DIM), F32)
        kb_ref[0:BLK, :] = jnp.concatenate([km_ref[...], zpad], axis=0).astype(BF16)
        kb_ref[BLK:, :] = k_ref[...].astype(BF16)
        vm = jnp.concatenate([vm_ref[...], zpad], axis=0)
        for g in range(ATT_HG):
            vt_ref[g, :, 0:BLK] = vm[:, lanes[g]].T.astype(BF16)
            for j in range(1, nblk):
                vt = v_ref[(j - 1) * BLK:j * BLK, lanes[g]].T
                vt_ref[g, :, j * BLK:(j + 1) * BLK] = vt.astype(BF16)

    def attend(nt):
        lam = _lambda(lamp_ref, lam_init)
        mrun = [None] * ATT_HG
        lrun = [None] * ATT_HG
        m = [None] * ATT_HG
        qst = [None] * ATT_HG

        def score_tile(g, j):
            if j == 0:
                qst[g] = _split_maps(q_ref[:, lanes[g]] * (HALF ** -0.5)).T.astype(BF16)
            if j == 0:
                bt = band_ref[g, jnp.where(qi == 0, 4, 5)]
            else:
                bt = band_ref[g, jnp.clip(j - qi + 1, 0, 3)]
            s = jnp.dot(kb_ref[j * BLK:(j + 1) * BLK, lanes[g]], qst[g],
                        preferred_element_type=F32)
            s = s + jnp.concatenate([bt, bt], axis=1)
            s_ref[g, j] = s
            mrun[g] = s if mrun[g] is None else jnp.maximum(mrun[g], s)
            if j == nt - 1:
                m[g] = jnp.max(mrun[g], axis=0, keepdims=True)

        def exp_tile(g, j):
            p = jnp.exp(s_ref[g, j] - m[g])
            lrun[g] = p if lrun[g] is None else lrun[g] + p
            p_ref[g, j * BLK:(j + 1) * BLK, :] = p.astype(BF16)
            if j == nt - 1:
                l = jnp.sum(lrun[g], axis=0, keepdims=True)
                ot = jnp.dot(vt_ref[g, :, 0:nt * BLK], p_ref[g, 0:nt * BLK, :],
                             preferred_element_type=F32) / l
                at = ot[:, :BLK] - lam * ot[:, BLK:]
                at = at * lax.rsqrt(jnp.mean(at * at, axis=0, keepdims=True) + EPS)
                o_ref[:, lanes[g]] = (at.T * sub_ref[...] * (1.0 - lam_init)).astype(o_ref.dtype)

        for step in range(ATT_HG + 1):
            for j in range(nt):
                if step < ATT_HG:
                    score_tile(step, j)
                if step >= 1:
                    exp_tile(step - 1, j)

    lo = 0
    for hi in sorted(set(range(2, nblk, 3)) | {nblk}):
        pl.when(jnp.logical_and(qi + 2 > lo, qi + 2 <= hi))(functools.partial(attend, hi))
        lo = hi


def _attn_prompt(proj, band, lamp, sub, nb, seq, meta_blk, lam_init, name):
    nq = seq // BLK
    nblk = nq + 1
    tp = nblk * BLK
    wg = ATT_HG * HEAD_DIM
    body = functools.partial(_attn_prompt_body, lam_init=lam_init, nblk=nblk)
    return pl.pallas_call(
        body,
        grid=(nb, H_A // ATT_HG, nq),
        in_specs=[
            pl.BlockSpec((BLK, wg), lambda b, h, qi: (b * nq + qi, COL_QA // ATT_HG + h)),
            pl.BlockSpec((seq, wg), lambda b, h, qi: (b, COL_KA // ATT_HG + h)),
            pl.BlockSpec((seq, wg), lambda b, h, qi: (b, COL_VA // ATT_HG + h)),
            pl.BlockSpec((N_META, wg), lambda b, h, qi: (meta_blk, COL_KA // ATT_HG + h)),
            pl.BlockSpec((N_META, wg), lambda b, h, qi: (meta_blk, COL_VA // ATT_HG + h)),
            pl.BlockSpec((ATT_HG, 6, BLK, BLK), lambda b, h, qi: (h, 0, 0, 0)),
            pl.BlockSpec((4, HALF), lambda b, h, qi: (0, 0)),
            pl.BlockSpec((1, HEAD_DIM), lambda b, h, qi: (0, 0)),
        ],
        out_specs=pl.BlockSpec((BLK, wg), lambda b, h, qi: (b * nq + qi, h)),
        out_shape=jax.ShapeDtypeStruct((nb * seq, W_A), BF16),
        scratch_shapes=[pltpu.VMEM((tp, wg), BF16), pltpu.VMEM((ATT_HG, HEAD_DIM, tp), BF16),
                        pltpu.VMEM((ATT_HG, nblk, BLK, 2 * BLK), F32),
                        pltpu.VMEM((ATT_HG, tp, 2 * BLK), BF16)],
        compiler_params=_cparams(("parallel", "parallel", "arbitrary"), VMEM_LIMIT),
        name=name,
    )(proj, proj, proj, proj, proj, band, lamp, sub)


def _attn_meta_body(q_ref, k_ref, v_ref, bias_ref, lamp_ref, sub_ref, o_ref, *, lam_init):
    zpad = jnp.zeros((BLK - N_META, HEAD_DIM), F32)
    lam = _lambda(lamp_ref, lam_init)
    for h in range(H_A):
        sl = slice(h * HEAD_DIM, (h + 1) * HEAD_DIM)
        qs = _split_maps(q_ref[:, sl] * (HALF ** -0.5)).astype(BF16)
        kp = jnp.concatenate([k_ref[:, sl], zpad], axis=0).astype(BF16)
        vp = jnp.concatenate([v_ref[:, sl], zpad], axis=0).astype(BF16)
        s = lax.dot_general(qs, kp, (((1,), (1,)), ((), ())),
                            preferred_element_type=F32) + bias_ref[h]
        p = jnp.exp(s - jnp.max(s, axis=1, keepdims=True))
        o = jnp.dot(p.astype(BF16), vp, preferred_element_type=F32) / jnp.sum(p, axis=1,
                                                                              keepdims=True)
        a = o[0:N_META] - lam * o[N_META:]
        o_ref[:, sl] = _subln(a, sub_ref, lam_init).astype(o_ref.dtype)


def _attn_meta(proj, bias, lamp, sub, meta_blk, lam_init, name):
    def spec(col):
        return pl.BlockSpec((N_META, W_A), lambda i: (meta_blk, col))

    return pl.pallas_call(
        functools.partial(_attn_meta_body, lam_init=lam_init),
        grid=(1,),
        in_specs=[spec(0), spec(1), spec(2),
                  pl.BlockSpec((H_A, 2 * N_META, BLK), lambda i: (0, 0, 0)),
                  pl.BlockSpec((4, HALF), lambda i: (0, 0)),
                  pl.BlockSpec((1, HEAD_DIM), lambda i: (0, 0))],
        out_specs=pl.BlockSpec((N_META, W_A), lambda i: (0, 0)),
        out_shape=jax.ShapeDtypeStruct((N_META, W_A), BF16),
        compiler_params=_cparams(("arbitrary",)),
        name=name,
    )(proj, proj, proj, bias, lamp, sub)


PPS = 8


def _attn_sample_body(pt_ref, q_ref, kn_ref, vn_ref, *rest, lam_init, n_pages, n_new):
    kp_refs, vp_refs = rest[:PPS], rest[PPS:2 * PPS]
    bias_ref, lamp_ref, sub_ref, o_ref, qs_ref, m_ref, l_ref, acc_ref = rest[2 * PPS:]
    p = pl.program_id(1)
    n_steps = n_pages // PPS

    @pl.when(p == 0)
    def _():
        for h in range(H_A):
            q = q_ref[0:n_new, h * HEAD_DIM:(h + 1) * HEAD_DIM] * (HALF ** -0.5)
            qs = _split_maps(q)
            pad = jnp.zeros((SROWS - 2 * n_new, HEAD_DIM), F32)
            qs_ref[h] = jnp.concatenate([qs, pad], axis=0).astype(BF16)
        m_ref[...] = jnp.full(m_ref.shape, M_INIT, F32)
        l_ref[...] = jnp.zeros(l_ref.shape, F32)
        acc_ref[...] = jnp.zeros(acc_ref.shape, F32)

    def update(get_k, get_v, bias, n_rep):
        s = jnp.stack([lax.dot_general(qs_ref[h], get_k(h).astype(BF16), (((1,), (1,)), ((), ())),
                                       preferred_element_type=F32) for h in range(H_A)])
        s = s + bias
        m_prev = m_ref[...]
        m_new = jnp.maximum(m_prev, jnp.max(s, axis=-1, keepdims=True))
        alpha = jnp.exp(m_prev - m_new)
        pm = jnp.exp(s - jnp.concatenate([m_new] * n_rep, axis=-1))
        l_ref[...] = alpha * l_ref[...] + jnp.sum(pm, axis=-1, keepdims=True)
        m_ref[...] = m_new
        pb = pm.astype(BF16)
        for h in range(H_A):
            acc_ref[h] = alpha[h] * acc_ref[h] + jnp.dot(pb[h], get_v(h).astype(BF16),
                                                         preferred_element_type=F32)

    @pl.when(p < n_steps)
    def _():
        def pages(refs, h):
            return jnp.concatenate([r[h] for r in refs], axis=0)

        last_idx = jnp.where(p == n_steps - 1, 1, 0)
        bias = jnp.concatenate([bias_ref[0]] * (PPS - 1) + [bias_ref[last_idx]], axis=-1)
        update(functools.partial(pages, kp_refs), functools.partial(pages, vp_refs), bias, PPS)

    @pl.when(p == n_steps)
    def _():
        zpad = jnp.zeros((PAGE_SIZE - SROWS, HEAD_DIM), F32)

        def new_rows(ref, h):
            return jnp.concatenate([ref[:, h * HEAD_DIM:(h + 1) * HEAD_DIM], zpad], axis=0)

        update(functools.partial(new_rows, kn_ref), functools.partial(new_rows, vn_ref),
               bias_ref[2], 1)
        lam = _lambda(lamp_ref, lam_init)
        for h in range(H_A):
            o = acc_ref[h] / l_ref[h]
            a = o[0:n_new] - lam * o[n_new:2 * n_new]
            y = _subln(a, sub_ref, lam_init)
            pad = jnp.zeros((SROWS - n_new, HEAD_DIM), F32)
            o_ref[:, h * HEAD_DIM:(h + 1) * HEAD_DIM] = (
                jnp.concatenate([y, pad], axis=0).astype(o_ref.dtype))


def _attn_sample(page_table, proj, cache_k, cache_v, bias, lamp, sub, layer, row_blk0, n_new,
                 lam_init, name):
    db, n_pages = page_table.shape
    assert n_pages % PPS == 0
    body = functools.partial(_attn_sample_body, lam_init=lam_init, n_pages=n_pages, n_new=n_new)

    def page_spec(t):
        return pl.BlockSpec(
            (None, None, H_A, PAGE_SIZE, HEAD_DIM),
            lambda b, p, pt: (layer, pt[b, jnp.minimum(p * PPS + t, n_pages - 1)], 0, 0, 0))

    page_specs = [page_spec(t) for t in range(PPS)]
    grid_spec = pltpu.PrefetchScalarGridSpec(
        num_scalar_prefetch=1,
        grid=(db, n_pages // PPS + 1),
        in_specs=[
            pl.BlockSpec((SROWS, W_A), lambda b, p, pt: (row_blk0 + b, 0)),
            pl.BlockSpec((SROWS, W_A), lambda b, p, pt: (row_blk0 + b, 1)),
            pl.BlockSpec((SROWS, W_A), lambda b, p, pt: (row_blk0 + b, 2)),
            *page_specs, *page_specs,
            pl.BlockSpec((3, H_A, SROWS, PAGE_SIZE), lambda b, p, pt: (0, 0, 0, 0)),
            pl.BlockSpec((4, HALF), lambda b, p, pt: (0, 0)),
            pl.BlockSpec((1, HEAD_DIM), lambda b, p, pt: (0, 0)),
        ],
        out_specs=pl.BlockSpec((SROWS, W_A), lambda b, p, pt: (b, 0)),
        scratch_shapes=[pltpu.VMEM((H_A, SROWS, HEAD_DIM), BF16),
                        pltpu.VMEM((H_A, SROWS, HEAD_DIM), F32),
                        pltpu.VMEM((H_A, SROWS, HEAD_DIM), F32),
                        pltpu.VMEM((H_A, SROWS, HEAD_DIM), F32)],
    )
    return pl.pallas_call(
        body,
        grid_spec=grid_spec,
        out_shape=jax.ShapeDtypeStruct((db * SROWS, W_A), BF16),
        compiler_params=_cparams(("parallel", "arbitrary"), VMEM_LIMIT),
        name=name,
    )(page_table, proj, proj, proj, *([cache_k] * PPS), *([cache_v] * PPS), bias, lamp, sub)


RET_HG = 12


def _retention_body(q_ref, k_ref, v_ref, g_ref, c2_ref, s2_ref, lg_ref, gn_ref, s0_ref,
                    y_ref, sout_ref, st_ref, *, rows, n_chunks, last_valid):
    c = pl.program_id(2)

    @pl.when(c == 0)
    def _():
        st_ref[...] = s0_ref[...]

    lc = jnp.where(c == n_chunks - 1, float(last_valid), float(BLK))
    ii = lax.broadcasted_iota(jnp.int32, (BLK, BLK), 0).astype(F32)
    jj = lax.broadcasted_iota(jnp.int32, (BLK, BLK), 1).astype(F32)
    rel = ii - jj

    def rows128(x):
        if rows == BLK:
            return x
        return jnp.concatenate([x, jnp.zeros((BLK - rows, x.shape[1]), x.dtype)], axis=0)

    c2 = rows128(c2_ref[...])
    s2 = rows128(s2_ref[...])

    def rot(x):
        return x * c2 + pltpu.roll(x, HALF, 1) * s2

    heads = []
    for j in range(RET_HG):
        sl = slice(j * HEAD_DIM, (j + 1) * HEAD_DIM)
        lgv = lg_ref[:, sl]
        v = rows128(v_ref[:, sl]).astype(BF16)
        qr = rot(rows128(q_ref[:, sl])).astype(BF16)
        kr = rot(rows128(k_ref[:, sl])) * (HEAD_DIM ** -0.5)
        st = st_ref[j]
        sc = lax.dot_general(qr, kr.astype(BF16), (((1,), (1,)), ((), ())),
                             preferred_element_type=F32)
        cross = jnp.dot(qr, st.astype(BF16), preferred_element_type=F32)
        kdec = kr * jnp.where(ii < lc, jnp.exp((lc - 1.0 - ii) * lgv), 0.0)
        st_ref[j] = jnp.exp(lc * lgv) * st + jnp.dot(kdec.T.astype(BF16), v,
                                                     preferred_element_type=F32)
        heads.append((sl, lgv, v, sc, cross))
    for sl, lgv, v, sc, cross in heads:
        decay = jnp.where(rel >= 0, jnp.exp(lgv * jnp.maximum(rel, 0.0)), 0.0)
        o = jnp.dot((sc * decay).astype(BF16), v, preferred_element_type=F32)
        o = o + cross * jnp.exp((ii + 1.0) * lgv)
        y = o[:rows]
        y = y * lax.rsqrt(jnp.mean(y * y, axis=-1, keepdims=True) + EPS) * gn_ref[...]
        g = g_ref[:, sl]
        y_ref[:, sl] = (y * (g * _sigmoid(g))).astype(y_ref.dtype)

    @pl.when(c == n_chunks - 1)
    def _():
        sout_ref[...] = st_ref[...]


def _retention(proj, c2, s2, lgrow, gn, s0, nb, n_chunks, rows, row_blk0, last_valid, name,
               shared_init=False):
    body = functools.partial(_retention_body, rows=rows, n_chunks=n_chunks, last_valid=last_valid)
    wb = RET_HG * HEAD_DIM
    cb = wb // 128

    def in_spec(col0):
        return pl.BlockSpec((rows, wb),
                            lambda b, hg, c: (row_blk0 + b * n_chunks + c, col0 // cb + hg))

    st_block = (None, RET_HG, HEAD_DIM, HEAD_DIM)
    st_spec = pl.BlockSpec(st_block, lambda b, hg, c: (b, hg, 0, 0))
    s0_spec = pl.BlockSpec(st_block, lambda b, hg, c: (0, hg, 0, 0)) if shared_init else st_spec
    return pl.pallas_call(
        body,
        grid=(nb, H_B // RET_HG, n_chunks),
        in_specs=[in_spec(COL_QB), in_spec(COL_KB), in_spec(COL_VB), in_spec(COL_GB),
                  pl.BlockSpec((rows, HEAD_DIM), lambda b, hg, c: (c, 0)),
                  pl.BlockSpec((rows, HEAD_DIM), lambda b, hg, c: (c, 0)),
                  pl.BlockSpec((1, wb), lambda b, hg, c: (0, hg)),
                  pl.BlockSpec((1, HEAD_DIM), lambda b, hg, c: (0, 0)),
                  s0_spec],
        out_specs=[pl.BlockSpec((rows, wb), lambda b, hg, c: (b * n_chunks + c, hg)), st_spec],
        out_shape=[jax.ShapeDtypeStruct((nb * n_chunks * rows, W_B), BF16),
                   jax.ShapeDtypeStruct((nb, H_B, HEAD_DIM, HEAD_DIM), F32)],
        scratch_shapes=[pltpu.VMEM((RET_HG, HEAD_DIM, HEAD_DIM), F32)],
        compiler_params=_cparams(("parallel", "parallel", "arbitrary"), VMEM_LIMIT),
        name=name,
    )(proj, proj, proj, proj, c2, s2, lgrow, gn, s0)


CONV_LANES = 256


def _conv_body(ca0_ref, ca1_ref, cb0_ref, cb1_ref, ctx_ref, w_ref, b_ref, lg_ref, lb_ref,
               y_ref, tail_ref, uc_ref, us_ref, cv_ref, *, rows, n_chunks, last_valid):
    c = pl.program_id(1)

    @pl.when(c == 0)
    def _():
        uc_ref[0:CTX_ROWS, :] = ctx_ref[...]

    ca = jnp.concatenate([ca0_ref[...], ca1_ref[...]], axis=1)
    cb = jnp.concatenate([cb0_ref[...], cb1_ref[...]], axis=1)
    uc_ref[CTX_ROWS:CTX_ROWS + rows, :] = ca * _sigmoid(cb)
    span = CTX_ROWS + rows - 8
    for r in range(1, 8):
        us_ref[r] = uc_ref[pl.ds(r, span), :]
    for cc in range(0, W_C, CONV_LANES):
        cs = slice(cc, cc + CONV_LANES)
        acc = jnp.broadcast_to(b_ref[:, cs], (rows, CONV_LANES))
        for w in range(CONV_W):
            off = CTX_ROWS - (CONV_W - 1) + w
            r, a = off % 8, off - off % 8
            src = uc_ref[a:a + rows, cs] if r == 0 else us_ref[r, a:a + rows, cs]
            acc = acc + src * w_ref[w:w + 1, cs]
        cv_ref[:, cs] = acc
    acc = cv_ref[...]
    mu = jnp.mean(acc, axis=-1, keepdims=True)
    xc = acc - mu
    var = jnp.mean(xc * xc, axis=-1, keepdims=True)
    y = xc * lax.rsqrt(var + EPS) * lg_ref[...] + lb_ref[...]
    y_ref[...] = (y * _sigmoid(y)).astype(y_ref.dtype)

    @pl.when(c == n_chunks - 1)
    def _():
        tail_ref[...] = uc_ref[pl.ds(CTX_ROWS + last_valid - (CONV_W - 1), CONV_W - 1), :]

    nxt = uc_ref[rows:rows + CTX_ROWS, :]
    uc_ref[0:CTX_ROWS, :] = nxt


def _conv(proj, ctx0, conv_w, conv_b, ln_g, ln_b, layer, nb, n_chunks, rows, row_blk0,
          last_valid, name, shared_ctx=False):
    body = functools.partial(_conv_body, rows=rows, n_chunks=n_chunks, last_valid=last_valid)
    half = W_C // 2
    hb = half // 128

    def in_spec(col):
        return pl.BlockSpec((rows, half), lambda b, c: (row_blk0 + b * n_chunks + c, col))

    def par_spec():
        return pl.BlockSpec((None, 1, W_C), lambda b, c: (layer, 0, 0))

    return pl.pallas_call(
        body,
        grid=(nb, n_chunks),
        in_specs=[in_spec(COL_CA // hb), in_spec(COL_CA // hb + 1),
                  in_spec(COL_CB // hb), in_spec(COL_CB // hb + 1),
                  pl.BlockSpec((None, CTX_ROWS, W_C),
                               lambda b, c: (0 if shared_ctx else b, 0, 0)),
                  pl.BlockSpec((None, CONV_W, W_C), lambda b, c: (layer, 0, 0)),
                  par_spec(), par_spec(), par_spec()],
        out_specs=[pl.BlockSpec((rows, W_C), lambda b, c: (b * n_chunks + c, 0)),
                   pl.BlockSpec((None, CONV_W - 1, W_C), lambda b, c: (b, 0, 0))],
        out_shape=[jax.ShapeDtypeStruct((nb * n_chunks * rows, W_C), BF16),
                   jax.ShapeDtypeStruct((nb, CONV_W - 1, W_C), F32)],
        scratch_shapes=[pltpu.VMEM((CTX_ROWS + rows, W_C), F32),
                        pltpu.VMEM((8, CTX_ROWS + rows - 8, W_C), F32),
                        pltpu.VMEM((rows, W_C), F32)],
        compiler_params=_cparams(("parallel", "arbitrary"), VMEM_LIMIT),
        name=name,
    )(proj, proj, proj, proj, ctx0, conv_w, conv_b, ln_g, ln_b)


def _t5_bias(rel_bias, dist):
    n = jnp.maximum(dist, 0)
    nf = jnp.maximum(n, 1).astype(F32)
    large = REL_EXACT + (jnp.log(nf / REL_EXACT) / math.log(REL_MAX_DIST / REL_EXACT)
                         * (N_BUCKETS - REL_EXACT)).astype(jnp.int32)
    bucket = jnp.where(n < REL_EXACT, n, jnp.minimum(large, N_BUCKETS - 1))
    onehot = (bucket[..., None] == jnp.arange(N_BUCKETS, dtype=jnp.int32)).astype(F32)
    bias = jnp.einsum("...k,kh->h...", onehot, rel_bias.astype(F32),
                      precision=lax.Precision.HIGHEST)
    return jnp.where(dist[None] >= 0, bias, NEG_INF)


def _rotary_tables(pos):
    inv = ROPE_BASE ** (-jnp.arange(HALF, dtype=F32) / HALF)
    ang = pos.astype(F32)[:, None] * inv[None, :]
    cos, sin = jnp.cos(ang), jnp.sin(ang)
    return jnp.concatenate([cos, cos], axis=1), jnp.concatenate([-sin, sin], axis=1)


def kernel(x_prompt, x_sample, cache_k, cache_v, state_ret, state_conv, page_table, meta, rel_bias,
           norm_mix, w_in, lam_q1, lam_k1, lam_q2, lam_k2, subln_a, gn_b, conv_w, conv_b, conv_ln_g,
           conv_ln_b, w_out, norm_ffn, w_gate, w_up, w_down, norm_final):
    nb, seq = x_prompt.shape[:2]
    db, n_new = x_sample.shape[:2]
    depth = w_in.shape[0]
    n_pages = page_table.shape[1]
    past = n_pages * PAGE_SIZE
    assert seq % BLK == 0 and N_META == SROWS and BLK >= REL_MAX_DIST and PAGE_SIZE >= REL_MAX_DIST
    real = nb * seq
    samp0 = real + BLK
    m_all = samp0 + db * SROWS
    meta_blk = real // SROWS
    s_blk0 = samp0 // SROWS
    d_ff = w_gate.shape[2]

    tm = m_all // 6
    tr = m_all // 24

    h = jnp.concatenate([x_prompt.reshape(real, D_MODEL), meta.astype(F32),
                         jnp.zeros((BLK - N_META, D_MODEL), F32),
                         jnp.pad(x_sample, ((0, 0), (0, SROWS - n_new), (0, 0))
                                 ).reshape(db * SROWS, D_MODEL)], axis=0)

    kk = jnp.arange(BLK, dtype=jnp.int32)[:, None]
    ii = jnp.arange(BLK, dtype=jnp.int32)[None, :]
    is_meta = kk < N_META
    far_d = jnp.full((BLK, BLK), REL_MAX_DIST, jnp.int32)
    band = _t5_bias(rel_bias, jnp.stack([
        far_d, BLK + ii - kk, ii - kk, jnp.full((BLK, BLK), -1, jnp.int32),
        jnp.where(is_meta, N_META + ii - kk, -1), jnp.where(is_meta, far_d, -1)]))
    rm = jnp.arange(2 * N_META, dtype=jnp.int32)[:, None] % N_META
    jm = jnp.arange(BLK, dtype=jnp.int32)[None, :]
    bias_m = _t5_bias(rel_bias, jnp.where(jm < N_META, rm - jm, -1))
    r_new = jnp.arange(SROWS, dtype=jnp.int32) % n_new
    jk = jnp.arange(PAGE_SIZE, dtype=jnp.int32)[None, :]
    dist_s = jnp.stack([jnp.full((SROWS, PAGE_SIZE), REL_MAX_DIST, jnp.int32),
                        PAGE_SIZE + r_new[:, None] - jk,
                        jnp.where(jk < n_new, r_new[:, None] - jk, -1)])
    bias_s = jnp.transpose(_t5_bias(rel_bias, dist_s), (1, 0, 2, 3))
    cache_kh = jnp.transpose(cache_k, (0, 1, 3, 2, 4))
    cache_vh = jnp.transpose(cache_v, (0, 1, 3, 2, 4))
    c2_m, s2_m = _rotary_tables(jnp.arange(N_META, dtype=jnp.int32))
    c2_p, s2_p = _rotary_tables(N_META + jnp.arange(seq, dtype=jnp.int32))
    c2_s, s2_s = _rotary_tables(past + jnp.arange(SROWS, dtype=jnp.int32))
    log_gamma = jnp.log1p(-jnp.exp2(-5.0 - jnp.arange(H_B, dtype=F32)))
    lgrow = jnp.repeat(log_gamma, HEAD_DIM)[None, :]

    zero_state = jnp.zeros((1, H_B, HEAD_DIM, HEAD_DIM), F32)
    zero_ctx = jnp.zeros((1, CTX_ROWS, W_C), F32)
    ctx_pad = ((0, 0), (CTX_ROWS - (CONV_W - 1), 0), (0, 0))

    def small_rows(y_meta, y_sample):
        return jnp.concatenate([y_meta, jnp.zeros((BLK - N_META, y_meta.shape[1]), y_meta.dtype),
                                y_sample], axis=0)

    conv_b3 = conv_b[:, None, :]
    ln_g3 = conv_ln_g[:, None, :]
    ln_b3 = conv_ln_b[:, None, :]

    projs, ret_p, conv_p, k_s, v_s, ret_s, conv_s = [], [], [], [], [], [], []
    for l in range(depth):
        lam_init = 0.8 - 0.6 * math.exp(-0.3 * l)
        lamp = jnp.stack([lam_q1[l], lam_k1[l], lam_q2[l], lam_k2[l]]).astype(F32)
        sub = subln_a[l][None, :]
        gn = gn_b[l][None, :]

        n1 = _rmsnorm(h, norm_mix[l], BF16, tr, f"norm_mix{l}")
        proj = _matmul(n1, w_in, l, tm, 512, f"in_proj{l}")

        ya_m = _attn_meta(proj, bias_m, lamp, sub, meta_blk, lam_init, f"attn_meta{l}")
        yb_m, st_m = _retention(proj, c2_m, s2_m, lgrow, gn, zero_state, 1, 1, N_META, meta_blk,
                                N_META, f"ret_meta{l}")
        yc_m, tail_m = _conv(proj, zero_ctx, conv_w, conv_b3, ln_g3, ln_b3, l, 1, 1, N_META,
                             meta_blk, N_META, f"conv_meta{l}")

        ya_p = _attn_prompt(proj, band, lamp, sub, nb, seq, meta_blk, lam_init, f"attn_prompt{l}")
        ya_s = _attn_sample(page_table, proj, cache_kh, cache_vh, bias_s, lamp, sub, l, s_blk0,
                            n_new, lam_init, f"attn_sample{l}")
        yb_p, st_p = _retention(proj, c2_p, s2_p, lgrow, gn, st_m, nb, seq // BLK, BLK, 0, BLK,
                                f"ret_prompt{l}", shared_init=True)
        yb_s, st_s = _retention(proj, c2_s, s2_s, lgrow, gn, state_ret[l], db, 1, SROWS,
                                s_blk0, n_new, f"ret_sample{l}")
        yc_p, tail_p = _conv(proj, jnp.pad(tail_m, ctx_pad), conv_w, conv_b3, ln_g3, ln_b3, l, nb,
                             seq // BLK, BLK, 0, BLK, f"conv_prompt{l}", shared_ctx=True)
        yc_s, tail_s = _conv(proj, jnp.pad(state_conv[l], ctx_pad), conv_w, conv_b3, ln_g3, ln_b3,
                             l, db, 1, SROWS, s_blk0, n_new, f"conv_sample{l}")

        h = _out_proj(ya_p, yb_p, yc_p, w_out, l, h, 0, real // 8, 512, f"out_proj_p{l}")
        n_small = m_all - real
        h = _out_proj(small_rows(ya_m, ya_s), small_rows(yb_m, yb_s), small_rows(yc_m, yc_s),
                      w_out, l, h, real // n_small, n_small, 256, f"out_proj_s{l}")
        n2 = _rmsnorm(h, norm_ffn[l], BF16, tr, f"norm_ffn{l}")
        hid = _ffn_up(n2, w_gate, w_up, l, tm, 256, f"ffn_up{l}")
        for kb in range(2):
            h = _matmul(hid, w_down, l, tm, 256, f"ffn_down{l}_{kb}", res=h,
                        tk=d_ff // 2, kblk=kb)

        projs.append(proj)
        kvs = proj[samp0:, W_A:3 * W_A].reshape(db, SROWS, 2, H_A, HEAD_DIM)[:, :n_new]
        k_s.append(kvs[:, :, 0])
        v_s.append(kvs[:, :, 1])
        ret_p.append(st_p)
        ret_s.append(st_s)
        conv_p.append(tail_p)
        conv_s.append(tail_s)

    y_prompt = _rmsnorm(h, norm_final, F32, 2 * BLK, "norm_final_p", rows=real).reshape(
        nb, seq, D_MODEL)
    hs = _rmsnorm(h[samp0:], norm_final, F32, db * SROWS, "norm_final_s")
    y_sample = hs.reshape(db, SROWS, D_MODEL)[:, :n_new]
    k_hm, v_hm = _kv_out(projs, nb, seq, meta_blk, "kv_out")
    k_prompt = jnp.transpose(k_hm, (0, 1, 3, 2, 4))
    v_prompt = jnp.transpose(v_hm, (0, 1, 3, 2, 4))
    return (y_prompt, y_sample, k_prompt, v_prompt, jnp.stack(ret_p), jnp.stack(conv_p),
            jnp.stack(k_s), jnp.stack(v_s), jnp.stack(ret_s), jnp.stack(conv_s))
```

```python
import functools
import math

import jax
import jax.numpy as jnp
from jax import lax
from jax.experimental import pallas as pl
from jax.experimental.pallas import tpu as pltpu

F32 = jnp.float32
BF16 = jnp.bfloat16

D_MODEL = 4096
N_META = 16
HEAD_DIM = 128
HALF = HEAD_DIM // 2
H_A = 12
H_B = 12
W_A = H_A * HEAD_DIM
W_B = H_B * HEAD_DIM
W_C = D_MODEL - W_A - W_B
N_IN = 3 * W_A + 4 * W_B + 2 * W_C
CONV_W = 31
N_BUCKETS = 32
REL_EXACT = 16
REL_MAX_DIST = 128
ROPE_BASE = 10000.0
EPS = 1e-6
NEG_INF = -1e30
PAGE_SIZE = 128

BLK = 128
SROWS = 16
CTX_ROWS = 32
M_INIT = -3.0e38
VMEM_LIMIT = 56 * 1024 * 1024

COL_QA, COL_KA, COL_VA = 0, W_A // 128, 2 * W_A // 128
COL_QB = 3 * W_A // 128
COL_KB = COL_QB + W_B // 128
COL_VB = COL_KB + W_B // 128
COL_GB = COL_VB + W_B // 128
COL_CA = COL_GB + W_B // 128
COL_CB = COL_CA + W_C // 128


def _cparams(sem, vmem=None):
    return pltpu.CompilerParams(dimension_semantics=sem, vmem_limit_bytes=vmem)


def _sigmoid(x):
    return 1.0 / (1.0 + jnp.exp(-x))


def _rmsnorm_body(x_ref, g_ref, o_ref):
    x = x_ref[...]
    ms = jnp.mean(x * x, axis=-1, keepdims=True)
    o_ref[...] = (x * lax.rsqrt(ms + EPS) * g_ref[...]).astype(o_ref.dtype)


def _rmsnorm(x, g, out_dtype, tr, name, rows=None):
    m, d = x.shape
    m = m if rows is None else rows
    return pl.pallas_call(
        _rmsnorm_body,
        grid=(m // tr,),
        in_specs=[pl.BlockSpec((tr, d), lambda i: (i, 0)),
                  pl.BlockSpec((1, d), lambda i: (0, 0))],
        out_specs=pl.BlockSpec((tr, d), lambda i: (i, 0)),
        out_shape=jax.ShapeDtypeStruct((m, d), out_dtype),
        compiler_params=_cparams(("parallel",)),
        name=name,
    )(x, g.reshape(1, d))


def _mm_body(a_ref, w_ref, o_ref):
    o_ref[...] = jnp.dot(a_ref[...], w_ref[...].astype(BF16), preferred_element_type=F32)


def _mm_res_body(a_ref, w_ref, r_ref, o_ref):
    o_ref[...] = r_ref[...] + jnp.dot(a_ref[...], w_ref[...].astype(BF16),
                                      preferred_element_type=F32)


def _matmul(a, w, layer, tm, tn, name, res=None, tk=None, kblk=0):
    m = a.shape[0]
    k = a.shape[1] if tk is None else tk
    n = w.shape[2]
    in_specs = [pl.BlockSpec((tm, k), lambda i, j: (i, kblk)),
                pl.BlockSpec((None, k, tn), lambda i, j: (layer, kblk, j))]
    args = [a, w]
    body = _mm_body
    if res is not None:
        in_specs.append(pl.BlockSpec((tm, tn), lambda i, j: (i, j)))
        args.append(res)
        body = _mm_res_body
    return pl.pallas_call(
        body,
        grid=(m // tm, n // tn),
        in_specs=in_specs,
        out_specs=pl.BlockSpec((tm, tn), lambda i, j: (i, j)),
        out_shape=jax.ShapeDtypeStruct((m, n), F32),
        compiler_params=_cparams(("parallel", "parallel"), VMEM_LIMIT),
        name=name,
    )(*args)


def _ffn_up_body(a_ref, wg_ref, wu_ref, o_ref):
    a = a_ref[...]
    g = jnp.dot(a, wg_ref[...].astype(BF16), preferred_element_type=F32)
    u = jnp.dot(a, wu_ref[...].astype(BF16), preferred_element_type=F32)
    o_ref[...] = (g * _sigmoid(g) * u).astype(o_ref.dtype)


def _ffn_up(a, wg, wu, layer, tm, tn, name):
    m, k = a.shape
    n = wg.shape[2]
    wspec = pl.BlockSpec((None, k, tn), lambda i, j: (layer, 0, j))
    return pl.pallas_call(
        _ffn_up_body,
        grid=(m // tm, n // tn),
        in_specs=[pl.BlockSpec((tm, k), lambda i, j: (i, 0)), wspec, wspec],
        out_specs=pl.BlockSpec((tm, tn), lambda i, j: (i, j)),
        out_shape=jax.ShapeDtypeStruct((m, n), BF16),
        compiler_params=_cparams(("parallel", "parallel"), VMEM_LIMIT),
        name=name,
    )(a, wg, wu)


def _out_proj_body(ya_ref, yb_ref, yc_ref, w_ref, r_ref, o_ref):
    acc = jnp.dot(ya_ref[...], w_ref[0:W_A, :].astype(BF16), preferred_element_type=F32)
    acc += jnp.dot(yb_ref[...], w_ref[W_A:W_A + W_B, :].astype(BF16), preferred_element_type=F32)
    acc += jnp.dot(yc_ref[...], w_ref[W_A + W_B:, :].astype(BF16), preferred_element_type=F32)
    o_ref[...] = r_ref[...] + acc


def _out_proj(ya, yb, yc, w, layer, h, row_tile0, tm, tn, name):
    rows = ya.shape[0]
    n = w.shape[2]
    hspec = pl.BlockSpec((tm, tn), lambda i, j: (row_tile0 + i, j))
    return pl.pallas_call(
        _out_proj_body,
        grid=(rows // tm, n // tn),
        in_specs=[pl.BlockSpec((tm, W_A), lambda i, j: (i, 0)),
                  pl.BlockSpec((tm, W_B), lambda i, j: (i, 0)),
                  pl.BlockSpec((tm, W_C), lambda i, j: (i, 0)),
                  pl.BlockSpec((None, W_A + W_B + W_C, tn), lambda i, j: (layer, 0, j)),
                  hspec],
        out_specs=hspec,
        out_shape=jax.ShapeDtypeStruct(h.shape, F32),
        input_output_aliases={4: 0},
        compiler_params=_cparams(("parallel", "parallel"), VMEM_LIMIT),
        name=name,
    )(ya, yb, yc, w, h)


def _kv_out_body(*refs, depth, seq):
    ko_ref, vo_ref = refs[4 * depth:]
    for l in range(depth):
        km_ref, kr_ref, vm_ref, vr_ref = refs[4 * l:4 * l + 4]
        ko_ref[l, 0:N_META] = km_ref[...]
        ko_ref[l, N_META:N_META + seq] = kr_ref[...]
        vo_ref[l, 0:N_META] = vm_ref[...]
        vo_ref[l, N_META:N_META + seq] = vr_ref[...]


def _kv_out(projs, nb, seq, meta_blk, name):
    depth = len(projs)
    t_valid = N_META + seq
    in_specs, args = [], []
    for proj in projs:
        for col in (COL_KA, COL_VA):
            in_specs += [pl.BlockSpec((N_META, HEAD_DIM), lambda b, h, col=col: (meta_blk, col + h)),
                         pl.BlockSpec((seq, HEAD_DIM), lambda b, h, col=col: (b, col + h))]
            args += [proj, proj]
    ospec = pl.BlockSpec((depth, None, None, t_valid, HEAD_DIM), lambda b, h: (0, b, h, 0, 0))
    oshape = jax.ShapeDtypeStruct((depth, nb, H_A, t_valid, HEAD_DIM), F32)
    return pl.pallas_call(
        functools.partial(_kv_out_body, depth=depth, seq=seq),
        grid=(nb, H_A),
        in_specs=in_specs,
        out_specs=[ospec, ospec],
        out_shape=[oshape, oshape],
        compiler_params=_cparams(("parallel", "parallel")),
        name=name,
    )(*args)


def _lambda(lamp_ref, lam_init):
    lp = lamp_ref[...]
    s1 = jnp.sum(lp[0:1] * lp[1:2], axis=-1, keepdims=True)
    s2 = jnp.sum(lp[2:3] * lp[3:4], axis=-1, keepdims=True)
    return jnp.exp(s1) - jnp.exp(s2) + lam_init


def _split_maps(q):
    lane = lax.broadcasted_iota(jnp.int32, q.shape, 1)
    lo = jnp.where(lane < HALF, q, 0.0)
    hi = jnp.where(lane >= HALF, q, 0.0)
    return jnp.concatenate([lo, hi], axis=0)


def _subln(a, sub_ref, lam_init):
    y = a * lax.rsqrt(jnp.mean(a * a, axis=-1, keepdims=True) + EPS) * sub_ref[...]
    return y * (1.0 - lam_init)


ATT_HG = 4


def _attn_prompt_body(q_ref, k_ref, v_ref, km_ref, vm_ref, band_ref, lamp_ref, sub_ref, o_ref,
                      kb_ref, vt_ref, s_ref, p_ref, *, lam_init, nblk):
    lanes = [slice(g * HEAD_DIM, (g + 1) * HEAD_DIM) for g in range(ATT_HG)]

    zpad = jnp.zeros((BLK - N_META, ATT_HG * HEAD_DIM), F32)
    kb_ref[0:BLK, :] = jnp.concatenate([km_ref[...], zpad], axis=0).astype(BF16)
    kb_ref[BLK:, :] = k_ref[...].astype(BF16)
    vm = jnp.concatenate([vm_ref[...], zpad], axis=0)
    for g in range(ATT_HG):
        vt_ref[g, :, 0:BLK] = vm[:, lanes[g]].T.astype(BF16)
        for j in range(1, nblk):
            vt = v_ref[(j - 1) * BLK:j * BLK, lanes[g]].T
            vt_ref[g, :, j * BLK:(j + 1) * BLK] = vt.astype(BF16)

    def attend(qi, nt):
        rows = pl.ds(pl.multiple_of(qi * BLK, BLK), BLK)
        lam = _lambda(lamp_ref, lam_init)
        mrun = [None] * ATT_HG
        lrun = [None] * ATT_HG
        m = [None] * ATT_HG
        qst = [None] * ATT_HG

        def score_tile(g, j):
            if j == 0:
                qst[g] = _split_maps(q_ref[rows, lanes[g]] * (HALF ** -0.5)).T.astype(BF16)
            if j == 0:
                bt = band_ref[g, jnp.where(qi == 0, 4, 5)]
            else:
                bt = band_ref[g, jnp.clip(j - qi + 1, 0, 3)]
            s = jnp.dot(kb_ref[j * BLK:(j + 1) * BLK, lanes[g]], qst[g],
                        preferred_element_type=F32)
            s = s + jnp.concatenate([bt, bt], axis=1)
            s_ref[g, j] = s
            mrun[g] = s if mrun[g] is None else jnp.maximum(mrun[g], s)
            if j == nt - 1:
                m[g] = jnp.max(mrun[g], axis=0, keepdims=True)

        def exp_tile(g, j):
            p = jnp.exp(s_ref[g, j] - m[g])
            lrun[g] = p if lrun[g] is None else lrun[g] + p
            p_ref[g, j * BLK:(j + 1) * BLK, :] = p.astype(BF16)
            if j == nt - 1:
                l = jnp.sum(lrun[g], axis=0, keepdims=True)
                ot = jnp.dot(vt_ref[g, :, 0:nt * BLK], p_ref[g, 0:nt * BLK, :],
                             preferred_element_type=F32) / l
                at = ot[:, :BLK] - lam * ot[:, BLK:]
                at = at * lax.rsqrt(jnp.mean(at * at, axis=0, keepdims=True) + EPS)
                o_ref[rows, lanes[g]] = (at.T * sub_ref[...] * (1.0 - lam_init)
                                         ).astype(o_ref.dtype)

        for step in range(ATT_HG + 1):
            for j in range(nt):
                if step < ATT_HG:
                    score_tile(step, j)
                if step >= 1:
                    exp_tile(step - 1, j)

    def query_tile(qi, carry):
        lo = 0
        for hi in sorted(set(range(2, nblk, 3)) | {nblk}):
            pl.when(jnp.logical_and(qi + 2 > lo, qi + 2 <= hi))(functools.partial(attend, qi, hi))
            lo = hi
        return carry

    lax.fori_loop(0, nblk - 1, query_tile, 0)


def _attn_prompt(proj, band, lamp, sub, nb, seq, meta_blk, lam_init, name):
    nq = seq // BLK
    nblk = nq + 1
    tp = nblk * BLK
    wg = ATT_HG * HEAD_DIM
    body = functools.partial(_attn_prompt_body, lam_init=lam_init, nblk=nblk)
    return pl.pallas_call(
        body,
        grid=(nb, H_A // ATT_HG),
        in_specs=[
            pl.BlockSpec((seq, wg), lambda b, h: (b, COL_QA // ATT_HG + h)),
            pl.BlockSpec((seq, wg), lambda b, h: (b, COL_KA // ATT_HG + h)),
            pl.BlockSpec((seq, wg), lambda b, h: (b, COL_VA // ATT_HG + h)),
            pl.BlockSpec((N_META, wg), lambda b, h: (meta_blk, COL_KA // ATT_HG + h)),
            pl.BlockSpec((N_META, wg), lambda b, h: (meta_blk, COL_VA // ATT_HG + h)),
            pl.BlockSpec((ATT_HG, 6, BLK, BLK), lambda b, h: (h, 0, 0, 0)),
            pl.BlockSpec((4, HALF), lambda b, h: (0, 0)),
            pl.BlockSpec((1, HEAD_DIM), lambda b, h: (0, 0)),
        ],
        out_specs=pl.BlockSpec((seq, wg), lambda b, h: (b, h)),
        out_shape=jax.ShapeDtypeStruct((nb * seq, W_A), BF16),
        scratch_shapes=[pltpu.VMEM((tp, wg), BF16), pltpu.VMEM((ATT_HG, HEAD_DIM, tp), BF16),
                        pltpu.VMEM((ATT_HG, nblk, BLK, 2 * BLK), F32),
                        pltpu.VMEM((ATT_HG, tp, 2 * BLK), BF16)],
        compiler_params=_cparams(("parallel", "parallel"), VMEM_LIMIT),
        name=name,
    )(proj, proj, proj, proj, proj, band, lamp, sub)


def _attn_meta_body(q_ref, k_ref, v_ref, bias_ref, lamp_ref, sub_ref, o_ref, *, lam_init):
    zpad = jnp.zeros((BLK - N_META, HEAD_DIM), F32)
    lam = _lambda(lamp_ref, lam_init)
    for h in range(H_A):
        sl = slice(h * HEAD_DIM, (h + 1) * HEAD_DIM)
        qs = _split_maps(q_ref[:, sl] * (HALF ** -0.5)).astype(BF16)
        kp = jnp.concatenate([k_ref[:, sl], zpad], axis=0).astype(BF16)
        vp = jnp.concatenate([v_ref[:, sl], zpad], axis=0).astype(BF16)
        s = lax.dot_general(qs, kp, (((1,), (1,)), ((), ())),
                            preferred_element_type=F32) + bias_ref[h]
        p = jnp.exp(s - jnp.max(s, axis=1, keepdims=True))
        o = jnp.dot(p.astype(BF16), vp, preferred_element_type=F32) / jnp.sum(p, axis=1,
                                                                              keepdims=True)
        a = o[0:N_META] - lam * o[N_META:]
        o_ref[:, sl] = _subln(a, sub_ref, lam_init).astype(o_ref.dtype)


def _attn_meta(proj, bias, lamp, sub, meta_blk, lam_init, name):
    def spec(col):
        return pl.BlockSpec((N_META, W_A), lambda i: (meta_blk, col))

    return pl.pallas_call(
        functools.partial(_attn_meta_body, lam_init=lam_init),
        grid=(1,),
        in_specs=[spec(0), spec(1), spec(2),
                  pl.BlockSpec((H_A, 2 * N_META, BLK), lambda i: (0, 0, 0)),
                  pl.BlockSpec((4, HALF), lambda i: (0, 0)),
                  pl.BlockSpec((1, HEAD_DIM), lambda i: (0, 0))],
        out_specs=pl.BlockSpec((N_META, W_A), lambda i: (0, 0)),
        out_shape=jax.ShapeDtypeStruct((N_META, W_A), BF16),
        compiler_params=_cparams(("arbitrary",)),
        name=name,
    )(proj, proj, proj, bias, lamp, sub)


PPS = 8


def _attn_sample_body(pt_ref, q_ref, kn_ref, vn_ref, *rest, lam_init, n_pages, n_new):
    kp_refs, vp_refs = rest[:PPS], rest[PPS:2 * PPS]
    bias_ref, lamp_ref, sub_ref, o_ref, qs_ref, m_ref, l_ref, acc_ref = rest[2 * PPS:]
    p = pl.program_id(1)
    n_steps = n_pages // PPS

    @pl.when(p == 0)
    def _():
        for h in range(H_A):
            q = q_ref[0:n_new, h * HEAD_DIM:(h + 1) * HEAD_DIM] * (HALF ** -0.5)
            qs = _split_maps(q)
            pad = jnp.zeros((SROWS - 2 * n_new, HEAD_DIM), F32)
            qs_ref[h] = jnp.concatenate([qs, pad], axis=0).astype(BF16)
        m_ref[...] = jnp.full(m_ref.shape, M_INIT, F32)
        l_ref[...] = jnp.zeros(l_ref.shape, F32)
        acc_ref[...] = jnp.zeros(acc_ref.shape, F32)

    def update(get_k, get_v, bias, n_rep):
        s = jnp.stack([lax.dot_general(qs_ref[h], get_k(h).astype(BF16), (((1,), (1,)), ((), ())),
                                       preferred_element_type=F32) for h in range(H_A)])
        s = s + bias
        m_prev = m_ref[...]
        m_new = jnp.maximum(m_prev, jnp.max(s, axis=-1, keepdims=True))
        alpha = jnp.exp(m_prev - m_new)
        pm = jnp.exp(s - jnp.concatenate([m_new] * n_rep, axis=-1))
        l_ref[...] = alpha * l_ref[...] + jnp.sum(pm, axis=-1, keepdims=True)
        m_ref[...] = m_new
        pb = pm.astype(BF16)
        for h in range(H_A):
            acc_ref[h] = alpha[h] * acc_ref[h] + jnp.dot(pb[h], get_v(h).astype(BF16),
                                                         preferred_element_type=F32)

    @pl.when(p < n_steps)
    def _():
        def pages(refs, h):
            return jnp.concatenate([r[h] for r in refs], axis=0)

        last_idx = jnp.where(p == n_steps - 1, 1, 0)
        bias = jnp.concatenate([bias_ref[0]] * (PPS - 1) + [bias_ref[last_idx]], axis=-1)
        update(functools.partial(pages, kp_refs), functools.partial(pages, vp_refs), bias, PPS)

    @pl.when(p == n_steps)
    def _():
        zpad = jnp.zeros((PAGE_SIZE - SROWS, HEAD_DIM), F32)

        def new_rows(ref, h):
            return jnp.concatenate([ref[:, h * HEAD_DIM:(h + 1) * HEAD_DIM], zpad], axis=0)

        update(functools.partial(new_rows, kn_ref), functools.partial(new_rows, vn_ref),
               bias_ref[2], 1)
        lam = _lambda(lamp_ref, lam_init)
        for h in range(H_A):
            o = acc_ref[h] / l_ref[h]
            a = o[0:n_new] - lam * o[n_new:2 * n_new]
            y = _subln(a, sub_ref, lam_init)
            pad = jnp.zeros((SROWS - n_new, HEAD_DIM), F32)
            o_ref[:, h * HEAD_DIM:(h + 1) * HEAD_DIM] = (
                jnp.concatenate([y, pad], axis=0).astype(o_ref.dtype))


def _attn_sample(page_table, proj, cache_k, cache_v, bias, lamp, sub, layer, row_blk0, n_new,
                 lam_init, name):
    db, n_pages = page_table.shape
    assert n_pages % PPS == 0
    body = functools.partial(_attn_sample_body, lam_init=lam_init, n_pages=n_pages, n_new=n_new)

    def page_spec(t):
        return pl.BlockSpec(
            (None, None, H_A, PAGE_SIZE, HEAD_DIM),
            lambda b, p, pt: (layer, pt[b, jnp.minimum(p * PPS + t, n_pages - 1)], 0, 0, 0))

    page_specs = [page_spec(t) for t in range(PPS)]
    grid_spec = pltpu.PrefetchScalarGridSpec(
        num_scalar_prefetch=1,
        grid=(db, n_pages // PPS + 1),
        in_specs=[
            pl.BlockSpec((SROWS, W_A), lambda b, p, pt: (row_blk0 + b, 0)),
            pl.BlockSpec((SROWS, W_A), lambda b, p, pt: (row_blk0 + b, 1)),
            pl.BlockSpec((SROWS, W_A), lambda b, p, pt: (row_blk0 + b, 2)),
            *page_specs, *page_specs,
            pl.BlockSpec((3, H_A, SROWS, PAGE_SIZE), lambda b, p, pt: (0, 0, 0, 0)),
            pl.BlockSpec((4, HALF), lambda b, p, pt: (0, 0)),
            pl.BlockSpec((1, HEAD_DIM), lambda b, p, pt: (0, 0)),
        ],
        out_specs=pl.BlockSpec((SROWS, W_A), lambda b, p, pt: (b, 0)),
        scratch_shapes=[pltpu.VMEM((H_A, SROWS, HEAD_DIM), BF16),
                        pltpu.VMEM((H_A, SROWS, HEAD_DIM), F32),
                        pltpu.VMEM((H_A, SROWS, HEAD_DIM), F32),
                        pltpu.VMEM((H_A, SROWS, HEAD_DIM), F32)],
    )
    return pl.pallas_call(
        body,
        grid_spec=grid_spec,
        out_shape=jax.ShapeDtypeStruct((db * SROWS, W_A), BF16),
        compiler_params=_cparams(("parallel", "arbitrary"), VMEM_LIMIT),
        name=name,
    )(page_table, proj, proj, proj, *([cache_k] * PPS), *([cache_v] * PPS), bias, lamp, sub)


RET_HG = 12


def _retention_body(q_ref, k_ref, v_ref, g_ref, c2_ref, s2_ref, lg_ref, gn_ref, s0_ref,
                    y_ref, sout_ref, st_ref, *, rows, n_chunks, last_valid):
    c = pl.program_id(2)

    @pl.when(c == 0)
    def _():
        st_ref[...] = s0_ref[...]

    lc = jnp.where(c == n_chunks - 1, float(last_valid), float(BLK))
    ii = lax.broadcasted_iota(jnp.int32, (BLK, BLK), 0).astype(F32)
    jj = lax.broadcasted_iota(jnp.int32, (BLK, BLK), 1).astype(F32)
    rel = ii - jj

    def rows128(x):
        if rows == BLK:
            return x
        return jnp.concatenate([x, jnp.zeros((BLK - rows, x.shape[1]), x.dtype)], axis=0)

    c2 = rows128(c2_ref[...])
    s2 = rows128(s2_ref[...])

    def rot(x):
        return x * c2 + pltpu.roll(x, HALF, 1) * s2

    heads = []
    for j in range(RET_HG):
        sl = slice(j * HEAD_DIM, (j + 1) * HEAD_DIM)
        lgv = lg_ref[:, sl]
        v = rows128(v_ref[:, sl]).astype(BF16)
        qr = rot(rows128(q_ref[:, sl])).astype(BF16)
        kr = rot(rows128(k_ref[:, sl])) * (HEAD_DIM ** -0.5)
        st = st_ref[j]
        sc = lax.dot_general(qr, kr.astype(BF16), (((1,), (1,)), ((), ())),
                             preferred_element_type=F32)
        cross = jnp.dot(qr, st.astype(BF16), preferred_element_type=F32)
        kdec = kr * jnp.where(ii < lc, jnp.exp((lc - 1.0 - ii) * lgv), 0.0)
        st_ref[j] = jnp.exp(lc * lgv) * st + jnp.dot(kdec.T.astype(BF16), v,
                                                     preferred_element_type=F32)
        heads.append((sl, lgv, v, sc, cross))
    for sl, lgv, v, sc, cross in heads:
        decay = jnp.where(rel >= 0, jnp.exp(lgv * jnp.maximum(rel, 0.0)), 0.0)
        o = jnp.dot((sc * decay).astype(BF16), v, preferred_element_type=F32)
        o = o + cross * jnp.exp((ii + 1.0) * lgv)
        y = o[:rows]
        y = y * lax.rsqrt(jnp.mean(y * y, axis=-1, keepdims=True) + EPS) * gn_ref[...]
        g = g_ref[:, sl]
        y_ref[:, sl] = (y * (g * _sigmoid(g))).astype(y_ref.dtype)

    @pl.when(c == n_chunks - 1)
    def _():
        sout_ref[...] = st_ref[...]


def _retention(proj, c2, s2, lgrow, gn, s0, nb, n_chunks, rows, row_blk0, last_valid, name,
               shared_init=False):
    body = functools.partial(_retention_body, rows=rows, n_chunks=n_chunks, last_valid=last_valid)
    wb = RET_HG * HEAD_DIM
    cb = wb // 128

    def in_spec(col0):
        return pl.BlockSpec((rows, wb),
                            lambda b, hg, c: (row_blk0 + b * n_chunks + c, col0 // cb + hg))

    st_block = (None, RET_HG, HEAD_DIM, HEAD_DIM)
    st_spec = pl.BlockSpec(st_block, lambda b, hg, c: (b, hg, 0, 0))
    s0_spec = pl.BlockSpec(st_block, lambda b, hg, c: (0, hg, 0, 0)) if shared_init else st_spec
    return pl.pallas_call(
        body,
        grid=(nb, H_B // RET_HG, n_chunks),
        in_specs=[in_spec(COL_QB), in_spec(COL_KB), in_spec(COL_VB), in_spec(COL_GB),
                  pl.BlockSpec((rows, HEAD_DIM), lambda b, hg, c: (c, 0)),
                  pl.BlockSpec((rows, HEAD_DIM), lambda b, hg, c: (c, 0)),
                  pl.BlockSpec((1, wb), lambda b, hg, c: (0, hg)),
                  pl.BlockSpec((1, HEAD_DIM), lambda b, hg, c: (0, 0)),
                  s0_spec],
        out_specs=[pl.BlockSpec((rows, wb), lambda b, hg, c: (b * n_chunks + c, hg)), st_spec],
        out_shape=[jax.ShapeDtypeStruct((nb * n_chunks * rows, W_B), BF16),
                   jax.ShapeDtypeStruct((nb, H_B, HEAD_DIM, HEAD_DIM), F32)],
        scratch_shapes=[pltpu.VMEM((RET_HG, HEAD_DIM, HEAD_DIM), F32)],
        compiler_params=_cparams(("parallel", "parallel", "arbitrary"), VMEM_LIMIT),
        name=name,
    )(proj, proj, proj, proj, c2, s2, lgrow, gn, s0)


CONV_LANES = 256


def _conv_body(ca0_ref, ca1_ref, cb0_ref, cb1_ref, ctx_ref, w_ref, b_ref, lg_ref, lb_ref,
               y_ref, tail_ref, uc_ref, us_ref, cv_ref, *, rows, n_chunks, last_valid):
    c = pl.program_id(1)

    @pl.when(c == 0)
    def _():
        uc_ref[0:CTX_ROWS, :] = ctx_ref[...]

    ca = jnp.concatenate([ca0_ref[...], ca1_ref[...]], axis=1)
    cb = jnp.concatenate([cb0_ref[...], cb1_ref[...]], axis=1)
    uc_ref[CTX_ROWS:CTX_ROWS + rows, :] = ca * _sigmoid(cb)
    span = CTX_ROWS + rows - 8
    for r in range(1, 8):
        us_ref[r] = uc_ref[pl.ds(r, span), :]
    for cc in range(0, W_C, CONV_LANES):
        cs = slice(cc, cc + CONV_LANES)
        acc = jnp.broadcast_to(b_ref[:, cs], (rows, CONV_LANES))
        for w in range(CONV_W):
            off = CTX_ROWS - (CONV_W - 1) + w
            r, a = off % 8, off - off % 8
            src = uc_ref[a:a + rows, cs] if r == 0 else us_ref[r, a:a + rows, cs]
            acc = acc + src * w_ref[w:w + 1, cs]
        cv_ref[:, cs] = acc
    acc = cv_ref[...]
    mu = jnp.mean(acc, axis=-1, keepdims=True)
    xc = acc - mu
    var = jnp.mean(xc * xc, axis=-1, keepdims=True)
    y = xc * lax.rsqrt(var + EPS) * lg_ref[...] + lb_ref[...]
    y_ref[...] = (y * _sigmoid(y)).astype(y_ref.dtype)

    @pl.when(c == n_chunks - 1)
    def _():
        tail_ref[...] = uc_ref[pl.ds(CTX_ROWS + last_valid - (CONV_W - 1), CONV_W - 1), :]

    nxt = uc_ref[rows:rows + CTX_ROWS, :]
    uc_ref[0:CTX_ROWS, :] = nxt


def _conv(proj, ctx0, conv_w, conv_b, ln_g, ln_b, layer, nb, n_chunks, rows, row_blk0,
          last_valid, name, shared_ctx=False):
    body = functools.partial(_conv_body, rows=rows, n_chunks=n_chunks, last_valid=last_valid)
    half = W_C // 2
    hb = half // 128

    def in_spec(col):
        return pl.BlockSpec((rows, half), lambda b, c: (row_blk0 + b * n_chunks + c, col))

    def par_spec():
        return pl.BlockSpec((None, 1, W_C), lambda b, c: (layer, 0, 0))

    return pl.pallas_call(
        body,
        grid=(nb, n_chunks),
        in_specs=[in_spec(COL_CA // hb), in_spec(COL_CA // hb + 1),
                  in_spec(COL_CB // hb), in_spec(COL_CB // hb + 1),
                  pl.BlockSpec((None, CTX_ROWS, W_C),
                               lambda b, c: (0 if shared_ctx else b, 0, 0)),
                  pl.BlockSpec((None, CONV_W, W_C), lambda b, c: (layer, 0, 0)),
                  par_spec(), par_spec(), par_spec()],
        out_specs=[pl.BlockSpec((rows, W_C), lambda b, c: (b * n_chunks + c, 0)),
                   pl.BlockSpec((None, CONV_W - 1, W_C), lambda b, c: (b, 0, 0))],
        out_shape=[jax.ShapeDtypeStruct((nb * n_chunks * rows, W_C), BF16),
                   jax.ShapeDtypeStruct((nb, CONV_W - 1, W_C), F32)],
        scratch_shapes=[pltpu.VMEM((CTX_ROWS + rows, W_C), F32),
                        pltpu.VMEM((8, CTX_ROWS + rows - 8, W_C), F32),
                        pltpu.VMEM((rows, W_C), F32)],
        compiler_params=_cparams(("parallel", "arbitrary"), VMEM_LIMIT),
        name=name,
    )(proj, proj, proj, proj, ctx0, conv_w, conv_b, ln_g, ln_b)


def _t5_bias(rel_bias, dist):
    n = jnp.maximum(dist, 0)
    nf = jnp.maximum(n, 1).astype(F32)
    large = REL_EXACT + (jnp.log(nf / REL_EXACT) / math.log(REL_MAX_DIST / REL_EXACT)
                         * (N_BUCKETS - REL_EXACT)).astype(jnp.int32)
    bucket = jnp.where(n < REL_EXACT, n, jnp.minimum(large, N_BUCKETS - 1))
    onehot = (bucket[..., None] == jnp.arange(N_BUCKETS, dtype=jnp.int32)).astype(F32)
    bias = jnp.einsum("...k,kh->h...", onehot, rel_bias.astype(F32),
                      precision=lax.Precision.HIGHEST)
    return jnp.where(dist[None] >= 0, bias, NEG_INF)


def _rotary_tables(pos):
    inv = ROPE_BASE ** (-jnp.arange(HALF, dtype=F32) / HALF)
    ang = pos.astype(F32)[:, None] * inv[None, :]
    cos, sin = jnp.cos(ang), jnp.sin(ang)
    return jnp.concatenate([cos, cos], axis=1), jnp.concatenate([-sin, sin], axis=1)


def kernel(x_prompt, x_sample, cache_k, cache_v, state_ret, state_conv, page_table, meta, rel_bias,
           norm_mix, w_in, lam_q1, lam_k1, lam_q2, lam_k2, subln_a, gn_b, conv_w, conv_b, conv_ln_g,
           conv_ln_b, w_out, norm_ffn, w_gate, w_up, w_down, norm_final):
    nb, seq = x_prompt.shape[:2]
    db, n_new = x_sample.shape[:2]
    depth = w_in.shape[0]
    n_pages = page_table.shape[1]
    past = n_pages * PAGE_SIZE
    assert seq % BLK == 0 and N_META == SROWS and BLK >= REL_MAX_DIST and PAGE_SIZE >= REL_MAX_DIST
    real = nb * seq
    samp0 = real + BLK
    m_all = samp0 + db * SROWS
    meta_blk = real // SROWS
    s_blk0 = samp0 // SROWS
    d_ff = w_gate.shape[2]

    tm = m_all // 6
    tr = m_all // 24

    h = jnp.concatenate([x_prompt.reshape(real, D_MODEL), meta.astype(F32),
                         jnp.zeros((BLK - N_META, D_MODEL), F32),
                         jnp.pad(x_sample, ((0, 0), (0, SROWS - n_new), (0, 0))
                                 ).reshape(db * SROWS, D_MODEL)], axis=0)

    kk = jnp.arange(BLK, dtype=jnp.int32)[:, None]
    ii = jnp.arange(BLK, dtype=jnp.int32)[None, :]
    is_meta = kk < N_META
    far_d = jnp.full((BLK, BLK), REL_MAX_DIST, jnp.int32)
    band = _t5_bias(rel_bias, jnp.stack([
        far_d, BLK + ii - kk, ii - kk, jnp.full((BLK, BLK), -1, jnp.int32),
        jnp.where(is_meta, N_META + ii - kk, -1), jnp.where(is_meta, far_d, -1)]))
    rm = jnp.arange(2 * N_META, dtype=jnp.int32)[:, None] % N_META
    jm = jnp.arange(BLK, dtype=jnp.int32)[None, :]
    bias_m = _t5_bias(rel_bias, jnp.where(jm < N_META, rm - jm, -1))
    r_new = jnp.arange(SROWS, dtype=jnp.int32) % n_new
    jk = jnp.arange(PAGE_SIZE, dtype=jnp.int32)[None, :]
    dist_s = jnp.stack([jnp.full((SROWS, PAGE_SIZE), REL_MAX_DIST, jnp.int32),
                        PAGE_SIZE + r_new[:, None] - jk,
                        jnp.where(jk < n_new, r_new[:, None] - jk, -1)])
    bias_s = jnp.transpose(_t5_bias(rel_bias, dist_s), (1, 0, 2, 3))
    cache_kh = jnp.transpose(cache_k, (0, 1, 3, 2, 4))
    cache_vh = jnp.transpose(cache_v, (0, 1, 3, 2, 4))
    c2_m, s2_m = _rotary_tables(jnp.arange(N_META, dtype=jnp.int32))
    c2_p, s2_p = _rotary_tables(N_META + jnp.arange(seq, dtype=jnp.int32))
    c2_s, s2_s = _rotary_tables(past + jnp.arange(SROWS, dtype=jnp.int32))
    log_gamma = jnp.log1p(-jnp.exp2(-5.0 - jnp.arange(H_B, dtype=F32)))
    lgrow = jnp.repeat(log_gamma, HEAD_DIM)[None, :]

    zero_state = jnp.zeros((1, H_B, HEAD_DIM, HEAD_DIM), F32)
    zero_ctx = jnp.zeros((1, CTX_ROWS, W_C), F32)
    ctx_pad = ((0, 0), (CTX_ROWS - (CONV_W - 1), 0), (0, 0))

    def small_rows(y_meta, y_sample):
        return jnp.concatenate([y_meta, jnp.zeros((BLK - N_META, y_meta.shape[1]), y_meta.dtype),
                                y_sample], axis=0)

    conv_b3 = conv_b[:, None, :]
    ln_g3 = conv_ln_g[:, None, :]
    ln_b3 = conv_ln_b[:, None, :]

    projs, ret_p, conv_p, k_s, v_s, ret_s, conv_s = [], [], [], [], [], [], []
    for l in range(depth):
        lam_init = 0.8 - 0.6 * math.exp(-0.3 * l)
        lamp = jnp.stack([lam_q1[l], lam_k1[l], lam_q2[l], lam_k2[l]]).astype(F32)
        sub = subln_a[l][None, :]
        gn = gn_b[l][None, :]

        n1 = _rmsnorm(h, norm_mix[l], BF16, tr, f"norm_mix{l}")
        proj = _matmul(n1, w_in, l, tm, 512, f"in_proj{l}")

        ya_m = _attn_meta(proj, bias_m, lamp, sub, meta_blk, lam_init, f"attn_meta{l}")
        yb_m, st_m = _retention(proj, c2_m, s2_m, lgrow, gn, zero_state, 1, 1, N_META, meta_blk,
                                N_META, f"ret_meta{l}")
        yc_m, tail_m = _conv(proj, zero_ctx, conv_w, conv_b3, ln_g3, ln_b3, l, 1, 1, N_META,
                             meta_blk, N_META, f"conv_meta{l}")

        ya_p = _attn_prompt(proj, band, lamp, sub, nb, seq, meta_blk, lam_init, f"attn_prompt{l}")
        ya_s = _attn_sample(page_table, proj, cache_kh, cache_vh, bias_s, lamp, sub, l, s_blk0,
                            n_new, lam_init, f"attn_sample{l}")
        yb_p, st_p = _retention(proj, c2_p, s2_p, lgrow, gn, st_m, nb, seq // BLK, BLK, 0, BLK,
                                f"ret_prompt{l}", shared_init=True)
        yb_s, st_s = _retention(proj, c2_s, s2_s, lgrow, gn, state_ret[l], db, 1, SROWS,
                                s_blk0, n_new, f"ret_sample{l}")
        yc_p, tail_p = _conv(proj, jnp.pad(tail_m, ctx_pad), conv_w, conv_b3, ln_g3, ln_b3, l, nb,
                             seq // BLK, BLK, 0, BLK, f"conv_prompt{l}", shared_ctx=True)
        yc_s, tail_s = _conv(proj, jnp.pad(state_conv[l], ctx_pad), conv_w, conv_b3, ln_g3, ln_b3,
                             l, db, 1, SROWS, s_blk0, n_new, f"conv_sample{l}")

        h = _out_proj(ya_p, yb_p, yc_p, w_out, l, h, 0, real // 8, 512, f"out_proj_p{l}")
        n_small = m_all - real
        h = _out_proj(small_rows(ya_m, ya_s), small_rows(yb_m, yb_s), small_rows(yc_m, yc_s),
                      w_out, l, h, real // n_small, n_small, 256, f"out_proj_s{l}")
        n2 = _rmsnorm(h, norm_ffn[l], BF16, tr, f"norm_ffn{l}")
        hid = _ffn_up(n2, w_gate, w_up, l, tm, 256, f"ffn_up{l}")
        for kb in range(2):
            h = _matmul(hid, w_down, l, tm, 256, f"ffn_down{l}_{kb}", res=h,
                        tk=d_ff // 2, kblk=kb)

        projs.append(proj)
        kvs = proj[samp0:, W_A:3 * W_A].reshape(db, SROWS, 2, H_A, HEAD_DIM)[:, :n_new]
        k_s.append(kvs[:, :, 0])
        v_s.append(kvs[:, :, 1])
        ret_p.append(st_p)
        ret_s.append(st_s)
        conv_p.append(tail_p)
        conv_s.append(tail_s)

    y_prompt = _rmsnorm(h, norm_final, F32, 2 * BLK, "norm_final_p", rows=real).reshape(
        nb, seq, D_MODEL)
    hs = _rmsnorm(h[samp0:], norm_final, F32, db * SROWS, "norm_final_s")
    y_sample = hs.reshape(db, SROWS, D_MODEL)[:, :n_new]
    k_hm, v_hm = _kv_out(projs, nb, seq, meta_blk, "kv_out")
    k_prompt = jnp.transpose(k_hm, (0, 1, 3, 2, 4))
    v_prompt = jnp.transpose(v_hm, (0, 1, 3, 2, 4))
    return (y_prompt, y_sample, k_prompt, v_prompt, jnp.stack(ret_p), jnp.stack(conv_p),
            jnp.stack(k_s), jnp.stack(v_s), jnp.stack(ret_s), jnp.stack(conv_s))
```

```python
import functools
import math

import jax
import jax.numpy as jnp
from jax import lax
from jax.experimental import pallas as pl
from jax.experimental.pallas import tpu as pltpu

F32 = jnp.float32
BF16 = jnp.bfloat16

D_MODEL = 4096
N_META = 16
HEAD_DIM = 128
HALF = HEAD_DIM // 2
H_A = 12
H_B = 12
W_A = H_A * HEAD_DIM
W_B = H_B * HEAD_DIM
W_C = D_MODEL - W_A - W_B
N_IN = 3 * W_A + 4 * W_B + 2 * W_C
CONV_W = 31
N_BUCKETS = 32
REL_EXACT = 16
REL_MAX_DIST = 128
ROPE_BASE = 10000.0
EPS = 1e-6
NEG_INF = -1e30
PAGE_SIZE = 128

BLK = 128
SROWS = 16
CTX_ROWS = 32
M_INIT = -3.0e38
VMEM_LIMIT = 56 * 1024 * 1024

COL_QA, COL_KA, COL_VA = 0, W_A // 128, 2 * W_A // 128
COL_QB = 3 * W_A // 128
COL_KB = COL_QB + W_B // 128
COL_VB = COL_KB + W_B // 128
COL_GB = COL_VB + W_B // 128
COL_CA = COL_GB + W_B // 128
COL_CB = COL_CA + W_C // 128


def _cparams(sem, vmem=None):
    return pltpu.CompilerParams(dimension_semantics=sem, vmem_limit_bytes=vmem)


def _sigmoid(x):
    return 1.0 / (1.0 + jnp.exp(-x))


def _rmsnorm_body(x_ref, g_ref, o_ref):
    x = x_ref[...]
    ms = jnp.mean(x * x, axis=-1, keepdims=True)
    o_ref[...] = (x * lax.rsqrt(ms + EPS) * g_ref[...]).astype(o_ref.dtype)


def _rmsnorm(x, g, out_dtype, tr, name, rows=None):
    m, d = x.shape
    m = m if rows is None else rows
    return pl.pallas_call(
        _rmsnorm_body,
        grid=(m // tr,),
        in_specs=[pl.BlockSpec((tr, d), lambda i: (i, 0)),
                  pl.BlockSpec((1, d), lambda i: (0, 0))],
        out_specs=pl.BlockSpec((tr, d), lambda i: (i, 0)),
        out_shape=jax.ShapeDtypeStruct((m, d), out_dtype),
        compiler_params=_cparams(("parallel",)),
        name=name,
    )(x, g.reshape(1, d))


def _mm_body(a_ref, w_ref, o_ref):
    o_ref[...] = jnp.dot(a_ref[...], w_ref[...].astype(BF16), preferred_element_type=F32)


def _mm_res_body(a_ref, w_ref, r_ref, o_ref):
    o_ref[...] = r_ref[...] + jnp.dot(a_ref[...], w_ref[...].astype(BF16),
                                      preferred_element_type=F32)


def _matmul(a, w, layer, tm, tn, name, res=None, tk=None, kblk=0):
    m = a.shape[0]
    k = a.shape[1] if tk is None else tk
    n = w.shape[2]
    in_specs = [pl.BlockSpec((tm, k), lambda i, j: (i, kblk)),
                pl.BlockSpec((None, k, tn), lambda i, j: (layer, kblk, j))]
    args = [a, w]
    body = _mm_body
    if res is not None:
        in_specs.append(pl.BlockSpec((tm, tn), lambda i, j: (i, j)))
        args.append(res)
        body = _mm_res_body
    return pl.pallas_call(
        body,
        grid=(m // tm, n // tn),
        in_specs=in_specs,
        out_specs=pl.BlockSpec((tm, tn), lambda i, j: (i, j)),
        out_shape=jax.ShapeDtypeStruct((m, n), F32),
        compiler_params=_cparams(("parallel", "parallel"), VMEM_LIMIT),
        name=name,
    )(*args)


def _ffn_up_body(a_ref, wg_ref, wu_ref, o_ref):
    a = a_ref[...]
    g = jnp.dot(a, wg_ref[...].astype(BF16), preferred_element_type=F32)
    u = jnp.dot(a, wu_ref[...].astype(BF16), preferred_element_type=F32)
    o_ref[...] = (g * _sigmoid(g) * u).astype(o_ref.dtype)


def _ffn_up(a, wg, wu, layer, tm, tn, name):
    m, k = a.shape
    n = wg.shape[2]
    wspec = pl.BlockSpec((None, k, tn), lambda i, j: (layer, 0, j))
    return pl.pallas_call(
        _ffn_up_body,
        grid=(m // tm, n // tn),
        in_specs=[pl.BlockSpec((tm, k), lambda i, j: (i, 0)), wspec, wspec],
        out_specs=pl.BlockSpec((tm, tn), lambda i, j: (i, j)),
        out_shape=jax.ShapeDtypeStruct((m, n), BF16),
        compiler_params=_cparams(("parallel", "parallel"), VMEM_LIMIT),
        name=name,
    )(a, wg, wu)


def _out_proj_body(ya_ref, yb_ref, yc_ref, w_ref, r_ref, o_ref):
    acc = jnp.dot(ya_ref[...], w_ref[0:W_A, :].astype(BF16), preferred_element_type=F32)
    acc += jnp.dot(yb_ref[...], w_ref[W_A:W_A + W_B, :].astype(BF16), preferred_element_type=F32)
    acc += jnp.dot(yc_ref[...], w_ref[W_A + W_B:, :].astype(BF16), preferred_element_type=F32)
    o_ref[...] = r_ref[...] + acc


def _out_proj(ya, yb, yc, w, layer, h, row_tile0, tm, tn, name):
    rows = ya.shape[0]
    n = w.shape[2]
    hspec = pl.BlockSpec((tm, tn), lambda i, j: (row_tile0 + i, j))
    return pl.pallas_call(
        _out_proj_body,
        grid=(rows // tm, n // tn),
        in_specs=[pl.BlockSpec((tm, W_A), lambda i, j: (i, 0)),
                  pl.BlockSpec((tm, W_B), lambda i, j: (i, 0)),
                  pl.BlockSpec((tm, W_C), lambda i, j: (i, 0)),
                  pl.BlockSpec((None, W_A + W_B + W_C, tn), lambda i, j: (layer, 0, j)),
                  hspec],
        out_specs=hspec,
        out_shape=jax.ShapeDtypeStruct(h.shape, F32),
        input_output_aliases={4: 0},
        compiler_params=_cparams(("parallel", "parallel"), VMEM_LIMIT),
        name=name,
    )(ya, yb, yc, w, h)


def _kv_out_body(*refs, depth, seq):
    ko_ref, vo_ref = refs[4 * depth:]
    for l in range(depth):
        km_ref, kr_ref, vm_ref, vr_ref = refs[4 * l:4 * l + 4]
        ko_ref[l, 0:N_META] = km_ref[...]
        ko_ref[l, N_META:N_META + seq] = kr_ref[...]
        vo_ref[l, 0:N_META] = vm_ref[...]
        vo_ref[l, N_META:N_META + seq] = vr_ref[...]


def _kv_out(projs, nb, seq, meta_blk, name):
    depth = len(projs)
    t_valid = N_META + seq
    in_specs, args = [], []
    for proj in projs:
        for col in (COL_KA, COL_VA):
            in_specs += [pl.BlockSpec((N_META, HEAD_DIM), lambda b, h, col=col: (meta_blk, col + h)),
                         pl.BlockSpec((seq, HEAD_DIM), lambda b, h, col=col: (b, col + h))]
            args += [proj, proj]
    ospec = pl.BlockSpec((depth, None, None, t_valid, HEAD_DIM), lambda b, h: (0, b, h, 0, 0))
    oshape = jax.ShapeDtypeStruct((depth, nb, H_A, t_valid, HEAD_DIM), F32)
    return pl.pallas_call(
        functools.partial(_kv_out_body, depth=depth, seq=seq),
        grid=(nb, H_A),
        in_specs=in_specs,
        out_specs=[ospec, ospec],
        out_shape=[oshape, oshape],
        compiler_params=_cparams(("parallel", "parallel")),
        name=name,
    )(*args)


def _lambda(lamp_ref, lam_init):
    lp = lamp_ref[...]
    s1 = jnp.sum(lp[0:1] * lp[1:2], axis=-1, keepdims=True)
    s2 = jnp.sum(lp[2:3] * lp[3:4], axis=-1, keepdims=True)
    return jnp.exp(s1) - jnp.exp(s2) + lam_init


def _split_maps(q):
    lane = lax.broadcasted_iota(jnp.int32, q.shape, 1)
    lo = jnp.where(lane < HALF, q, 0.0)
    hi = jnp.where(lane >= HALF, q, 0.0)
    return jnp.concatenate([lo, hi], axis=0)


def _subln(a, sub_ref, lam_init):
    y = a * lax.rsqrt(jnp.mean(a * a, axis=-1, keepdims=True) + EPS) * sub_ref[...]
    return y * (1.0 - lam_init)


ATT_HG = 4


def _attn_prompt_body(q_ref, k_ref, v_ref, km_ref, vm_ref, band_ref, lamp_ref, sub_ref, o_ref,
                      kb_ref, vt_ref, s_ref, p_ref, *, lam_init, nblk):
    lanes = [slice(g * HEAD_DIM, (g + 1) * HEAD_DIM) for g in range(ATT_HG)]

    zpad = jnp.zeros((BLK - N_META, ATT_HG * HEAD_DIM), F32)
    kb_ref[0:BLK, :] = jnp.concatenate([km_ref[...], zpad], axis=0).astype(BF16)
    kb_ref[BLK:, :] = k_ref[...].astype(BF16)
    vm = jnp.concatenate([vm_ref[...], zpad], axis=0)
    for g in range(ATT_HG):
        vt_ref[g, :, 0:BLK] = vm[:, lanes[g]].T.astype(BF16)
        for j in range(1, nblk):
            vt = v_ref[(j - 1) * BLK:j * BLK, lanes[g]].T
            vt_ref[g, :, j * BLK:(j + 1) * BLK] = vt.astype(BF16)

    def attend(qi, nt):
        rows = pl.ds(pl.multiple_of(qi * BLK, BLK), BLK)
        lam = _lambda(lamp_ref, lam_init)
        mrun = [None] * ATT_HG
        lrun = [None] * ATT_HG
        m = [None] * ATT_HG
        qst = [None] * ATT_HG

        def score_tile(g, j):
            if j == 0:
                qst[g] = _split_maps(q_ref[rows, lanes[g]] * (HALF ** -0.5)).T.astype(BF16)
            if j == 0:
                bt = band_ref[g, jnp.where(qi == 0, 4, 5)]
            else:
                bt = band_ref[g, jnp.clip(j - qi + 1, 0, 3)]
            s = jnp.dot(kb_ref[j * BLK:(j + 1) * BLK, lanes[g]], qst[g],
                        preferred_element_type=F32)
            s = s + jnp.concatenate([bt, bt], axis=1)
            s_ref[g, j] = s
            mrun[g] = s if mrun[g] is None else jnp.maximum(mrun[g], s)
            if j == nt - 1:
                m[g] = jnp.max(mrun[g], axis=0, keepdims=True)

        def exp_tile(g, j):
            p = jnp.exp(s_ref[g, j] - m[g])
            lrun[g] = p if lrun[g] is None else lrun[g] + p
            p_ref[g, j * BLK:(j + 1) * BLK, :] = p.astype(BF16)
            if j == nt - 1:
                l = jnp.sum(lrun[g], axis=0, keepdims=True)
                ot = jnp.dot(vt_ref[g, :, 0:nt * BLK], p_ref[g, 0:nt * BLK, :],
                             preferred_element_type=F32) / l
                at = ot[:, :BLK] - lam * ot[:, BLK:]
                at = at * lax.rsqrt(jnp.mean(at * at, axis=0, keepdims=True) + EPS)
                o_ref[rows, lanes[g]] = (at.T * sub_ref[...] * (1.0 - lam_init)
                                         ).astype(o_ref.dtype)

        for step in range(ATT_HG + 1):
            for j in range(nt):
                if step < ATT_HG:
                    score_tile(step, j)
                if step >= 1:
                    exp_tile(step - 1, j)

    def query_tile(qi, carry):
        lo = 0
        for hi in sorted(set(range(2, nblk, 2)) | {nblk}):
            pl.when(jnp.logical_and(qi + 2 > lo, qi + 2 <= hi))(functools.partial(attend, qi, hi))
            lo = hi
        return carry

    lax.fori_loop(0, nblk - 1, query_tile, 0)


def _attn_prompt(proj, band, lamp, sub, nb, seq, meta_blk, lam_init, name):
    nq = seq // BLK
    nblk = nq + 1
    tp = nblk * BLK
    wg = ATT_HG * HEAD_DIM
    body = functools.partial(_attn_prompt_body, lam_init=lam_init, nblk=nblk)
    return pl.pallas_call(
        body,
        grid=(nb, H_A // ATT_HG),
        in_specs=[
            pl.BlockSpec((seq, wg), lambda b, h: (b, COL_QA // ATT_HG + h)),
            pl.BlockSpec((seq, wg), lambda b, h: (b, COL_KA // ATT_HG + h)),
            pl.BlockSpec((seq, wg), lambda b, h: (b, COL_VA // ATT_HG + h)),
            pl.BlockSpec((N_META, wg), lambda b, h: (meta_blk, COL_KA // ATT_HG + h)),
            pl.BlockSpec((N_META, wg), lambda b, h: (meta_blk, COL_VA // ATT_HG + h)),
            pl.BlockSpec((ATT_HG, 6, BLK, BLK), lambda b, h: (h, 0, 0, 0)),
            pl.BlockSpec((4, HALF), lambda b, h: (0, 0)),
            pl.BlockSpec((1, HEAD_DIM), lambda b, h: (0, 0)),
        ],
        out_specs=pl.BlockSpec((seq, wg), lambda b, h: (b, h)),
        out_shape=jax.ShapeDtypeStruct((nb * seq, W_A), BF16),
        scratch_shapes=[pltpu.VMEM((tp, wg), BF16), pltpu.VMEM((ATT_HG, HEAD_DIM, tp), BF16),
                        pltpu.VMEM((ATT_HG, nblk, BLK, 2 * BLK), F32),
                        pltpu.VMEM((ATT_HG, tp, 2 * BLK), BF16)],
        compiler_params=_cparams(("parallel", "parallel"), VMEM_LIMIT),
        name=name,
    )(proj, proj, proj, proj, proj, band, lamp, sub)


def _attn_meta_body(q_ref, k_ref, v_ref, bias_ref, lamp_ref, sub_ref, o_ref, *, lam_init):
    zpad = jnp.zeros((BLK - N_META, HEAD_DIM), F32)
    lam = _lambda(lamp_ref, lam_init)
    for h in range(H_A):
        sl = slice(h * HEAD_DIM, (h + 1) * HEAD_DIM)
        qs = _split_maps(q_ref[:, sl] * (HALF ** -0.5)).astype(BF16)
        kp = jnp.concatenate([k_ref[:, sl], zpad], axis=0).astype(BF16)
        vp = jnp.concatenate([v_ref[:, sl], zpad], axis=0).astype(BF16)
        s = lax.dot_general(qs, kp, (((1,), (1,)), ((), ())),
                            preferred_element_type=F32) + bias_ref[h]
        p = jnp.exp(s - jnp.max(s, axis=1, keepdims=True))
        o = jnp.dot(p.astype(BF16), vp, preferred_element_type=F32) / jnp.sum(p, axis=1,
                                                                              keepdims=True)
        a = o[0:N_META] - lam * o[N_META:]
        o_ref[:, sl] = _subln(a, sub_ref, lam_init).astype(o_ref.dtype)


def _attn_meta(proj, bias, lamp, sub, meta_blk, lam_init, name):
    def spec(col):
        return pl.BlockSpec((N_META, W_A), lambda i: (meta_blk, col))

    return pl.pallas_call(
        functools.partial(_attn_meta_body, lam_init=lam_init),
        grid=(1,),
        in_specs=[spec(0), spec(1), spec(2),
                  pl.BlockSpec((H_A, 2 * N_META, BLK), lambda i: (0, 0, 0)),
                  pl.BlockSpec((4, HALF), lambda i: (0, 0)),
                  pl.BlockSpec((1, HEAD_DIM), lambda i: (0, 0))],
        out_specs=pl.BlockSpec((N_META, W_A), lambda i: (0, 0)),
        out_shape=jax.ShapeDtypeStruct((N_META, W_A), BF16),
        compiler_params=_cparams(("arbitrary",)),
        name=name,
    )(proj, proj, proj, bias, lamp, sub)


PPS = 8


def _attn_sample_body(pt_ref, q_ref, kn_ref, vn_ref, *rest, lam_init, n_pages, n_new):
    kp_refs, vp_refs = rest[:PPS], rest[PPS:2 * PPS]
    bias_ref, lamp_ref, sub_ref, o_ref, qs_ref, m_ref, l_ref, acc_ref = rest[2 * PPS:]
    p = pl.program_id(1)
    n_steps = n_pages // PPS

    @pl.when(p == 0)
    def _():
        for h in range(H_A):
            q = q_ref[0:n_new, h * HEAD_DIM:(h + 1) * HEAD_DIM] * (HALF ** -0.5)
            qs = _split_maps(q)
            pad = jnp.zeros((SROWS - 2 * n_new, HEAD_DIM), F32)
            qs_ref[h] = jnp.concatenate([qs, pad], axis=0).astype(BF16)
        m_ref[...] = jnp.full(m_ref.shape, M_INIT, F32)
        l_ref[...] = jnp.zeros(l_ref.shape, F32)
        acc_ref[...] = jnp.zeros(acc_ref.shape, F32)

    def update(get_k, get_v, bias, n_rep):
        s = jnp.stack([lax.dot_general(qs_ref[h], get_k(h).astype(BF16), (((1,), (1,)), ((), ())),
                                       preferred_element_type=F32) for h in range(H_A)])
        s = s + bias
        m_prev = m_ref[...]
        m_new = jnp.maximum(m_prev, jnp.max(s, axis=-1, keepdims=True))
        alpha = jnp.exp(m_prev - m_new)
        pm = jnp.exp(s - jnp.concatenate([m_new] * n_rep, axis=-1))
        l_ref[...] = alpha * l_ref[...] + jnp.sum(pm, axis=-1, keepdims=True)
        m_ref[...] = m_new
        pb = pm.astype(BF16)
        for h in range(H_A):
            acc_ref[h] = alpha[h] * acc_ref[h] + jnp.dot(pb[h], get_v(h).astype(BF16),
                                                         preferred_element_type=F32)

    @pl.when(p < n_steps)
    def _():
        def pages(refs, h):
            return jnp.concatenate([r[h] for r in refs], axis=0)

        last_idx = jnp.where(p == n_steps - 1, 1, 0)
        bias = jnp.concatenate([bias_ref[0]] * (PPS - 1) + [bias_ref[last_idx]], axis=-1)
        update(functools.partial(pages, kp_refs), functools.partial(pages, vp_refs), bias, PPS)

    @pl.when(p == n_steps)
    def _():
        zpad = jnp.zeros((PAGE_SIZE - SROWS, HEAD_DIM), F32)

        def new_rows(ref, h):
            return jnp.concatenate([ref[:, h * HEAD_DIM:(h + 1) * HEAD_DIM], zpad], axis=0)

        update(functools.partial(new_rows, kn_ref), functools.partial(new_rows, vn_ref),
               bias_ref[2], 1)
        lam = _lambda(lamp_ref, lam_init)
        for h in range(H_A):
            o = acc_ref[h] / l_ref[h]
            a = o[0:n_new] - lam * o[n_new:2 * n_new]
            y = _subln(a, sub_ref, lam_init)
            pad = jnp.zeros((SROWS - n_new, HEAD_DIM), F32)
            o_ref[:, h * HEAD_DIM:(h + 1) * HEAD_DIM] = (
                jnp.concatenate([y, pad], axis=0).astype(o_ref.dtype))


def _attn_sample(page_table, proj, cache_k, cache_v, bias, lamp, sub, layer, row_blk0, n_new,
                 lam_init, name):
    db, n_pages = page_table.shape
    assert n_pages % PPS == 0
    body = functools.partial(_attn_sample_body, lam_init=lam_init, n_pages=n_pages, n_new=n_new)

    def page_spec(t):
        return pl.BlockSpec(
            (None, None, H_A, PAGE_SIZE, HEAD_DIM),
            lambda b, p, pt: (layer, pt[b, jnp.minimum(p * PPS + t, n_pages - 1)], 0, 0, 0))

    page_specs = [page_spec(t) for t in range(PPS)]
    grid_spec = pltpu.PrefetchScalarGridSpec(
        num_scalar_prefetch=1,
        grid=(db, n_pages // PPS + 1),
        in_specs=[
            pl.BlockSpec((SROWS, W_A), lambda b, p, pt: (row_blk0 + b, 0)),
            pl.BlockSpec((SROWS, W_A), lambda b, p, pt: (row_blk0 + b, 1)),
            pl.BlockSpec((SROWS, W_A), lambda b, p, pt: (row_blk0 + b, 2)),
            *page_specs, *page_specs,
            pl.BlockSpec((3, H_A, SROWS, PAGE_SIZE), lambda b, p, pt: (0, 0, 0, 0)),
            pl.BlockSpec((4, HALF), lambda b, p, pt: (0, 0)),
            pl.BlockSpec((1, HEAD_DIM), lambda b, p, pt: (0, 0)),
        ],
        out_specs=pl.BlockSpec((SROWS, W_A), lambda b, p, pt: (b, 0)),
        scratch_shapes=[pltpu.VMEM((H_A, SROWS, HEAD_DIM), BF16),
                        pltpu.VMEM((H_A, SROWS, HEAD_DIM), F32),
                        pltpu.VMEM((H_A, SROWS, HEAD_DIM), F32),
                        pltpu.VMEM((H_A, SROWS, HEAD_DIM), F32)],
    )
    return pl.pallas_call(
        body,
        grid_spec=grid_spec,
        out_shape=jax.ShapeDtypeStruct((db * SROWS, W_A), BF16),
        compiler_params=_cparams(("parallel", "arbitrary"), VMEM_LIMIT),
        name=name,
    )(page_table, proj, proj, proj, *([cache_k] * PPS), *([cache_v] * PPS), bias, lamp, sub)


RET_HG = 12


def _retention_body(q_ref, k_ref, v_ref, g_ref, c2_ref, s2_ref, lg_ref, gn_ref, s0_ref,
                    y_ref, sout_ref, st_ref, *, rows, n_chunks, last_valid):
    c = pl.program_id(2)

    @pl.when(c == 0)
    def _():
        st_ref[...] = s0_ref[...]

    lc = jnp.where(c == n_chunks - 1, float(last_valid), float(BLK))
    ii = lax.broadcasted_iota(jnp.int32, (BLK, BLK), 0).astype(F32)
    jj = lax.broadcasted_iota(jnp.int32, (BLK, BLK), 1).astype(F32)
    rel = ii - jj

    def rows128(x):
        if rows == BLK:
            return x
        return jnp.concatenate([x, jnp.zeros((BLK - rows, x.shape[1]), x.dtype)], axis=0)

    c2 = rows128(c2_ref[...])
    s2 = rows128(s2_ref[...])

    def rot(x):
        return x * c2 + pltpu.roll(x, HALF, 1) * s2

    heads = []
    for j in range(RET_HG):
        sl = slice(j * HEAD_DIM, (j + 1) * HEAD_DIM)
        lgv = lg_ref[:, sl]
        v = rows128(v_ref[:, sl]).astype(BF16)
        qr = rot(rows128(q_ref[:, sl])).astype(BF16)
        kr = rot(rows128(k_ref[:, sl])) * (HEAD_DIM ** -0.5)
        st = st_ref[j]
        sc = lax.dot_general(qr, kr.astype(BF16), (((1,), (1,)), ((), ())),
                             preferred_element_type=F32)
        cross = jnp.dot(qr, st.astype(BF16), preferred_element_type=F32)
        kdec = kr * jnp.where(ii < lc, jnp.exp((lc - 1.0 - ii) * lgv), 0.0)
        st_ref[j] = jnp.exp(lc * lgv) * st + jnp.dot(kdec.T.astype(BF16), v,
                                                     preferred_element_type=F32)
        heads.append((sl, lgv, v, sc, cross))
    for sl, lgv, v, sc, cross in heads:
        decay = jnp.where(rel >= 0, jnp.exp(lgv * jnp.maximum(rel, 0.0)), 0.0)
        o = jnp.dot((sc * decay).astype(BF16), v, preferred_element_type=F32)
        o = o + cross * jnp.exp((ii + 1.0) * lgv)
        y = o[:rows]
        y = y * lax.rsqrt(jnp.mean(y * y, axis=-1, keepdims=True) + EPS) * gn_ref[...]
        g = g_ref[:, sl]
        y_ref[:, sl] = (y * (g * _sigmoid(g))).astype(y_ref.dtype)

    @pl.when(c == n_chunks - 1)
    def _():
        sout_ref[...] = st_ref[...]


def _retention(proj, c2, s2, lgrow, gn, s0, nb, n_chunks, rows, row_blk0, last_valid, name,
               shared_init=False):
    body = functools.partial(_retention_body, rows=rows, n_chunks=n_chunks, last_valid=last_valid)
    wb = RET_HG * HEAD_DIM
    cb = wb // 128

    def in_spec(col0):
        return pl.BlockSpec((rows, wb),
                            lambda b, hg, c: (row_blk0 + b * n_chunks + c, col0 // cb + hg))

    st_block = (None, RET_HG, HEAD_DIM, HEAD_DIM)
    st_spec = pl.BlockSpec(st_block, lambda b, hg, c: (b, hg, 0, 0))
    s0_spec = pl.BlockSpec(st_block, lambda b, hg, c: (0, hg, 0, 0)) if shared_init else st_spec
    return pl.pallas_call(
        body,
        grid=(nb, H_B // RET_HG, n_chunks),
        in_specs=[in_spec(COL_QB), in_spec(COL_KB), in_spec(COL_VB), in_spec(COL_GB),
                  pl.BlockSpec((rows, HEAD_DIM), lambda b, hg, c: (c, 0)),
                  pl.BlockSpec((rows, HEAD_DIM), lambda b, hg, c: (c, 0)),
                  pl.BlockSpec((1, wb), lambda b, hg, c: (0, hg)),
                  pl.BlockSpec((1, HEAD_DIM), lambda b, hg, c: (0, 0)),
                  s0_spec],
        out_specs=[pl.BlockSpec((rows, wb), lambda b, hg, c: (b * n_chunks + c, hg)), st_spec],
        out_shape=[jax.ShapeDtypeStruct((nb * n_chunks * rows, W_B), BF16),
                   jax.ShapeDtypeStruct((nb, H_B, HEAD_DIM, HEAD_DIM), F32)],
        scratch_shapes=[pltpu.VMEM((RET_HG, HEAD_DIM, HEAD_DIM), F32)],
        compiler_params=_cparams(("parallel", "parallel", "arbitrary"), VMEM_LIMIT),
        name=name,
    )(proj, proj, proj, proj, c2, s2, lgrow, gn, s0)


CONV_LANES = 256


def _conv_body(ca0_ref, ca1_ref, cb0_ref, cb1_ref, ctx_ref, w_ref, b_ref, lg_ref, lb_ref,
               y_ref, tail_ref, uc_ref, us_ref, cv_ref, *, rows, n_chunks, last_valid):
    c = pl.program_id(1)

    @pl.when(c == 0)
    def _():
        uc_ref[0:CTX_ROWS, :] = ctx_ref[...]

    ca = jnp.concatenate([ca0_ref[...], ca1_ref[...]], axis=1)
    cb = jnp.concatenate([cb0_ref[...], cb1_ref[...]], axis=1)
    uc_ref[CTX_ROWS:CTX_ROWS + rows, :] = ca * _sigmoid(cb)
    span = CTX_ROWS + rows - 8
    for r in range(1, 8):
        us_ref[r] = uc_ref[pl.ds(r, span), :]
    for cc in range(0, W_C, CONV_LANES):
        cs = slice(cc, cc + CONV_LANES)
        acc = jnp.broadcast_to(b_ref[:, cs], (rows, CONV_LANES))
        for w in range(CONV_W):
            off = CTX_ROWS - (CONV_W - 1) + w
            r, a = off % 8, off - off % 8
            src = uc_ref[a:a + rows, cs] if r == 0 else us_ref[r, a:a + rows, cs]
            acc = acc + src * w_ref[w:w + 1, cs]
        cv_ref[:, cs] = acc
    acc = cv_ref[...]
    mu = jnp.mean(acc, axis=-1, keepdims=True)
    xc = acc - mu
    var = jnp.mean(xc * xc, axis=-1, keepdims=True)
    y = xc * lax.rsqrt(var + EPS) * lg_ref[...] + lb_ref[...]
    y_ref[...] = (y * _sigmoid(y)).astype(y_ref.dtype)

    @pl.when(c == n_chunks - 1)
    def _():
        tail_ref[...] = uc_ref[pl.ds(CTX_ROWS + last_valid - (CONV_W - 1), CONV_W - 1), :]

    nxt = uc_ref[rows:rows + CTX_ROWS, :]
    uc_ref[0:CTX_ROWS, :] = nxt


def _conv(proj, ctx0, conv_w, conv_b, ln_g, ln_b, layer, nb, n_chunks, rows, row_blk0,
          last_valid, name, shared_ctx=False):
    body = functools.partial(_conv_body, rows=rows, n_chunks=n_chunks, last_valid=last_valid)
    half = W_C // 2
    hb = half // 128

    def in_spec(col):
        return pl.BlockSpec((rows, half), lambda b, c: (row_blk0 + b * n_chunks + c, col))

    def par_spec():
        return pl.BlockSpec((None, 1, W_C), lambda b, c: (layer, 0, 0))

    return pl.pallas_call(
        body,
        grid=(nb, n_chunks),
        in_specs=[in_spec(COL_CA // hb), in_spec(COL_CA // hb + 1),
                  in_spec(COL_CB // hb), in_spec(COL_CB // hb + 1),
                  pl.BlockSpec((None, CTX_ROWS, W_C),
                               lambda b, c: (0 if shared_ctx else b, 0, 0)),
                  pl.BlockSpec((None, CONV_W, W_C), lambda b, c: (layer, 0, 0)),
                  par_spec(), par_spec(), par_spec()],
        out_specs=[pl.BlockSpec((rows, W_C), lambda b, c: (b * n_chunks + c, 0)),
                   pl.BlockSpec((None, CONV_W - 1, W_C), lambda b, c: (b, 0, 0))],
        out_shape=[jax.ShapeDtypeStruct((nb * n_chunks * rows, W_C), BF16),
                   jax.ShapeDtypeStruct((nb, CONV_W - 1, W_C), F32)],
        scratch_shapes=[pltpu.VMEM((CTX_ROWS + rows, W_C), F32),
                        pltpu.VMEM((8, CTX_ROWS + rows - 8, W_C), F32),
                        pltpu.VMEM((rows, W_C), F32)],
        compiler_params=_cparams(("parallel", "arbitrary"), VMEM_LIMIT),
        name=name,
    )(proj, proj, proj, proj, ctx0, conv_w, conv_b, ln_g, ln_b)


def _t5_bias(rel_bias, dist):
    n = jnp.maximum(dist, 0)
    nf = jnp.maximum(n, 1).astype(F32)
    large = REL_EXACT + (jnp.log(nf / REL_EXACT) / math.log(REL_MAX_DIST / REL_EXACT)
                         * (N_BUCKETS - REL_EXACT)).astype(jnp.int32)
    bucket = jnp.where(n < REL_EXACT, n, jnp.minimum(large, N_BUCKETS - 1))
    onehot = (bucket[..., None] == jnp.arange(N_BUCKETS, dtype=jnp.int32)).astype(F32)
    bias = jnp.einsum("...k,kh->h...", onehot, rel_bias.astype(F32),
                      precision=lax.Precision.HIGHEST)
    return jnp.where(dist[None] >= 0, bias, NEG_INF)


def _rotary_tables(pos):
    inv = ROPE_BASE ** (-jnp.arange(HALF, dtype=F32) / HALF)
    ang = pos.astype(F32)[:, None] * inv[None, :]
    cos, sin = jnp.cos(ang), jnp.sin(ang)
    return jnp.concatenate([cos, cos], axis=1), jnp.concatenate([-sin, sin], axis=1)


def kernel(x_prompt, x_sample, cache_k, cache_v, state_ret, state_conv, page_table, meta, rel_bias,
           norm_mix, w_in, lam_q1, lam_k1, lam_q2, lam_k2, subln_a, gn_b, conv_w, conv_b, conv_ln_g,
           conv_ln_b, w_out, norm_ffn, w_gate, w_up, w_down, norm_final):
    nb, seq = x_prompt.shape[:2]
    db, n_new = x_sample.shape[:2]
    depth = w_in.shape[0]
    n_pages = page_table.shape[1]
    past = n_pages * PAGE_SIZE
    assert seq % BLK == 0 and N_META == SROWS and BLK >= REL_MAX_DIST and PAGE_SIZE >= REL_MAX_DIST
    real = nb * seq
    samp0 = real + BLK
    m_all = samp0 + db * SROWS
    meta_blk = real // SROWS
    s_blk0 = samp0 // SROWS
    d_ff = w_gate.shape[2]

    tm = m_all // 6
    tr = m_all // 24

    h = jnp.concatenate([x_prompt.reshape(real, D_MODEL), meta.astype(F32),
                         jnp.zeros((BLK - N_META, D_MODEL), F32),
                         jnp.pad(x_sample, ((0, 0), (0, SROWS - n_new), (0, 0))
                                 ).reshape(db * SROWS, D_MODEL)], axis=0)

    kk = jnp.arange(BLK, dtype=jnp.int32)[:, None]
    ii = jnp.arange(BLK, dtype=jnp.int32)[None, :]
    is_meta = kk < N_META
    far_d = jnp.full((BLK, BLK), REL_MAX_DIST, jnp.int32)
    band = _t5_bias(rel_bias, jnp.stack([
        far_d, BLK + ii - kk, ii - kk, jnp.full((BLK, BLK), -1, jnp.int32),
        jnp.where(is_meta, N_META + ii - kk, -1), jnp.where(is_meta, far_d, -1)]))
    rm = jnp.arange(2 * N_META, dtype=jnp.int32)[:, None] % N_META
    jm = jnp.arange(BLK, dtype=jnp.int32)[None, :]
    bias_m = _t5_bias(rel_bias, jnp.where(jm < N_META, rm - jm, -1))
    r_new = jnp.arange(SROWS, dtype=jnp.int32) % n_new
    jk = jnp.arange(PAGE_SIZE, dtype=jnp.int32)[None, :]
    dist_s = jnp.stack([jnp.full((SROWS, PAGE_SIZE), REL_MAX_DIST, jnp.int32),
                        PAGE_SIZE + r_new[:, None] - jk,
                        jnp.where(jk < n_new, r_new[:, None] - jk, -1)])
    bias_s = jnp.transpose(_t5_bias(rel_bias, dist_s), (1, 0, 2, 3))
    cache_kh = jnp.transpose(cache_k, (0, 1, 3, 2, 4))
    cache_vh = jnp.transpose(cache_v, (0, 1, 3, 2, 4))
    c2_m, s2_m = _rotary_tables(jnp.arange(N_META, dtype=jnp.int32))
    c2_p, s2_p = _rotary_tables(N_META + jnp.arange(seq, dtype=jnp.int32))
    c2_s, s2_s = _rotary_tables(past + jnp.arange(SROWS, dtype=jnp.int32))
    log_gamma = jnp.log1p(-jnp.exp2(-5.0 - jnp.arange(H_B, dtype=F32)))
    lgrow = jnp.repeat(log_gamma, HEAD_DIM)[None, :]

    zero_state = jnp.zeros((1, H_B, HEAD_DIM, HEAD_DIM), F32)
    zero_ctx = jnp.zeros((1, CTX_ROWS, W_C), F32)
    ctx_pad = ((0, 0), (CTX_ROWS - (CONV_W - 1), 0), (0, 0))

    def small_rows(y_meta, y_sample):
        return jnp.concatenate([y_meta, jnp.zeros((BLK - N_META, y_meta.shape[1]), y_meta.dtype),
                                y_sample], axis=0)

    conv_b3 = conv_b[:, None, :]
    ln_g3 = conv_ln_g[:, None, :]
    ln_b3 = conv_ln_b[:, None, :]

    projs, ret_p, conv_p, k_s, v_s, ret_s, conv_s = [], [], [], [], [], [], []
    for l in range(depth):
        lam_init = 0.8 - 0.6 * math.exp(-0.3 * l)
        lamp = jnp.stack([lam_q1[l], lam_k1[l], lam_q2[l], lam_k2[l]]).astype(F32)
        sub = subln_a[l][None, :]
        gn = gn_b[l][None, :]

        n1 = _rmsnorm(h, norm_mix[l], BF16, tr, f"norm_mix{l}")
        proj = _matmul(n1, w_in, l, tm, 512, f"in_proj{l}")

        ya_m = _attn_meta(proj, bias_m, lamp, sub, meta_blk, lam_init, f"attn_meta{l}")
        yb_m, st_m = _retention(proj, c2_m, s2_m, lgrow, gn, zero_state, 1, 1, N_META, meta_blk,
                                N_META, f"ret_meta{l}")
        yc_m, tail_m = _conv(proj, zero_ctx, conv_w, conv_b3, ln_g3, ln_b3, l, 1, 1, N_META,
                             meta_blk, N_META, f"conv_meta{l}")

        ya_p = _attn_prompt(proj, band, lamp, sub, nb, seq, meta_blk, lam_init, f"attn_prompt{l}")
        ya_s = _attn_sample(page_table, proj, cache_kh, cache_vh, bias_s, lamp, sub, l, s_blk0,
                            n_new, lam_init, f"attn_sample{l}")
        yb_p, st_p = _retention(proj, c2_p, s2_p, lgrow, gn, st_m, nb, seq // BLK, BLK, 0, BLK,
                                f"ret_prompt{l}", shared_init=True)
        yb_s, st_s = _retention(proj, c2_s, s2_s, lgrow, gn, state_ret[l], db, 1, SROWS,
                                s_blk0, n_new, f"ret_sample{l}")
        yc_p, tail_p = _conv(proj, jnp.pad(tail_m, ctx_pad), conv_w, conv_b3, ln_g3, ln_b3, l, nb,
                             seq // BLK, BLK, 0, BLK, f"conv_prompt{l}", shared_ctx=True)
        yc_s, tail_s = _conv(proj, jnp.pad(state_conv[l], ctx_pad), conv_w, conv_b3, ln_g3, ln_b3,
                             l, db, 1, SROWS, s_blk0, n_new, f"conv_sample{l}")

        h = _out_proj(ya_p, yb_p, yc_p, w_out, l, h, 0, real // 8, 512, f"out_proj_p{l}")
        n_small = m_all - real
        h = _out_proj(small_rows(ya_m, ya_s), small_rows(yb_m, yb_s), small_rows(yc_m, yc_s),
                      w_out, l, h, real // n_small, n_small, 256, f"out_proj_s{l}")
        n2 = _rmsnorm(h, norm_ffn[l], BF16, tr, f"norm_ffn{l}")
        hid = _ffn_up(n2, w_gate, w_up, l, tm, 256, f"ffn_up{l}")
        for kb in range(2):
            h = _matmul(hid, w_down, l, tm, 256, f"ffn_down{l}_{kb}", res=h,
                        tk=d_ff // 2, kblk=kb)

        projs.append(proj)
        kvs = proj[samp0:, W_A:3 * W_A].reshape(db, SROWS, 2, H_A, HEAD_DIM)[:, :n_new]
        k_s.append(kvs[:, :, 0])
        v_s.append(kvs[:, :, 1])
        ret_p.append(st_p)
        ret_s.append(st_s)
        conv_p.append(tail_p)
        conv_s.append(tail_s)

    y_prompt = _rmsnorm(h, norm_final, F32, 2 * BLK, "norm_final_p", rows=real).reshape(
        nb, seq, D_MODEL)
    hs = _rmsnorm(h[samp0:], norm_final, F32, db * SROWS, "norm_final_s")
    y_sample = hs.reshape(db, SROWS, D_MODEL)[:, :n_new]
    k_hm, v_hm = _kv_out(projs, nb, seq, meta_blk, "kv_out")
    k_prompt = jnp.transpose(k_hm, (0, 1, 3, 2, 4))
    v_prompt = jnp.transpose(v_hm, (0, 1, 3, 2, 4))
    return (y_prompt, y_sample, k_prompt, v_prompt, jnp.stack(ret_p), jnp.stack(conv_p),
            jnp.stack(k_s), jnp.stack(v_s), jnp.stack(ret_s), jnp.stack(conv_s))
```

```python
import functools
import math

import jax
import jax.numpy as jnp
from jax import lax
from jax.experimental import pallas as pl
from jax.experimental.pallas import tpu as pltpu

F32 = jnp.float32
BF16 = jnp.bfloat16

D_MODEL = 4096
N_META = 16
HEAD_DIM = 128
HALF = HEAD_DIM // 2
H_A = 12
H_B = 12
W_A = H_A * HEAD_DIM
W_B = H_B * HEAD_DIM
W_C = D_MODEL - W_A - W_B
N_IN = 3 * W_A + 4 * W_B + 2 * W_C
CONV_W = 31
N_BUCKETS = 32
REL_EXACT = 16
REL_MAX_DIST = 128
ROPE_BASE = 10000.0
EPS = 1e-6
NEG_INF = -1e30
PAGE_SIZE = 128

BLK = 128
SROWS = 16
CTX_ROWS = 32
M_INIT = -3.0e38
VMEM_LIMIT = 56 * 1024 * 1024

COL_QA, COL_KA, COL_VA = 0, W_A // 128, 2 * W_A // 128
COL_QB = 3 * W_A // 128
COL_KB = COL_QB + W_B // 128
COL_VB = COL_KB + W_B // 128
COL_GB = COL_VB + W_B // 128
COL_CA = COL_GB + W_B // 128
COL_CB = COL_CA + W_C // 128


def _cparams(sem, vmem=None):
    return pltpu.CompilerParams(dimension_semantics=sem, vmem_limit_bytes=vmem)


def _sigmoid(x):
    return 1.0 / (1.0 + jnp.exp(-x))


def _rmsnorm_body(x_ref, g_ref, o_ref):
    x = x_ref[...]
    ms = jnp.mean(x * x, axis=-1, keepdims=True)
    o_ref[...] = (x * lax.rsqrt(ms + EPS) * g_ref[...]).astype(o_ref.dtype)


def _rmsnorm(x, g, out_dtype, tr, name, rows=None):
    m, d = x.shape
    m = m if rows is None else rows
    return pl.pallas_call(
        _rmsnorm_body,
        grid=(m // tr,),
        in_specs=[pl.BlockSpec((tr, d), lambda i: (i, 0)),
                  pl.BlockSpec((1, d), lambda i: (0, 0))],
        out_specs=pl.BlockSpec((tr, d), lambda i: (i, 0)),
        out_shape=jax.ShapeDtypeStruct((m, d), out_dtype),
        compiler_params=_cparams(("parallel",)),
        name=name,
    )(x, g.reshape(1, d))


def _mm_body(a_ref, w_ref, o_ref):
    o_ref[...] = jnp.dot(a_ref[...], w_ref[...].astype(BF16), preferred_element_type=F32)


def _mm_res_body(a_ref, w_ref, r_ref, o_ref):
    o_ref[...] = r_ref[...] + jnp.dot(a_ref[...], w_ref[...].astype(BF16),
                                      preferred_element_type=F32)


def _matmul(a, w, layer, tm, tn, name, res=None, tk=None, kblk=0):
    m = a.shape[0]
    k = a.shape[1] if tk is None else tk
    n = w.shape[2]
    in_specs = [pl.BlockSpec((tm, k), lambda i, j: (i, kblk)),
                pl.BlockSpec((None, k, tn), lambda i, j: (layer, kblk, j))]
    args = [a, w]
    body = _mm_body
    if res is not None:
        in_specs.append(pl.BlockSpec((tm, tn), lambda i, j: (i, j)))
        args.append(res)
        body = _mm_res_body
    return pl.pallas_call(
        body,
        grid=(m // tm, n // tn),
        in_specs=in_specs,
        out_specs=pl.BlockSpec((tm, tn), lambda i, j: (i, j)),
        out_shape=jax.ShapeDtypeStruct((m, n), F32),
        compiler_params=_cparams(("parallel", "parallel"), VMEM_LIMIT),
        name=name,
    )(*args)


def _ffn_up_body(a_ref, wg_ref, wu_ref, o_ref):
    a = a_ref[...]
    g = jnp.dot(a, wg_ref[...].astype(BF16), preferred_element_type=F32)
    u = jnp.dot(a, wu_ref[...].astype(BF16), preferred_element_type=F32)
    o_ref[...] = (g * _sigmoid(g) * u).astype(o_ref.dtype)


def _ffn_up(a, wg, wu, layer, tm, tn, name):
    m, k = a.shape
    n = wg.shape[2]
    wspec = pl.BlockSpec((None, k, tn), lambda i, j: (layer, 0, j))
    return pl.pallas_call(
        _ffn_up_body,
        grid=(m // tm, n // tn),
        in_specs=[pl.BlockSpec((tm, k), lambda i, j: (i, 0), pipeline_mode=pl.Buffered(1)),
                  wspec, wspec],
        out_specs=pl.BlockSpec((tm, tn), lambda i, j: (i, j)),
        out_shape=jax.ShapeDtypeStruct((m, n), BF16),
        compiler_params=_cparams(("parallel", "parallel"), VMEM_LIMIT),
        name=name,
    )(a, wg, wu)


def _out_proj_body(ya_ref, yb_ref, yc_ref, w_ref, r_ref, o_ref):
    acc = jnp.dot(ya_ref[...], w_ref[0:W_A, :].astype(BF16), preferred_element_type=F32)
    acc += jnp.dot(yb_ref[...], w_ref[W_A:W_A + W_B, :].astype(BF16), preferred_element_type=F32)
    acc += jnp.dot(yc_ref[...], w_ref[W_A + W_B:, :].astype(BF16), preferred_element_type=F32)
    o_ref[...] = r_ref[...] + acc


def _out_proj(ya, yb, yc, w, layer, h, row_tile0, tm, tn, name):
    rows = ya.shape[0]
    n = w.shape[2]
    hspec = pl.BlockSpec((tm, tn), lambda i, j: (row_tile0 + i, j))
    return pl.pallas_call(
        _out_proj_body,
        grid=(rows // tm, n // tn),
        in_specs=[pl.BlockSpec((tm, W_A), lambda i, j: (i, 0)),
                  pl.BlockSpec((tm, W_B), lambda i, j: (i, 0)),
                  pl.BlockSpec((tm, W_C), lambda i, j: (i, 0)),
                  pl.BlockSpec((None, W_A + W_B + W_C, tn), lambda i, j: (layer, 0, j)),
                  hspec],
        out_specs=hspec,
        out_shape=jax.ShapeDtypeStruct(h.shape, F32),
        input_output_aliases={4: 0},
        compiler_params=_cparams(("parallel", "parallel"), VMEM_LIMIT),
        name=name,
    )(ya, yb, yc, w, h)


def _kv_out_body(*refs, depth, seq):
    ko_ref, vo_ref = refs[4 * depth:]
    for l in range(depth):
        km_ref, kr_ref, vm_ref, vr_ref = refs[4 * l:4 * l + 4]
        ko_ref[l, 0:N_META] = km_ref[...]
        ko_ref[l, N_META:N_META + seq] = kr_ref[...]
        vo_ref[l, 0:N_META] = vm_ref[...]
        vo_ref[l, N_META:N_META + seq] = vr_ref[...]


def _kv_out(projs, nb, seq, meta_blk, name):
    depth = len(projs)
    t_valid = N_META + seq
    in_specs, args = [], []
    for proj in projs:
        for col in (COL_KA, COL_VA):
            in_specs += [pl.BlockSpec((N_META, HEAD_DIM), lambda b, h, col=col: (meta_blk, col + h)),
                         pl.BlockSpec((seq, HEAD_DIM), lambda b, h, col=col: (b, col + h))]
            args += [proj, proj]
    ospec = pl.BlockSpec((depth, None, None, t_valid, HEAD_DIM), lambda b, h: (0, b, h, 0, 0))
    oshape = jax.ShapeDtypeStruct((depth, nb, H_A, t_valid, HEAD_DIM), F32)
    return pl.pallas_call(
        functools.partial(_kv_out_body, depth=depth, seq=seq),
        grid=(nb, H_A),
        in_specs=in_specs,
        out_specs=[ospec, ospec],
        out_shape=[oshape, oshape],
        compiler_params=_cparams(("parallel", "parallel")),
        name=name,
    )(*args)


def _lambda(lamp_ref, lam_init):
    lp = lamp_ref[...]
    s1 = jnp.sum(lp[0:1] * lp[1:2], axis=-1, keepdims=True)
    s2 = jnp.sum(lp[2:3] * lp[3:4], axis=-1, keepdims=True)
    return jnp.exp(s1) - jnp.exp(s2) + lam_init


def _split_maps(q):
    lane = lax.broadcasted_iota(jnp.int32, q.shape, 1)
    lo = jnp.where(lane < HALF, q, 0.0)
    hi = jnp.where(lane >= HALF, q, 0.0)
    return jnp.concatenate([lo, hi], axis=0)


def _subln(a, sub_ref, lam_init):
    y = a * lax.rsqrt(jnp.mean(a * a, axis=-1, keepdims=True) + EPS) * sub_ref[...]
    return y * (1.0 - lam_init)


ATT_HG = 4


def _attn_prompt_body(q_ref, k_ref, v_ref, km_ref, vm_ref, band_ref, lamp_ref, sub_ref, o_ref,
                      kb_ref, vt_ref, s_ref, p_ref, *, lam_init, nblk):
    lanes = [slice(g * HEAD_DIM, (g + 1) * HEAD_DIM) for g in range(ATT_HG)]

    zpad = jnp.zeros((BLK - N_META, ATT_HG * HEAD_DIM), F32)
    kb_ref[0:BLK, :] = jnp.concatenate([km_ref[...], zpad], axis=0).astype(BF16)
    kb_ref[BLK:, :] = k_ref[...].astype(BF16)
    vm = jnp.concatenate([vm_ref[...], zpad], axis=0)
    for g in range(ATT_HG):
        vt_ref[g, :, 0:BLK] = vm[:, lanes[g]].T.astype(BF16)
        for j in range(1, nblk):
            vt = v_ref[(j - 1) * BLK:j * BLK, lanes[g]].T
            vt_ref[g, :, j * BLK:(j + 1) * BLK] = vt.astype(BF16)

    def attend(qi, nt):
        rows = pl.ds(pl.multiple_of(qi * BLK, BLK), BLK)
        lam = _lambda(lamp_ref, lam_init)
        mrun = [None] * ATT_HG
        lrun = [None] * ATT_HG
        m = [None] * ATT_HG
        qst = [None] * ATT_HG

        def score_tile(g, j):
            if j == 0:
                qst[g] = _split_maps(q_ref[rows, lanes[g]] * (HALF ** -0.5)).T.astype(BF16)
            if j == 0:
                bt = band_ref[g, jnp.where(qi == 0, 4, 5)]
            else:
                bt = band_ref[g, jnp.clip(j - qi + 1, 0, 3)]
            s = jnp.dot(kb_ref[j * BLK:(j + 1) * BLK, lanes[g]], qst[g],
                        preferred_element_type=F32)
            s = s + jnp.concatenate([bt, bt], axis=1)
            s_ref[g, j] = s
            mrun[g] = s if mrun[g] is None else jnp.maximum(mrun[g], s)
            if j == nt - 1:
                m[g] = jnp.max(mrun[g], axis=0, keepdims=True)

        def exp_tile(g, j):
            p = jnp.exp(s_ref[g, j] - m[g])
            lrun[g] = p if lrun[g] is None else lrun[g] + p
            p_ref[g, j * BLK:(j + 1) * BLK, :] = p.astype(BF16)
            if j == nt - 1:
                l = jnp.sum(lrun[g], axis=0, keepdims=True)
                ot = jnp.dot(vt_ref[g, :, 0:nt * BLK], p_ref[g, 0:nt * BLK, :],
                             preferred_element_type=F32) / l
                at = ot[:, :BLK] - lam * ot[:, BLK:]
                at = at * lax.rsqrt(jnp.mean(at * at, axis=0, keepdims=True) + EPS)
                o_ref[rows, lanes[g]] = (at.T * sub_ref[...] * (1.0 - lam_init)
                                         ).astype(o_ref.dtype)

        for step in range(ATT_HG + 1):
            for j in range(nt):
                if step < ATT_HG:
                    score_tile(step, j)
                if step >= 1:
                    exp_tile(step - 1, j)

    def query_tile(qi, carry):
        lo = 0
        for hi in sorted(set(range(2, nblk, 2)) | {nblk}):
            pl.when(jnp.logical_and(qi + 2 > lo, qi + 2 <= hi))(functools.partial(attend, qi, hi))
            lo = hi
        return carry

    lax.fori_loop(0, nblk - 1, query_tile, 0)


def _attn_prompt(proj, band, lamp, sub, nb, seq, meta_blk, lam_init, name):
    nq = seq // BLK
    nblk = nq + 1
    tp = nblk * BLK
    wg = ATT_HG * HEAD_DIM
    body = functools.partial(_attn_prompt_body, lam_init=lam_init, nblk=nblk)
    return pl.pallas_call(
        body,
        grid=(nb, H_A // ATT_HG),
        in_specs=[
            pl.BlockSpec((seq, wg), lambda b, h: (b, COL_QA // ATT_HG + h)),
            pl.BlockSpec((seq, wg), lambda b, h: (b, COL_KA // ATT_HG + h)),
            pl.BlockSpec((seq, wg), lambda b, h: (b, COL_VA // ATT_HG + h)),
            pl.BlockSpec((N_META, wg), lambda b, h: (meta_blk, COL_KA // ATT_HG + h)),
            pl.BlockSpec((N_META, wg), lambda b, h: (meta_blk, COL_VA // ATT_HG + h)),
            pl.BlockSpec((ATT_HG, 6, BLK, BLK), lambda b, h: (h, 0, 0, 0)),
            pl.BlockSpec((4, HALF), lambda b, h: (0, 0)),
            pl.BlockSpec((1, HEAD_DIM), lambda b, h: (0, 0)),
        ],
        out_specs=pl.BlockSpec((seq, wg), lambda b, h: (b, h)),
        out_shape=jax.ShapeDtypeStruct((nb * seq, W_A), BF16),
        scratch_shapes=[pltpu.VMEM((tp, wg), BF16), pltpu.VMEM((ATT_HG, HEAD_DIM, tp), BF16),
                        pltpu.VMEM((ATT_HG, nblk, BLK, 2 * BLK), F32),
                        pltpu.VMEM((ATT_HG, tp, 2 * BLK), BF16)],
        compiler_params=_cparams(("parallel", "parallel"), VMEM_LIMIT),
        name=name,
    )(proj, proj, proj, proj, proj, band, lamp, sub)


def _attn_meta_body(q_ref, k_ref, v_ref, bias_ref, lamp_ref, sub_ref, o_ref, *, lam_init):
    zpad = jnp.zeros((BLK - N_META, HEAD_DIM), F32)
    lam = _lambda(lamp_ref, lam_init)
    for h in range(H_A):
        sl = slice(h * HEAD_DIM, (h + 1) * HEAD_DIM)
        qs = _split_maps(q_ref[:, sl] * (HALF ** -0.5)).astype(BF16)
        kp = jnp.concatenate([k_ref[:, sl], zpad], axis=0).astype(BF16)
        vp = jnp.concatenate([v_ref[:, sl], zpad], axis=0).astype(BF16)
        s = lax.dot_general(qs, kp, (((1,), (1,)), ((), ())),
                            preferred_element_type=F32) + bias_ref[h]
        p = jnp.exp(s - jnp.max(s, axis=1, keepdims=True))
        o = jnp.dot(p.astype(BF16), vp, preferred_element_type=F32) / jnp.sum(p, axis=1,
                                                                              keepdims=True)
        a = o[0:N_META] - lam * o[N_META:]
        o_ref[:, sl] = _subln(a, sub_ref, lam_init).astype(o_ref.dtype)


def _attn_meta(proj, bias, lamp, sub, meta_blk, lam_init, name):
    def spec(col):
        return pl.BlockSpec((N_META, W_A), lambda i: (meta_blk, col))

    return pl.pallas_call(
        functools.partial(_attn_meta_body, lam_init=lam_init),
        grid=(1,),
        in_specs=[spec(0), spec(1), spec(2),
                  pl.BlockSpec((H_A, 2 * N_META, BLK), lambda i: (0, 0, 0)),
                  pl.BlockSpec((4, HALF), lambda i: (0, 0)),
                  pl.BlockSpec((1, HEAD_DIM), lambda i: (0, 0))],
        out_specs=pl.BlockSpec((N_META, W_A), lambda i: (0, 0)),
        out_shape=jax.ShapeDtypeStruct((N_META, W_A), BF16),
        compiler_params=_cparams(("arbitrary",)),
        name=name,
    )(proj, proj, proj, bias, lamp, sub)


PPS = 8


def _attn_sample_body(pt_ref, q_ref, kn_ref, vn_ref, *rest, lam_init, n_pages, n_new):
    kp_refs, vp_refs = rest[:PPS], rest[PPS:2 * PPS]
    bias_ref, lamp_ref, sub_ref, o_ref, qs_ref, m_ref, l_ref, acc_ref = rest[2 * PPS:]
    p = pl.program_id(1)
    n_steps = n_pages // PPS

    @pl.when(p == 0)
    def _():
        for h in range(H_A):
            q = q_ref[0:n_new, h * HEAD_DIM:(h + 1) * HEAD_DIM] * (HALF ** -0.5)
            qs = _split_maps(q)
            pad = jnp.zeros((SROWS - 2 * n_new, HEAD_DIM), F32)
            qs_ref[h] = jnp.concatenate([qs, pad], axis=0).astype(BF16)
        m_ref[...] = jnp.full(m_ref.shape, M_INIT, F32)
        l_ref[...] = jnp.zeros(l_ref.shape, F32)
        acc_ref[...] = jnp.zeros(acc_ref.shape, F32)

    def update(get_k, get_v, bias, n_rep):
        s = jnp.stack([lax.dot_general(qs_ref[h], get_k(h).astype(BF16), (((1,), (1,)), ((), ())),
                                       preferred_element_type=F32) for h in range(H_A)])
        s = s + bias
        m_prev = m_ref[...]
        m_new = jnp.maximum(m_prev, jnp.max(s, axis=-1, keepdims=True))
        alpha = jnp.exp(m_prev - m_new)
        pm = jnp.exp(s - jnp.concatenate([m_new] * n_rep, axis=-1))
        l_ref[...] = alpha * l_ref[...] + jnp.sum(pm, axis=-1, keepdims=True)
        m_ref[...] = m_new
        pb = pm.astype(BF16)
        for h in range(H_A):
            acc_ref[h] = alpha[h] * acc_ref[h] + jnp.dot(pb[h], get_v(h).astype(BF16),
                                                         preferred_element_type=F32)

    @pl.when(p < n_steps)
    def _():
        def pages(refs, h):
            return jnp.concatenate([r[h] for r in refs], axis=0)

        last_idx = jnp.where(p == n_steps - 1, 1, 0)
        bias = jnp.concatenate([bias_ref[0]] * (PPS - 1) + [bias_ref[last_idx]], axis=-1)
        update(functools.partial(pages, kp_refs), functools.partial(pages, vp_refs), bias, PPS)

    @pl.when(p == n_steps)
    def _():
        zpad = jnp.zeros((PAGE_SIZE - SROWS, HEAD_DIM), F32)

        def new_rows(ref, h):
            return jnp.concatenate([ref[:, h * HEAD_DIM:(h + 1) * HEAD_DIM], zpad], axis=0)

        update(functools.partial(new_rows, kn_ref), functools.partial(new_rows, vn_ref),
               bias_ref[2], 1)
        lam = _lambda(lamp_ref, lam_init)
        for h in range(H_A):
            o = acc_ref[h] / l_ref[h]
            a = o[0:n_new] - lam * o[n_new:2 * n_new]
            y = _subln(a, sub_ref, lam_init)
            pad = jnp.zeros((SROWS - n_new, HEAD_DIM), F32)
            o_ref[:, h * HEAD_DIM:(h + 1) * HEAD_DIM] = (
                jnp.concatenate([y, pad], axis=0).astype(o_ref.dtype))


def _attn_sample(page_table, proj, cache_k, cache_v, bias, lamp, sub, layer, row_blk0, n_new,
                 lam_init, name):
    db, n_pages = page_table.shape
    assert n_pages % PPS == 0
    body = functools.partial(_attn_sample_body, lam_init=lam_init, n_pages=n_pages, n_new=n_new)

    def page_spec(t):
        return pl.BlockSpec(
            (None, None, H_A, PAGE_SIZE, HEAD_DIM),
            lambda b, p, pt: (layer, pt[b, jnp.minimum(p * PPS + t, n_pages - 1)], 0, 0, 0))

    page_specs = [page_spec(t) for t in range(PPS)]
    grid_spec = pltpu.PrefetchScalarGridSpec(
        num_scalar_prefetch=1,
        grid=(db, n_pages // PPS + 1),
        in_specs=[
            pl.BlockSpec((SROWS, W_A), lambda b, p, pt: (row_blk0 + b, 0)),
            pl.BlockSpec((SROWS, W_A), lambda b, p, pt: (row_blk0 + b, 1)),
            pl.BlockSpec((SROWS, W_A), lambda b, p, pt: (row_blk0 + b, 2)),
            *page_specs, *page_specs,
            pl.BlockSpec((3, H_A, SROWS, PAGE_SIZE), lambda b, p, pt: (0, 0, 0, 0)),
            pl.BlockSpec((4, HALF), lambda b, p, pt: (0, 0)),
            pl.BlockSpec((1, HEAD_DIM), lambda b, p, pt: (0, 0)),
        ],
        out_specs=pl.BlockSpec((SROWS, W_A), lambda b, p, pt: (b, 0)),
        scratch_shapes=[pltpu.VMEM((H_A, SROWS, HEAD_DIM), BF16),
                        pltpu.VMEM((H_A, SROWS, HEAD_DIM), F32),
                        pltpu.VMEM((H_A, SROWS, HEAD_DIM), F32),
                        pltpu.VMEM((H_A, SROWS, HEAD_DIM), F32)],
    )
    return pl.pallas_call(
        body,
        grid_spec=grid_spec,
        out_shape=jax.ShapeDtypeStruct((db * SROWS, W_A), BF16),
        compiler_params=_cparams(("parallel", "arbitrary"), VMEM_LIMIT),
        name=name,
    )(page_table, proj, proj, proj, *([cache_k] * PPS), *([cache_v] * PPS), bias, lamp, sub)


RET_HG = 12


def _retention_body(q_ref, k_ref, v_ref, g_ref, c2_ref, s2_ref, lg_ref, gn_ref, s0_ref,
                    y_ref, sout_ref, st_ref, *, rows, n_chunks, last_valid):
    c = pl.program_id(2)

    @pl.when(c == 0)
    def _():
        st_ref[...] = s0_ref[...]

    lc = jnp.where(c == n_chunks - 1, float(last_valid), float(BLK))
    ii = lax.broadcasted_iota(jnp.int32, (BLK, BLK), 0).astype(F32)
    jj = lax.broadcasted_iota(jnp.int32, (BLK, BLK), 1).astype(F32)
    rel = ii - jj

    def rows128(x):
        if rows == BLK:
            return x
        return jnp.concatenate([x, jnp.zeros((BLK - rows, x.shape[1]), x.dtype)], axis=0)

    c2 = rows128(c2_ref[...])
    s2 = rows128(s2_ref[...])

    def rot(x):
        return x * c2 + pltpu.roll(x, HALF, 1) * s2

    heads = []
    for j in range(RET_HG):
        sl = slice(j * HEAD_DIM, (j + 1) * HEAD_DIM)
        lgv = lg_ref[:, sl]
        v = rows128(v_ref[:, sl]).astype(BF16)
        qr = rot(rows128(q_ref[:, sl])).astype(BF16)
        kr = rot(rows128(k_ref[:, sl])) * (HEAD_DIM ** -0.5)
        st = st_ref[j]
        sc = lax.dot_general(qr, kr.astype(BF16), (((1,), (1,)), ((), ())),
                             preferred_element_type=F32)
        cross = jnp.dot(qr, st.astype(BF16), preferred_element_type=F32)
        kdec = kr * jnp.where(ii < lc, jnp.exp((lc - 1.0 - ii) * lgv), 0.0)
        st_ref[j] = jnp.exp(lc * lgv) * st + jnp.dot(kdec.T.astype(BF16), v,
                                                     preferred_element_type=F32)
        heads.append((sl, lgv, v, sc, cross))
    for sl, lgv, v, sc, cross in heads:
        decay = jnp.where(rel >= 0, jnp.exp(lgv * jnp.maximum(rel, 0.0)), 0.0)
        o = jnp.dot((sc * decay).astype(BF16), v, preferred_element_type=F32)
        o = o + cross * jnp.exp((ii + 1.0) * lgv)
        y = o[:rows]
        y = y * lax.rsqrt(jnp.mean(y * y, axis=-1, keepdims=True) + EPS) * gn_ref[...]
        g = g_ref[:, sl]
        y_ref[:, sl] = (y * (g * _sigmoid(g))).astype(y_ref.dtype)

    @pl.when(c == n_chunks - 1)
    def _():
        sout_ref[...] = st_ref[...]


def _retention(proj, c2, s2, lgrow, gn, s0, nb, n_chunks, rows, row_blk0, last_valid, name,
               shared_init=False):
    body = functools.partial(_retention_body, rows=rows, n_chunks=n_chunks, last_valid=last_valid)
    wb = RET_HG * HEAD_DIM
    cb = wb // 128

    def in_spec(col0):
        return pl.BlockSpec((rows, wb),
                            lambda b, hg, c: (row_blk0 + b * n_chunks + c, col0 // cb + hg))

    st_block = (None, RET_HG, HEAD_DIM, HEAD_DIM)
    st_spec = pl.BlockSpec(st_block, lambda b, hg, c: (b, hg, 0, 0))
    s0_spec = pl.BlockSpec(st_block, lambda b, hg, c: (0, hg, 0, 0)) if shared_init else st_spec
    return pl.pallas_call(
        body,
        grid=(nb, H_B // RET_HG, n_chunks),
        in_specs=[in_spec(COL_QB), in_spec(COL_KB), in_spec(COL_VB), in_spec(COL_GB),
                  pl.BlockSpec((rows, HEAD_DIM), lambda b, hg, c: (c, 0)),
                  pl.BlockSpec((rows, HEAD_DIM), lambda b, hg, c: (c, 0)),
                  pl.BlockSpec((1, wb), lambda b, hg, c: (0, hg)),
                  pl.BlockSpec((1, HEAD_DIM), lambda b, hg, c: (0, 0)),
                  s0_spec],
        out_specs=[pl.BlockSpec((rows, wb), lambda b, hg, c: (b * n_chunks + c, hg)), st_spec],
        out_shape=[jax.ShapeDtypeStruct((nb * n_chunks * rows, W_B), BF16),
                   jax.ShapeDtypeStruct((nb, H_B, HEAD_DIM, HEAD_DIM), F32)],
        scratch_shapes=[pltpu.VMEM((RET_HG, HEAD_DIM, HEAD_DIM), F32)],
        compiler_params=_cparams(("parallel", "parallel", "arbitrary"), VMEM_LIMIT),
        name=name,
    )(proj, proj, proj, proj, c2, s2, lgrow, gn, s0)


CONV_LANES = 256


def _conv_body(ca0_ref, ca1_ref, cb0_ref, cb1_ref, ctx_ref, w_ref, b_ref, lg_ref, lb_ref,
               y_ref, tail_ref, uc_ref, us_ref, cv_ref, *, rows, n_chunks, last_valid):
    c = pl.program_id(1)

    @pl.when(c == 0)
    def _():
        uc_ref[0:CTX_ROWS, :] = ctx_ref[...]

    ca = jnp.concatenate([ca0_ref[...], ca1_ref[...]], axis=1)
    cb = jnp.concatenate([cb0_ref[...], cb1_ref[...]], axis=1)
    uc_ref[CTX_ROWS:CTX_ROWS + rows, :] = ca * _sigmoid(cb)
    span = CTX_ROWS + rows - 8
    for r in range(1, 8):
        us_ref[r] = uc_ref[pl.ds(r, span), :]
    for cc in range(0, W_C, CONV_LANES):
        cs = slice(cc, cc + CONV_LANES)
        acc = jnp.broadcast_to(b_ref[:, cs], (rows, CONV_LANES))
        for w in range(CONV_W):
            off = CTX_ROWS - (CONV_W - 1) + w
            r, a = off % 8, off - off % 8
            src = uc_ref[a:a + rows, cs] if r == 0 else us_ref[r, a:a + rows, cs]
            acc = acc + src * w_ref[w:w + 1, cs]
        cv_ref[:, cs] = acc
    acc = cv_ref[...]
    mu = jnp.mean(acc, axis=-1, keepdims=True)
    xc = acc - mu
    var = jnp.mean(xc * xc, axis=-1, keepdims=True)
    y = xc * lax.rsqrt(var + EPS) * lg_ref[...] + lb_ref[...]
    y_ref[...] = (y * _sigmoid(y)).astype(y_ref.dtype)

    @pl.when(c == n_chunks - 1)
    def _():
        tail_ref[...] = uc_ref[pl.ds(CTX_ROWS + last_valid - (CONV_W - 1), CONV_W - 1), :]

    nxt = uc_ref[rows:rows + CTX_ROWS, :]
    uc_ref[0:CTX_ROWS, :] = nxt


def _conv(proj, ctx0, conv_w, conv_b, ln_g, ln_b, layer, nb, n_chunks, rows, row_blk0,
          last_valid, name, shared_ctx=False):
    body = functools.partial(_conv_body, rows=rows, n_chunks=n_chunks, last_valid=last_valid)
    half = W_C // 2
    hb = half // 128

    def in_spec(col):
        return pl.BlockSpec((rows, half), lambda b, c: (row_blk0 + b * n_chunks + c, col))

    def par_spec():
        return pl.BlockSpec((None, 1, W_C), lambda b, c: (layer, 0, 0))

    return pl.pallas_call(
        body,
        grid=(nb, n_chunks),
        in_specs=[in_spec(COL_CA // hb), in_spec(COL_CA // hb + 1),
                  in_spec(COL_CB // hb), in_spec(COL_CB // hb + 1),
                  pl.BlockSpec((None, CTX_ROWS, W_C),
                               lambda b, c: (0 if shared_ctx else b, 0, 0)),
                  pl.BlockSpec((None, CONV_W, W_C), lambda b, c: (layer, 0, 0)),
                  par_spec(), par_spec(), par_spec()],
        out_specs=[pl.BlockSpec((rows, W_C), lambda b, c: (b * n_chunks + c, 0)),
                   pl.BlockSpec((None, CONV_W - 1, W_C), lambda b, c: (b, 0, 0))],
        out_shape=[jax.ShapeDtypeStruct((nb * n_chunks * rows, W_C), BF16),
                   jax.ShapeDtypeStruct((nb, CONV_W - 1, W_C), F32)],
        scratch_shapes=[pltpu.VMEM((CTX_ROWS + rows, W_C), F32),
                        pltpu.VMEM((8, CTX_ROWS + rows - 8, W_C), F32),
                        pltpu.VMEM((rows, W_C), F32)],
        compiler_params=_cparams(("parallel", "arbitrary"), VMEM_LIMIT),
        name=name,
    )(proj, proj, proj, proj, ctx0, conv_w, conv_b, ln_g, ln_b)


def _t5_bias(rel_bias, dist):
    n = jnp.maximum(dist, 0)
    nf = jnp.maximum(n, 1).astype(F32)
    large = REL_EXACT + (jnp.log(nf / REL_EXACT) / math.log(REL_MAX_DIST / REL_EXACT)
                         * (N_BUCKETS - REL_EXACT)).astype(jnp.int32)
    bucket = jnp.where(n < REL_EXACT, n, jnp.minimum(large, N_BUCKETS - 1))
    onehot = (bucket[..., None] == jnp.arange(N_BUCKETS, dtype=jnp.int32)).astype(F32)
    bias = jnp.einsum("...k,kh->h...", onehot, rel_bias.astype(F32),
                      precision=lax.Precision.HIGHEST)
    return jnp.where(dist[None] >= 0, bias, NEG_INF)


def _rotary_tables(pos):
    inv = ROPE_BASE ** (-jnp.arange(HALF, dtype=F32) / HALF)
    ang = pos.astype(F32)[:, None] * inv[None, :]
    cos, sin = jnp.cos(ang), jnp.sin(ang)
    return jnp.concatenate([cos, cos], axis=1), jnp.concatenate([-sin, sin], axis=1)


def kernel(x_prompt, x_sample, cache_k, cache_v, state_ret, state_conv, page_table, meta, rel_bias,
           norm_mix, w_in, lam_q1, lam_k1, lam_q2, lam_k2, subln_a, gn_b, conv_w, conv_b, conv_ln_g,
           conv_ln_b, w_out, norm_ffn, w_gate, w_up, w_down, norm_final):
    nb, seq = x_prompt.shape[:2]
    db, n_new = x_sample.shape[:2]
    depth = w_in.shape[0]
    n_pages = page_table.shape[1]
    past = n_pages * PAGE_SIZE
    assert seq % BLK == 0 and N_META == SROWS and BLK >= REL_MAX_DIST and PAGE_SIZE >= REL_MAX_DIST
    real = nb * seq
    samp0 = real + BLK
    m_all = samp0 + db * SROWS
    meta_blk = real // SROWS
    s_blk0 = samp0 // SROWS
    d_ff = w_gate.shape[2]

    tm = m_all // 6
    tr = m_all // 24

    h = jnp.concatenate([x_prompt.reshape(real, D_MODEL), meta.astype(F32),
                         jnp.zeros((BLK - N_META, D_MODEL), F32),
                         jnp.pad(x_sample, ((0, 0), (0, SROWS - n_new), (0, 0))
                                 ).reshape(db * SROWS, D_MODEL)], axis=0)

    kk = jnp.arange(BLK, dtype=jnp.int32)[:, None]
    ii = jnp.arange(BLK, dtype=jnp.int32)[None, :]
    is_meta = kk < N_META
    far_d = jnp.full((BLK, BLK), REL_MAX_DIST, jnp.int32)
    band = _t5_bias(rel_bias, jnp.stack([
        far_d, BLK + ii - kk, ii - kk, jnp.full((BLK, BLK), -1, jnp.int32),
        jnp.where(is_meta, N_META + ii - kk, -1), jnp.where(is_meta, far_d, -1)]))
    rm = jnp.arange(2 * N_META, dtype=jnp.int32)[:, None] % N_META
    jm = jnp.arange(BLK, dtype=jnp.int32)[None, :]
    bias_m = _t5_bias(rel_bias, jnp.where(jm < N_META, rm - jm, -1))
    r_new = jnp.arange(SROWS, dtype=jnp.int32) % n_new
    jk = jnp.arange(PAGE_SIZE, dtype=jnp.int32)[None, :]
    dist_s = jnp.stack([jnp.full((SROWS, PAGE_SIZE), REL_MAX_DIST, jnp.int32),
                        PAGE_SIZE + r_new[:, None] - jk,
                        jnp.where(jk < n_new, r_new[:, None] - jk, -1)])
    bias_s = jnp.transpose(_t5_bias(rel_bias, dist_s), (1, 0, 2, 3))
    cache_kh = jnp.transpose(cache_k, (0, 1, 3, 2, 4))
    cache_vh = jnp.transpose(cache_v, (0, 1, 3, 2, 4))
    c2_m, s2_m = _rotary_tables(jnp.arange(N_META, dtype=jnp.int32))
    c2_p, s2_p = _rotary_tables(N_META + jnp.arange(seq, dtype=jnp.int32))
    c2_s, s2_s = _rotary_tables(past + jnp.arange(SROWS, dtype=jnp.int32))
    log_gamma = jnp.log1p(-jnp.exp2(-5.0 - jnp.arange(H_B, dtype=F32)))
    lgrow = jnp.repeat(log_gamma, HEAD_DIM)[None, :]

    zero_state = jnp.zeros((1, H_B, HEAD_DIM, HEAD_DIM), F32)
    zero_ctx = jnp.zeros((1, CTX_ROWS, W_C), F32)
    ctx_pad = ((0, 0), (CTX_ROWS - (CONV_W - 1), 0), (0, 0))

    def small_rows(y_meta, y_sample):
        return jnp.concatenate([y_meta, jnp.zeros((BLK - N_META, y_meta.shape[1]), y_meta.dtype),
                                y_sample], axis=0)

    conv_b3 = conv_b[:, None, :]
    ln_g3 = conv_ln_g[:, None, :]
    ln_b3 = conv_ln_b[:, None, :]

    projs, ret_p, conv_p, k_s, v_s, ret_s, conv_s = [], [], [], [], [], [], []
    for l in range(depth):
        lam_init = 0.8 - 0.6 * math.exp(-0.3 * l)
        lamp = jnp.stack([lam_q1[l], lam_k1[l], lam_q2[l], lam_k2[l]]).astype(F32)
        sub = subln_a[l][None, :]
        gn = gn_b[l][None, :]

        n1 = _rmsnorm(h, norm_mix[l], BF16, tr, f"norm_mix{l}")
        proj = _matmul(n1, w_in, l, tm, 512, f"in_proj{l}")

        ya_m = _attn_meta(proj, bias_m, lamp, sub, meta_blk, lam_init, f"attn_meta{l}")
        yb_m, st_m = _retention(proj, c2_m, s2_m, lgrow, gn, zero_state, 1, 1, N_META, meta_blk,
                                N_META, f"ret_meta{l}")
        yc_m, tail_m = _conv(proj, zero_ctx, conv_w, conv_b3, ln_g3, ln_b3, l, 1, 1, N_META,
                             meta_blk, N_META, f"conv_meta{l}")

        ya_p = _attn_prompt(proj, band, lamp, sub, nb, seq, meta_blk, lam_init, f"attn_prompt{l}")
        ya_s = _attn_sample(page_table, proj, cache_kh, cache_vh, bias_s, lamp, sub, l, s_blk0,
                            n_new, lam_init, f"attn_sample{l}")
        yb_p, st_p = _retention(proj, c2_p, s2_p, lgrow, gn, st_m, nb, seq // BLK, BLK, 0, BLK,
                                f"ret_prompt{l}", shared_init=True)
        yb_s, st_s = _retention(proj, c2_s, s2_s, lgrow, gn, state_ret[l], db, 1, SROWS,
                                s_blk0, n_new, f"ret_sample{l}")
        yc_p, tail_p = _conv(proj, jnp.pad(tail_m, ctx_pad), conv_w, conv_b3, ln_g3, ln_b3, l, nb,
                             seq // BLK, BLK, 0, BLK, f"conv_prompt{l}", shared_ctx=True)
        yc_s, tail_s = _conv(proj, jnp.pad(state_conv[l], ctx_pad), conv_w, conv_b3, ln_g3, ln_b3,
                             l, db, 1, SROWS, s_blk0, n_new, f"conv_sample{l}")

        h = _out_proj(ya_p, yb_p, yc_p, w_out, l, h, 0, real // 8, 512, f"out_proj_p{l}")
        n_small = m_all - real
        h = _out_proj(small_rows(ya_m, ya_s), small_rows(yb_m, yb_s), small_rows(yc_m, yc_s),
                      w_out, l, h, real // n_small, n_small, 256, f"out_proj_s{l}")
        n2 = _rmsnorm(h, norm_ffn[l], BF16, tr, f"norm_ffn{l}")
        hid = _ffn_up(n2, w_gate, w_up, l, m_all // 4, 256, f"ffn_up{l}")
        for kb in range(2):
            h = _matmul(hid, w_down, l, tm, 256, f"ffn_down{l}_{kb}", res=h,
                        tk=d_ff // 2, kblk=kb)

        projs.append(proj)
        kvs = proj[samp0:, W_A:3 * W_A].reshape(db, SROWS, 2, H_A, HEAD_DIM)[:, :n_new]
        k_s.append(kvs[:, :, 0])
        v_s.append(kvs[:, :, 1])
        ret_p.append(st_p)
        ret_s.append(st_s)
        conv_p.append(tail_p)
        conv_s.append(tail_s)

    y_prompt = _rmsnorm(h, norm_final, F32, 2 * BLK, "norm_final_p", rows=real).reshape(
        nb, seq, D_MODEL)
    hs = _rmsnorm(h[samp0:], norm_final, F32, db * SROWS, "norm_final_s")
    y_sample = hs.reshape(db, SROWS, D_MODEL)[:, :n_new]
    k_hm, v_hm = _kv_out(projs, nb, seq, meta_blk, "kv_out")
    k_prompt = jnp.transpose(k_hm, (0, 1, 3, 2, 4))
    v_prompt = jnp.transpose(v_hm, (0, 1, 3, 2, 4))
    return (y_prompt, y_sample, k_prompt, v_prompt, jnp.stack(ret_p), jnp.stack(conv_p),
            jnp.stack(k_s), jnp.stack(v_s), jnp.stack(ret_s), jnp.stack(conv_s))
```
